```python
import jax, jax.numpy as jnp
from jax import lax
import numpy as np

D_MODEL = 2048
BATCH = 2
SEQ = 4096
DEPTH = 4

GRID_W = 64
CTX_LEN = 256
N_MIXERS = 3
POOL_GROUPS = 4
POOL_WINDOWS = (2, 4, 8, 16)
POOL_GC = D_MODEL // POOL_GROUPS
HEAD_DIM = 128
GQA_HEADS = D_MODEL // HEAD_DIM
GQA_KV_HEADS = 4
GQA_GROUP = GQA_HEADS // GQA_KV_HEADS
Q_BLOCK = 128
ROPE_THETA = 10000.0
ROPE_AXIS_DIM = HEAD_DIM // 2
NAT_HEADS = D_MODEL // HEAD_DIM
NAT_KH = 8
NAT_KW = 16
FFN_HIDDEN = -(-8 * D_MODEL // (3 * 256)) * 256
NORM_EPS = 1e-6

kernel_name = "hybrid_pool_gqa_natten_dit_block"


def _rmsnorm(x, g):
    xf = x.astype(jnp.float32)
    y = xf * lax.rsqrt(jnp.mean(xf * xf, axis=-1, keepdims=True) + NORM_EPS)
    return y.astype(x.dtype) * g


def _modulate(h, shift, scale):
    return h * (1 + scale) + shift


def _ada(cvec, w, b):
    return jnp.split(jax.nn.silu(cvec) @ w + b, 6, axis=-1)


def _swiglu(h, w1, w3, w2):
    return (jax.nn.silu(h @ w1) * (h @ w3)) @ w2


def _centred_window_mean(x, w):
    b, n, ch = x.shape
    cs = lax.cumsum(x.astype(jnp.float32), axis=1)
    cs = jnp.concatenate([jnp.zeros((b, 1, ch), jnp.float32), cs], axis=1)
    t = jnp.arange(n)
    lo = jnp.clip(t - w // 2, 0, n)
    hi = jnp.clip(t + w - w // 2, 0, n)
    cnt = (hi - lo).astype(jnp.float32)
    return ((cs[:, hi] - cs[:, lo]) / cnt[None, :, None]).astype(x.dtype)


def _pool_mixer(h, w, ls):
    b, n, _ = h.shape
    g = h.reshape(b, n, POOL_GROUPS, POOL_GC)
    pooled = jnp.stack([_centred_window_mean(g[:, :, k], POOL_WINDOWS[k]) for k in range(POOL_GROUPS)], axis=2) - g
    return jnp.einsum('bngc,gcd->bngd', pooled, w).reshape(b, n, D_MODEL) * ls


def _axial_rope(n):
    t = jnp.arange(n)
    row = (t // GRID_W).astype(jnp.float32)
    col = (t % GRID_W).astype(jnp.float32)
    inv = ROPE_THETA ** (-jnp.arange(0, ROPE_AXIS_DIM, 2, dtype=jnp.float32) / ROPE_AXIS_DIM)
    ang = jnp.concatenate([row[:, None] * inv, col[:, None] * inv], axis=-1)
    return jnp.cos(ang), jnp.sin(ang)


def _apply_rope(x, cos, sin):
    xf = x.astype(jnp.float32).reshape(x.shape[:-1] + (HEAD_DIM // 2, 2))
    x1, x2 = xf[..., 0], xf[..., 1]
    c = cos[None, :, None]
    s = sin[None, :, None]
    y = jnp.stack([x1 * c - x2 * s, x1 * s + x2 * c], axis=-1)
    return y.reshape(x.shape).astype(x.dtype)


def _gqa_attend(q, k, v):
    s = jnp.einsum('bqkgd,bskd->bkgqs', q, k, preferred_element_type=jnp.float32) * (HEAD_DIM ** -0.5)
    p = jax.nn.softmax(s, axis=-1).astype(v.dtype)
    return jnp.einsum('bkgqs,bskd->bqkgd', p, v)


def _gqa_mixer(h_ctx, h_lat, wq, wk, wv, wo, gq, gk, need_ctx):
    def proj(h):
        b, n, _ = h.shape
        q = _rmsnorm((h @ wq).reshape(b, n, GQA_HEADS, HEAD_DIM), gq)
        k = _rmsnorm((h @ wk).reshape(b, n, GQA_KV_HEADS, HEAD_DIM), gk)
        v = (h @ wv).reshape(b, n, GQA_KV_HEADS, HEAD_DIM)
        return q, k, v

    qc, kc, vc = proj(h_ctx)
    ql, kl, vl = proj(h_lat)
    b, n, _ = h_lat.shape
    cos, sin = _axial_rope(n)
    ql = _apply_rope(ql, cos, sin)
    kl = _apply_rope(kl, cos, sin)
    k_all = jnp.concatenate([kc, kl], axis=1)
    v_all = jnp.concatenate([vc, vl], axis=1)
    nb = n // Q_BLOCK
    qb = ql.reshape(b, nb, Q_BLOCK, GQA_KV_HEADS, GQA_GROUP, HEAD_DIM).transpose(1, 0, 2, 3, 4, 5)
    ob = lax.map(lambda q: _gqa_attend(q, k_all, v_all), qb)
    y_lat = ob.transpose(1, 0, 2, 3, 4, 5).reshape(b, n, D_MODEL) @ wo
    y_ctx = None
    if need_ctx:
        bc, nc, _ = h_ctx.shape
        oc = _gqa_attend(qc.reshape(bc, nc, GQA_KV_HEADS, GQA_GROUP, HEAD_DIM), kc, vc)
        y_ctx = oc.reshape(bc, nc, D_MODEL) @ wo
    return y_ctx, y_lat


def _mha_attend(q, k, v):
    s = jnp.einsum('bqhd,bshd->bhqs', q, k, preferred_element_type=jnp.float32) * (HEAD_DIM ** -0.5)
    p = jax.nn.softmax(s, axis=-1).astype(v.dtype)
    return jnp.einsum('bhqs,bshd->bqhd', p, v)


def _nat_mixer(h_ctx, h_lat, wq, wk, wv, wo, rpb, need_ctx):
    def proj(h):
        b, n, _ = h.shape
        return tuple((h @ w).reshape(b, n, NAT_HEADS, HEAD_DIM) for w in (wq, wk, wv))

    qc, kc, vc = proj(h_ctx)
    ql, kl, vl = proj(h_lat)
    b, n, _ = h_lat.shape
    rows = n // GRID_W
    kh = min(NAT_KH, rows)
    kw = NAT_KW
    nk = kh * kw
    qg = ql.reshape(b, rows, GRID_W, NAT_HEADS, HEAD_DIM)
    kg = kl.reshape(b, rows, GRID_W, NAT_HEADS, HEAD_DIM)
    vg = vl.reshape(b, rows, GRID_W, NAT_HEADS, HEAD_DIM)
    cq = jnp.arange(GRID_W)
    cstart = jnp.clip(cq - kw // 2, 0, GRID_W - kw)
    col_idx = cstart[:, None] + jnp.arange(kw)[None, :]
    dc = col_idx - cq[:, None] + (NAT_KW - 1)
    scale = HEAD_DIM ** -0.5

    def row_fn(args):
        r, q_row = args
        rstart = jnp.clip(r - kh // 2, 0, rows - kh)
        kb = lax.dynamic_slice_in_dim(kg, rstart, kh, axis=1)
        vb = lax.dynamic_slice_in_dim(vg, rstart, kh, axis=1)
        kn = jnp.take(kb, col_idx, axis=2).transpose(0, 2, 1, 3, 4, 5).reshape(b, GRID_W, nk, NAT_HEADS, HEAD_DIM)
        vn = jnp.take(vb, col_idx, axis=2).transpose(0, 2, 1, 3, 4, 5).reshape(b, GRID_W, nk, NAT_HEADS, HEAD_DIM)
        dr = rstart + jnp.arange(kh) - r + (NAT_KH - 1)
        bias = rpb[:, dr[None, :, None], dc[:, None, :]].reshape(NAT_HEADS, GRID_W, nk).astype(jnp.float32)
        s_loc = jnp.einsum('bqhd,bqkhd->bhqk', q_row, kn, preferred_element_type=jnp.float32) * scale + bias[None]
        s_ctx = jnp.einsum('bqhd,bshd->bhqs', q_row, kc, preferred_element_type=jnp.float32) * scale
        p = jax.nn.softmax(jnp.concatenate([s_loc, s_ctx], axis=-1), axis=-1).astype(vn.dtype)
        return (jnp.einsum('bhqk,bqkhd->bqhd', p[..., :nk], vn)
                + jnp.einsum('bhqs,bshd->bqhd', p[..., nk:], vc))

    o = lax.map(row_fn, (jnp.arange(rows), qg.transpose(1, 0, 2, 3, 4)))
    y_lat = o.transpose(1, 0, 2, 3, 4).reshape(b, n, D_MODEL) @ wo
    y_ctx = None
    if need_ctx:
        bc, nc, _ = h_ctx.shape
        y_ctx = _mha_attend(qc, kc, vc).reshape(bc, nc, D_MODEL) @ wo
    return y_ctx, y_lat


def setup_inputs(seed: int = 0) -> dict:
    key = jax.random.key(seed)
    ks = iter(jax.random.split(key, 32))

    def nrm(shape, s):
        return jax.random.normal(next(ks), shape, jnp.float32) * s

    n_a = (DEPTH + 2) // 3
    n_b = (DEPTH + 1) // 3
    n_c = DEPTH // 3
    D, F = D_MODEL, FFN_HIDDEN
    return {
        "x": nrm((BATCH, SEQ, D), 1.0),
        "c": nrm((BATCH, D), 1.0),
        "ctx": nrm((BATCH, CTX_LEN, D), 1.0),
        "c_ctx": nrm((D,), 1.0),
        "ada_w": nrm((DEPTH, D, 6 * D), 0.5 * D ** -0.5),
        "ada_b": nrm((DEPTH, 6 * D), 0.02),
        "norm_g": 1.0 + nrm((DEPTH, 2, D), 0.02),
        "ffn_w1": nrm((DEPTH, D, F), D ** -0.5),
        "ffn_w3": nrm((DEPTH, D, F), D ** -0.5),
        "ffn_w2": nrm((DEPTH, F, D), F ** -0.5),
        "pool_w": nrm((n_a, POOL_GROUPS, POOL_GC, POOL_GC), POOL_GC ** -0.5),
        "pool_ls": 1.0 + nrm((n_a, D), 0.1),
        "gqa_wq": nrm((n_b, D, GQA_HEADS * HEAD_DIM), D ** -0.5),
        "gqa_wk": nrm((n_b, D, GQA_KV_HEADS * HEAD_DIM), D ** -0.5),
        "gqa_wv": nrm((n_b, D, GQA_KV_HEADS * HEAD_DIM), D ** -0.5),
        "gqa_wo": nrm((n_b, GQA_HEADS * HEAD_DIM, D), (GQA_HEADS * HEAD_DIM) ** -0.5),
        "gqa_qn": 1.0 + nrm((n_b, HEAD_DIM), 0.02),
        "gqa_kn": 1.0 + nrm((n_b, HEAD_DIM), 0.02),
        "nat_wq": nrm((n_c, D, NAT_HEADS * HEAD_DIM), D ** -0.5),
        "nat_wk": nrm((n_c, D, NAT_HEADS * HEAD_DIM), D ** -0.5),
        "nat_wv": nrm((n_c, D, NAT_HEADS * HEAD_DIM), D ** -0.5),
        "nat_wo": nrm((n_c, NAT_HEADS * HEAD_DIM, D), (NAT_HEADS * HEAD_DIM) ** -0.5),
        "nat_rpb": nrm((n_c, NAT_HEADS, 2 * NAT_KH - 1, 2 * NAT_KW - 1), 0.5),
        "final_g": 1.0 + nrm((D,), 0.02),
    }


def reference(x, c, ctx, c_ctx, ada_w, ada_b, norm_g, ffn_w1, ffn_w3, ffn_w2, pool_w, pool_ls,
              gqa_wq, gqa_wk, gqa_wv, gqa_wo, gqa_qn, gqa_kn,
              nat_wq, nat_wk, nat_wv, nat_wo, nat_rpb, final_g):
    xl, xc = x, ctx
    for i in range(DEPTH):
        kind, j = i % N_MIXERS, i // N_MIXERS
        need_ctx = i < DEPTH - 1
        sh1l, sc1l, g1l, sh2l, sc2l, g2l = [m[:, None, :] for m in _ada(c, ada_w[i], ada_b[i])]
        sh1c, sc1c, g1c, sh2c, sc2c, g2c = _ada(c_ctx, ada_w[i], ada_b[i])
        hl = _modulate(_rmsnorm(xl, norm_g[i, 0]), sh1l, sc1l)
        hc = _modulate(_rmsnorm(xc, norm_g[i, 0]), sh1c, sc1c)
        if kind == 0:
            yl = _pool_mixer(hl, pool_w[j], pool_ls[j])
            yc = _pool_mixer(hc, pool_w[j], pool_ls[j]) if need_ctx else None
        elif kind == 1:
            yc, yl = _gqa_mixer(hc, hl, gqa_wq[j], gqa_wk[j], gqa_wv[j], gqa_wo[j], gqa_qn[j], gqa_kn[j], need_ctx)
        else:
            yc, yl = _nat_mixer(hc, hl, nat_wq[j], nat_wk[j], nat_wv[j], nat_wo[j], nat_rpb[j], need_ctx)
        xl = xl + g1l * yl
        hl = _modulate(_rmsnorm(xl, norm_g[i, 1]), sh2l, sc2l)
        xl = xl + g2l * _swiglu(hl, ffn_w1[i], ffn_w3[i], ffn_w2[i])
        if need_ctx:
            xc = xc + g1c * yc
            hc = _modulate(_rmsnorm(xc, norm_g[i, 1]), sh2c, sc2c)
            xc = xc + g2c * _swiglu(hc, ffn_w1[i], ffn_w3[i], ffn_w2[i])
    return _rmsnorm(xl, final_g)
```

```python
import functools

import jax
import jax.numpy as jnp
from jax import lax
from jax.experimental import pallas as pl
from jax.experimental.pallas import tpu as pltpu

D_MODEL = 2048
BATCH = 2
SEQ = 4096
DEPTH = 4
GRID_W = 64
CTX_LEN = 256
N_MIXERS = 3
POOL_GROUPS = 4
POOL_WINDOWS = (2, 4, 8, 16)
POOL_GC = D_MODEL // POOL_GROUPS
HEAD_DIM = 128
GQA_HEADS = D_MODEL // HEAD_DIM
GQA_KV_HEADS = 4
GQA_GROUP = GQA_HEADS // GQA_KV_HEADS
ROPE_THETA = 10000.0
ROPE_AXIS_DIM = HEAD_DIM // 2
NAT_HEADS = D_MODEL // HEAD_DIM
NAT_KH = 8
NAT_KW = 16
FFN_HIDDEN = -(-8 * D_MODEL // (3 * 256)) * 256
NORM_EPS = 1e-6

ROWS_LAT = BATCH * SEQ
ROWS_CTX = BATCH * CTX_LEN
ROWS_ALL = ROWS_LAT + ROWS_CTX
GRID_H = SEQ // GRID_W
ATTN_SCALE = HEAD_DIM ** -0.5
MASK_VALUE = -1e30
POOL_HALO = 8

VMEM_LIMIT_BYTES = 56 * 1024 * 1024

F32 = jnp.float32
BF16 = jnp.bfloat16


def _params(*sem):
    return pltpu.CompilerParams(dimension_semantics=sem, vmem_limit_bytes=VMEM_LIMIT_BYTES)


def _dot(a, b):
    return jnp.dot(a, b, preferred_element_type=F32)


def _dot_nt(a, b):
    return lax.dot_general(a, b, (((1,), (1,)), ((), ())), preferred_element_type=F32)


def _silu(a):
    return a * (1.0 / (1.0 + jnp.exp(-a)))


def _rms(x, g):
    return x * lax.rsqrt(jnp.mean(x * x, axis=-1, keepdims=True) + NORM_EPS) * g


def _norm_mod(x, g, shift, scale):
    return _rms(x, g) * (1.0 + scale) + shift


def _mod_row(t, tm):
    return jnp.where(t < ROWS_LAT // tm, t // (SEQ // tm), BATCH)


def _ada_kernel(c_ref, w_ref, b_ref, o_ref):
    s = _silu(c_ref[...])
    o_ref[...] = _dot(s.astype(BF16), w_ref[...].astype(BF16)) + b_ref[...]


def _ada_all(c8, ada_w, ada_b):
    tn = 1024
    return pl.pallas_call(
        _ada_kernel,
        grid=(DEPTH, 6 * D_MODEL // tn),
        in_specs=[
            pl.BlockSpec((8, D_MODEL), lambda l, n: (0, 0)),
            pl.BlockSpec((None, D_MODEL, tn), lambda l, n: (l, 0, n)),
            pl.BlockSpec((None, 1, tn), lambda l, n: (l, 0, n)),
        ],
        out_specs=pl.BlockSpec((None, 8, tn), lambda l, n: (l, 0, n)),
        out_shape=jax.ShapeDtypeStruct((DEPTH, 8, 6 * D_MODEL), F32),
        compiler_params=_params("parallel", "parallel"),
        name="ada",
    )(c8, ada_w, ada_b.reshape(DEPTH, 1, 6 * D_MODEL))


def _pool_kernel(x_ref, xp_ref, xn_ref, mod_ref, g_ref, w_ref, ls_ref, o_ref, hext_ref, *, tm):
    t = pl.program_id(0)
    shift, scale, gate = mod_ref[0:1, :], mod_ref[1:2, :], mod_ref[2:3, :]
    g = g_ref[...]

    n_lat_tiles = ROWS_LAT // tm
    is_lat = t < n_lat_tiles
    tiles_in_seq = jnp.where(is_lat, SEQ // tm, CTX_LEN // tm)
    pos_tile = jnp.where(is_lat, t % (SEQ // tm), (t - n_lat_tiles) % (CTX_LEN // tm))
    first = pos_tile == 0
    last = pos_tile == tiles_in_seq - 1

    x = x_ref[...]
    hext_ref[POOL_HALO:POOL_HALO + tm, :] = _norm_mod(x, g, shift, scale)
    hext_ref[0:POOL_HALO, :] = jnp.where(first, 0.0, _norm_mod(xp_ref[...], g, shift, scale))
    hext_ref[POOL_HALO + tm:, :] = jnp.where(last, 0.0, _norm_mod(xn_ref[...], g, shift, scale))

    pos = pos_tile * tm + lax.broadcasted_iota(jnp.int32, (tm, 1), 0)
    n = tiles_in_seq * tm
    for k in range(POOL_GROUPS):
        w = POOL_WINDOWS[k]
        lanes = slice(k * POOL_GC, (k + 1) * POOL_GC)
        acc = hext_ref[POOL_HALO - w // 2:POOL_HALO - w // 2 + tm, lanes]
        for d in range(1 - w // 2, w - w // 2):
            acc = acc + hext_ref[POOL_HALO + d:POOL_HALO + d + tm, lanes]
        cnt = jnp.minimum(pos + (w - w // 2), n) - jnp.maximum(pos - w // 2, 0)
        pooled = acc / cnt.astype(F32) - hext_ref[POOL_HALO:POOL_HALO + tm, lanes]
        y = _dot(pooled.astype(BF16), w_ref[k]) * ls_ref[:, lanes]
        o_ref[:, lanes] = x[:, lanes] + gate[:, lanes] * y


def _pool_layer(x, mods_l, g, w, ls, n_rows):
    tm = 256
    nb8 = x.shape[0] // POOL_HALO
    return pl.pallas_call(
        functools.partial(_pool_kernel, tm=tm),
        grid=(n_rows // tm,),
        in_specs=[
            pl.BlockSpec((tm, D_MODEL), lambda t: (t, 0)),
            pl.BlockSpec((POOL_HALO, D_MODEL), lambda t: (jnp.maximum(t * (tm // POOL_HALO) - 1, 0), 0)),
            pl.BlockSpec((POOL_HALO, D_MODEL), lambda t: (jnp.minimum((t + 1) * (tm // POOL_HALO), nb8 - 1), 0)),
            pl.BlockSpec((None, 6, D_MODEL), lambda t: (_mod_row(t, tm), 0, 0)),
            pl.BlockSpec((1, D_MODEL), lambda t: (0, 0)),
            pl.BlockSpec((POOL_GROUPS, POOL_GC, POOL_GC), lambda t: (0, 0, 0)),
            pl.BlockSpec((1, D_MODEL), lambda t: (0, 0)),
        ],
        out_specs=pl.BlockSpec((tm, D_MODEL), lambda t: (t, 0)),
        out_shape=jax.ShapeDtypeStruct((n_rows, D_MODEL), F32),
        scratch_shapes=[pltpu.VMEM((tm + 2 * POOL_HALO, D_MODEL), F32)],
        compiler_params=_params("parallel"),
        name="pool_layer",
    )(x, x, x, mods_l, g.reshape(1, D_MODEL), w, ls.reshape(1, D_MODEL))


def _ffn_kernel(x_ref, mod_ref, g_ref, w1_ref, w3_ref, w2_ref, fg_ref, o_ref, h_ref, acc_ref, *, nf, final):
    f = pl.program_id(1)

    @pl.when(f == 0)
    def _():
        h = _norm_mod(x_ref[...], g_ref[...], mod_ref[3:4, :], mod_ref[4:5, :])
        h_ref[...] = h.astype(BF16)
        acc_ref[...] = jnp.zeros_like(acc_ref)

    h = h_ref[...]
    a = _dot(h, w1_ref[...])
    b = _dot(h, w3_ref[...])
    acc_ref[...] += _dot((_silu(a) * b).astype(BF16), w2_ref[...])

    @pl.when(f == nf - 1)
    def _():
        y = x_ref[...] + mod_ref[5:6, :] * acc_ref[...]
        if final:
            y = _rms(y, fg_ref[...])
        o_ref[...] = y


def _ffn_layer(x, mods_l, g, w1, w3, w2, final_g, n_rows, final):
    tm, tf = 512, 512
    nf = FFN_HIDDEN // tf
    return pl.pallas_call(
        functools.partial(_ffn_kernel, nf=nf, final=final),
        grid=(n_rows // tm, nf),
        in_specs=[
            pl.BlockSpec((tm, D_MODEL), lambda t, f: (t, 0)),
            pl.BlockSpec((None, 6, D_MODEL), lambda t, f: (_mod_row(t, tm), 0, 0)),
            pl.BlockSpec((1, D_MODEL), lambda t, f: (0, 0)),
            pl.BlockSpec((D_MODEL, tf), lambda t, f: (0, f)),
            pl.BlockSpec((D_MODEL, tf), lambda t, f: (0, f)),
            pl.BlockSpec((tf, D_MODEL), lambda t, f: (f, 0)),
            pl.BlockSpec((1, D_MODEL), lambda t, f: (0, 0)),
        ],
        out_specs=pl.BlockSpec((tm, D_MODEL), lambda t, f: (t, 0)),
        out_shape=jax.ShapeDtypeStruct((n_rows, D_MODEL), F32),
        scratch_shapes=[pltpu.VMEM((tm, D_MODEL), BF16), pltpu.VMEM((tm, D_MODEL), F32)],
        compiler_params=_params("parallel", "arbitrary"),
        name="ffn_layer",
    )(x, mods_l, g.reshape(1, D_MODEL), w1, w3, w2, final_g.reshape(1, D_MODEL))


def _attn_out_kernel(ol_ref, oc_ref, x_ref, mod_ref, wo_ref, o_ref, *, tm):
    t = pl.program_id(0)
    o = jnp.where(t < ROWS_LAT // tm, ol_ref[...], oc_ref[...])
    o_ref[...] = x_ref[...] + mod_ref[2:3, :] * _dot(o, wo_ref[...])


def _attn_out_layer(o_lat, o_ctx, x, mods_l, wo):
    tm = 256
    n_lat = ROWS_LAT // tm
    return pl.pallas_call(
        functools.partial(_attn_out_kernel, tm=tm),
        grid=(ROWS_ALL // tm,),
        in_specs=[
            pl.BlockSpec((tm, D_MODEL), lambda t: (jnp.minimum(t, n_lat - 1), 0)),
            pl.BlockSpec((tm, D_MODEL), lambda t: (jnp.maximum(t - n_lat, 0), 0)),
            pl.BlockSpec((tm, D_MODEL), lambda t: (t, 0)),
            pl.BlockSpec((None, 6, D_MODEL), lambda t: (_mod_row(t, tm), 0, 0)),
            pl.BlockSpec((D_MODEL, D_MODEL), lambda t: (0, 0)),
        ],
        out_specs=pl.BlockSpec((tm, D_MODEL), lambda t: (t, 0)),
        out_shape=jax.ShapeDtypeStruct((ROWS_ALL, D_MODEL), F32),
        compiler_params=_params("parallel"),
        name="attn_out",
    )(o_lat, o_ctx, x, mods_l, wo)


def _rope(xh, cosf, sinf, even):
    swapped = jnp.where(even, pltpu.roll(xh, HEAD_DIM - 1, 1), pltpu.roll(xh, 1, 1))
    return xh * cosf + swapped * sinf


def _gqa_proj_kernel(x_ref, mod_ref, g_ref, w_ref, qn_ref, kn_ref, cos_ref, sin_ref, q_ref, k_ref, v_ref, *, tm):
    h = _norm_mod(x_ref[...], g_ref[...], mod_ref[0:1, :], mod_ref[1:2, :]).astype(BF16)
    qkv = _dot(h, w_ref[...])
    cosf, sinf = cos_ref[...], sin_ref[...]
    even = lax.broadcasted_iota(jnp.int32, (tm, HEAD_DIM), 1) % 2 == 0
    for hd in range(GQA_HEADS):
        lanes = slice(hd * HEAD_DIM, (hd + 1) * HEAD_DIM)
        q_ref[:, lanes] = _rope(_rms(qkv[:, lanes], qn_ref[...]), cosf, sinf, even).astype(BF16)
    k0 = GQA_HEADS * HEAD_DIM
    for hd in range(GQA_KV_HEADS):
        lanes = slice(hd * HEAD_DIM, (hd + 1) * HEAD_DIM)
        kh = qkv[:, k0 + hd * HEAD_DIM:k0 + (hd + 1) * HEAD_DIM]
        k_ref[:, lanes] = _rope(_rms(kh, kn_ref[...]), cosf, sinf, even).astype(BF16)
    v_ref[...] = qkv[:, k0 + GQA_KV_HEADS * HEAD_DIM:].astype(BF16)


def _gqa_proj(x, mods_l, g, w_qkv, qn, kn, cos_ext, sin_ext):
    tm = 256
    dq, dkv = GQA_HEADS * HEAD_DIM, GQA_KV_HEADS * HEAD_DIM
    n_lat = ROWS_LAT // tm
    rope_blk = lambda t: (jnp.where(t < n_lat, t % (SEQ // tm), SEQ // tm), 0)
    return pl.pallas_call(
        functools.partial(_gqa_proj_kernel, tm=tm),
        grid=(ROWS_ALL // tm,),
        in_specs=[
            pl.BlockSpec((tm, D_MODEL), lambda t: (t, 0)),
            pl.BlockSpec((None, 6, D_MODEL), lambda t: (_mod_row(t, tm), 0, 0)),
            pl.BlockSpec((1, D_MODEL), lambda t: (0, 0)),
            pl.BlockSpec((D_MODEL, dq + 2 * dkv), lambda t: (0, 0)),
            pl.BlockSpec((1, HEAD_DIM), lambda t: (0, 0)),
            pl.BlockSpec((1, HEAD_DIM), lambda t: (0, 0)),
            pl.BlockSpec((tm, HEAD_DIM), rope_blk),
            pl.BlockSpec((tm, HEAD_DIM), rope_blk),
        ],
        out_specs=[
            pl.BlockSpec((tm, dq), lambda t: (t, 0)),
            pl.BlockSpec((tm, dkv), lambda t: (t, 0)),
            pl.BlockSpec((tm, dkv), lambda t: (t, 0)),
        ],
        out_shape=[
            jax.ShapeDtypeStruct((ROWS_ALL, dq), BF16),
            jax.ShapeDtypeStruct((ROWS_ALL, dkv), BF16),
            jax.ShapeDtypeStruct((ROWS_ALL, dkv), BF16),
        ],
        compiler_params=_params("parallel"),
        name="gqa_proj",
    )(x, mods_l, g.reshape(1, D_MODEL), w_qkv, qn.reshape(1, HEAD_DIM), kn.reshape(1, HEAD_DIM), cos_ext, sin_ext)


def _softmax_rows(s):
    m = jnp.max(s, axis=-1, keepdims=True)
    p = jnp.exp(s - m)
    return p, jnp.sum(p, axis=-1, keepdims=True)


def _stack_heads(ref, n):
    return jnp.concatenate([ref[:, g * HEAD_DIM:(g + 1) * HEAD_DIM] for g in range(n)], axis=0)


def _gqa_attn_kernel(q_ref, qc_ref, k_ref, v_ref, kc_ref, vc_ref, ol_ref, oc_ref, m_ref, l_ref, acc_ref, *, tq, tk):
    kc, vc = kc_ref[...], vc_ref[...]

    @pl.when(pl.program_id(2) == 0)
    def _():
        p, l = _softmax_rows(_dot_nt(_stack_heads(qc_ref, GQA_GROUP), kc) * ATTN_SCALE)
        oc = _dot(p.astype(BF16), vc) / l
        for g in range(GQA_GROUP):
            oc_ref[:, g * HEAD_DIM:(g + 1) * HEAD_DIM] = oc[g * CTX_LEN:(g + 1) * CTX_LEN].astype(BF16)

    qs = _stack_heads(q_ref, GQA_GROUP)
    s = _dot_nt(qs, kc) * ATTN_SCALE
    m = jnp.max(s, axis=-1, keepdims=True)
    p = jnp.exp(s - m)
    m_ref[...] = m
    l_ref[...] = jnp.sum(p, axis=-1, keepdims=True)
    acc_ref[...] = _dot(p.astype(BF16), vc)

    def body(j, carry):
        off = pl.multiple_of(j * tk, tk)
        s = _dot_nt(qs, k_ref[pl.ds(off, tk), :]) * ATTN_SCALE
        m_old = m_ref[...]
        m_new = jnp.maximum(m_old, jnp.max(s, axis=-1, keepdims=True))
        alpha = jnp.exp(m_old - m_new)
        p = jnp.exp(s - m_new)
        l_ref[...] = alpha * l_ref[...] + jnp.sum(p, axis=-1, keepdims=True)
        acc_ref[...] = alpha * acc_ref[...] + _dot(p.astype(BF16), v_ref[pl.ds(off, tk), :])
        m_ref[...] = m_new
        return carry

    lax.fori_loop(0, SEQ // tk, body, 0)
    o = acc_ref[...] / l_ref[...]
    for g in range(GQA_GROUP):
        ol_ref[:, g * HEAD_DIM:(g + 1) * HEAD_DIM] = o[g * tq:(g + 1) * tq].astype(BF16)


def _gqa_attn(q, k, v):
    tq, tk = 256, 512
    gw = GQA_GROUP * HEAD_DIM
    nq = SEQ // tq
    ctx_blk = ROWS_LAT // CTX_LEN
    return pl.pallas_call(
        functools.partial(_gqa_attn_kernel, tq=tq, tk=tk),
        grid=(BATCH, GQA_KV_HEADS, nq),
        in_specs=[
            pl.BlockSpec((tq, gw), lambda b, h, i: (b * nq + i, h)),
            pl.BlockSpec((CTX_LEN, gw), lambda b, h, i: (ctx_blk + b, h)),
            pl.BlockSpec((SEQ, HEAD_DIM), lambda b, h, i: (b, h)),
            pl.BlockSpec((SEQ, HEAD_DIM), lambda b, h, i: (b, h)),
            pl.BlockSpec((CTX_LEN, HEAD_DIM), lambda b, h, i: (ctx_blk + b, h)),
            pl.BlockSpec((CTX_LEN, HEAD_DIM), lambda b, h, i: (ctx_blk + b, h)),
        ],
        out_specs=[
            pl.BlockSpec((tq, gw), lambda b, h, i: (b * nq + i, h)),
            pl.BlockSpec((CTX_LEN, gw), lambda b, h, i: (b, h)),
        ],
        out_shape=[
            jax.ShapeDtypeStruct((ROWS_LAT, D_MODEL), BF16),
            jax.ShapeDtypeStruct((ROWS_CTX, D_MODEL), BF16),
        ],
        scratch_shapes=[
            pltpu.VMEM((GQA_GROUP * tq, 1), F32),
            pltpu.VMEM((GQA_GROUP * tq, 1), F32),
            pltpu.VMEM((GQA_GROUP * tq, HEAD_DIM), F32),
        ],
        compiler_params=_params("parallel", "parallel", "arbitrary"),
        name="gqa_attn",
    )(q, q, k, v, k, v)


def _nat_proj_kernel(x_ref, mod_ref, g_ref, w_ref, o_ref):
    h = _norm_mod(x_ref[...], g_ref[...], mod_ref[0:1, :], mod_ref[1:2, :]).astype(BF16)
    o_ref[...] = _dot(h, w_ref[...]).astype(BF16)


def _nat_proj(x, mods_l, g, w_qkv):
    tm = 512
    return pl.pallas_call(
        _nat_proj_kernel,
        grid=(3, ROWS_ALL // tm),
        in_specs=[
            pl.BlockSpec((tm, D_MODEL), lambda n, t: (t, 0)),
            pl.BlockSpec((None, 6, D_MODEL), lambda n, t: (_mod_row(t, tm), 0, 0)),
            pl.BlockSpec((1, D_MODEL), lambda n, t: (0, 0)),
            pl.BlockSpec((None, D_MODEL, D_MODEL), lambda n, t: (n, 0, 0)),
        ],
        out_specs=pl.BlockSpec((None, tm, D_MODEL), lambda n, t: (n, t, 0)),
        out_shape=jax.ShapeDtypeStruct((3, ROWS_ALL, D_MODEL), BF16),
        compiler_params=_params("parallel", "parallel"),
        name="nat_proj",
    )(x, mods_l, g.reshape(1, D_MODEL), w_qkv)


def _nat_attn_kernel(q_ref, k_ref, v_ref, qc_ref, kc_ref, vc_ref, bias_ref, ol_ref, oc_ref):
    kc, vc = kc_ref[...], vc_ref[...]
    p, l = _softmax_rows(_dot_nt(qc_ref[...], kc) * ATTN_SCALE)
    oc_ref[...] = (_dot(p.astype(BF16), vc) / l).astype(BF16)

    nk_rows = NAT_KH * GRID_W

    def body(r, carry):
        rstart = jnp.clip(r - NAT_KH // 2, 0, GRID_H - NAT_KH)
        q = q_ref[pl.ds(pl.multiple_of(r * GRID_W, GRID_W), GRID_W), :]
        koff = pl.multiple_of(rstart * GRID_W, GRID_W)
        s_loc = _dot_nt(q, k_ref[pl.ds(koff, nk_rows), :]) * ATTN_SCALE + bias_ref[r - rstart]
        s_ctx = _dot_nt(q, kc) * ATTN_SCALE
        m = jnp.maximum(jnp.max(s_loc, axis=-1, keepdims=True), jnp.max(s_ctx, axis=-1, keepdims=True))
        p_loc = jnp.exp(s_loc - m)
        p_ctx = jnp.exp(s_ctx - m)
        l = jnp.sum(p_loc, axis=-1, keepdims=True) + jnp.sum(p_ctx, axis=-1, keepdims=True)
        o = _dot(p_loc.astype(BF16), v_ref[pl.ds(koff, nk_rows), :]) + _dot(p_ctx.astype(BF16), vc)
        ol_ref[pl.ds(pl.multiple_of(r * GRID_W, GRID_W), GRID_W), :] = (o / l).astype(BF16)
        return carry

    lax.fori_loop(0, GRID_H, body, 0)


def _nat_attn(qkv, bias_tbl):
    ctx_blk = ROWS_LAT // CTX_LEN
    lat = lambda n: pl.BlockSpec((None, SEQ, HEAD_DIM), lambda b, h: (n, b, h))
    ctx = lambda n: pl.BlockSpec((None, CTX_LEN, HEAD_DIM), lambda b, h: (n, ctx_blk + b, h))
    return pl.pallas_call(
        _nat_attn_kernel,
        grid=(BATCH, NAT_HEADS),
        in_specs=[lat(0), lat(1), lat(2), ctx(0), ctx(1), ctx(2),
                  pl.BlockSpec((None, NAT_KH, GRID_W, NAT_KH * GRID_W), lambda b, h: (h, 0, 0, 0))],
        out_specs=[
            pl.BlockSpec((SEQ, HEAD_DIM), lambda b, h: (b, h)),
            pl.BlockSpec((CTX_LEN, HEAD_DIM), lambda b, h: (b, h)),
        ],
        out_shape=[
            jax.ShapeDtypeStruct((ROWS_LAT, D_MODEL), BF16),
            jax.ShapeDtypeStruct((ROWS_CTX, D_MODEL), BF16),
        ],
        compiler_params=_params("parallel", "parallel"),
        name="nat_attn",
    )(qkv, qkv, qkv, qkv, qkv, qkv, bias_tbl)


def _nat_bias_table(rpb):
    p = jnp.arange(NAT_KH)[:, None, None, None]
    q = jnp.arange(GRID_W)[None, :, None, None]
    kr = jnp.arange(NAT_KH)[None, None, :, None]
    kc = jnp.arange(GRID_W)[None, None, None, :]
    cstart = jnp.clip(q - NAT_KW // 2, 0, GRID_W - NAT_KW)
    inside = (kc >= cstart) & (kc < cstart + NAT_KW)
    dr = jnp.broadcast_to(kr - p + NAT_KH - 1, (NAT_KH, GRID_W, NAT_KH, GRID_W))
    dc = jnp.broadcast_to(jnp.clip(kc - q + NAT_KW - 1, 0, 2 * NAT_KW - 2), (NAT_KH, GRID_W, NAT_KH, GRID_W))
    tbl = jnp.where(inside[None], rpb[:, dr, dc], MASK_VALUE)
    return tbl.reshape(NAT_HEADS, NAT_KH, GRID_W, NAT_KH * GRID_W).astype(F32)


def _rope_tables(tm):
    t = jnp.arange(SEQ)
    row = (t // GRID_W).astype(F32)
    col = (t % GRID_W).astype(F32)
    inv = ROPE_THETA ** (-jnp.arange(0, ROPE_AXIS_DIM, 2, dtype=F32) / ROPE_AXIS_DIM)
    ang = jnp.concatenate([row[:, None] * inv, col[:, None] * inv], axis=-1)
    cosf = jnp.repeat(jnp.cos(ang), 2, axis=-1)
    sign = jnp.tile(jnp.array([-1.0, 1.0], F32), HEAD_DIM // 2)
    sinf = jnp.repeat(jnp.sin(ang), 2, axis=-1) * sign
    cos_ext = jnp.concatenate([cosf, jnp.ones((tm, HEAD_DIM), F32)], axis=0)
    sin_ext = jnp.concatenate([sinf, jnp.zeros((tm, HEAD_DIM), F32)], axis=0)
    return cos_ext, sin_ext


def kernel(x, c, ctx, c_ctx, ada_w, ada_b, norm_g, ffn_w1, ffn_w3, ffn_w2, pool_w, pool_ls, gqa_wq, gqa_wk, gqa_wv, gqa_wo, gqa_qn, gqa_kn, nat_wq, nat_wk, nat_wv, nat_wo, nat_rpb, final_g):
    xa = jnp.concatenate([x.reshape(ROWS_LAT, D_MODEL), ctx.reshape(ROWS_CTX, D_MODEL)], axis=0)
    c8 = jnp.concatenate([c, c_ctx[None, :], jnp.zeros((8 - BATCH - 1, D_MODEL), F32)], axis=0)
    mods = _ada_all(c8, ada_w, ada_b)[:, :BATCH + 1].reshape(DEPTH, BATCH + 1, 6, D_MODEL)

    for i in range(DEPTH):
        kind, j = i % N_MIXERS, i // N_MIXERS
        last = i == DEPTH - 1
        n_rows = ROWS_LAT if last else ROWS_ALL
        if kind == 0:
            xa = _pool_layer(xa, mods[i], norm_g[i, 0], pool_w[j].astype(BF16), pool_ls[j], n_rows)
        elif kind == 1:
            w_qkv = jnp.concatenate([gqa_wq[j], gqa_wk[j], gqa_wv[j]], axis=1).astype(BF16)
            cos_ext, sin_ext = _rope_tables(256)
            q, k, v = _gqa_proj(xa, mods[i], norm_g[i, 0], w_qkv, gqa_qn[j], gqa_kn[j], cos_ext, sin_ext)
            o_lat, o_ctx = _gqa_attn(q, k, v)
            xa = _attn_out_layer(o_lat, o_ctx, xa, mods[i], gqa_wo[j].astype(BF16))
        else:
            w_qkv = jnp.stack([nat_wq[j], nat_wk[j], nat_wv[j]], axis=0).astype(BF16)
            qkv = _nat_proj(xa, mods[i], norm_g[i, 0], w_qkv)
            o_lat, o_ctx = _nat_attn(qkv, _nat_bias_table(nat_rpb[j]))
            xa = _attn_out_layer(o_lat, o_ctx, xa, mods[i], nat_wo[j].astype(BF16))
        xa = _ffn_layer(xa, mods[i], norm_g[i, 1], ffn_w1[i].astype(BF16), ffn_w3[i].astype(BF16),
                        ffn_w2[i].astype(BF16), final_g, n_rows, last)
    return xa.reshape(BATCH, SEQ, D_MODEL)
```

```python
import functools

import jax
import jax.numpy as jnp
from jax import lax
from jax.experimental import pallas as pl
from jax.experimental.pallas import tpu as pltpu

D_MODEL = 2048
BATCH = 2
SEQ = 4096
DEPTH = 4
GRID_W = 64
CTX_LEN = 256
N_MIXERS = 3
POOL_GROUPS = 4
POOL_WINDOWS = (2, 4, 8, 16)
POOL_GC = D_MODEL // POOL_GROUPS
HEAD_DIM = 128
GQA_HEADS = D_MODEL // HEAD_DIM
GQA_KV_HEADS = 4
GQA_GROUP = GQA_HEADS // GQA_KV_HEADS
ROPE_THETA = 10000.0
ROPE_AXIS_DIM = HEAD_DIM // 2
NAT_HEADS = D_MODEL // HEAD_DIM
NAT_KH = 8
NAT_KW = 16
FFN_HIDDEN = -(-8 * D_MODEL // (3 * 256)) * 256
NORM_EPS = 1e-6

ROWS_LAT = BATCH * SEQ
ROWS_CTX = BATCH * CTX_LEN
ROWS_ALL = ROWS_LAT + ROWS_CTX
GRID_H = SEQ // GRID_W
ATTN_SCALE = HEAD_DIM ** -0.5
LOG2E = 1.4426950408889634
QK_PRESCALE = ATTN_SCALE * LOG2E
MASK_VALUE = -1e30
LANE = 128
POOL_HALO = 8
SOFTMAX_ROWS = 32

VMEM_LIMIT_BYTES = 56 * 1024 * 1024

F32 = jnp.float32
BF16 = jnp.bfloat16


def _params(*sem):
    return pltpu.CompilerParams(dimension_semantics=sem, vmem_limit_bytes=VMEM_LIMIT_BYTES)


def _dot(a, b):
    return jnp.dot(a, b, preferred_element_type=F32)


def _dot_nt(a, b):
    return lax.dot_general(a, b, (((1,), (1,)), ((), ())), preferred_element_type=F32)


def _silu(a):
    return a * (1.0 / (1.0 + jnp.exp(-a)))


def _rms(x, g):
    return x * lax.rsqrt(jnp.mean(x * x, axis=-1, keepdims=True) + NORM_EPS) * g


def _norm_mod(x, g, shift, scale):
    return _rms(x, g) * (1.0 + scale) + shift


def _mod_row(t, tm):
    return jnp.where(t < ROWS_LAT // tm, t // (SEQ // tm), BATCH)


def _ada_kernel(c_ref, w_ref, b_ref, o_ref):
    s = _silu(c_ref[...])
    o_ref[...] = _dot(s.astype(BF16), w_ref[...].astype(BF16)) + b_ref[...]


def _ada_all(c8, ada_w, ada_b):
    tn = 1024
    return pl.pallas_call(
        _ada_kernel,
        grid=(DEPTH, 6 * D_MODEL // tn),
        in_specs=[
            pl.BlockSpec((8, D_MODEL), lambda l, n: (0, 0)),
            pl.BlockSpec((None, D_MODEL, tn), lambda l, n: (l, 0, n)),
            pl.BlockSpec((None, 1, tn), lambda l, n: (l, 0, n)),
        ],
        out_specs=pl.BlockSpec((None, 8, tn), lambda l, n: (l, 0, n)),
        out_shape=jax.ShapeDtypeStruct((DEPTH, 8, 6 * D_MODEL), F32),
        compiler_params=_params("parallel", "parallel"),
        name="ada",
    )(c8, ada_w, ada_b.reshape(DEPTH, 1, 6 * D_MODEL))


def _pool_kernel(x_ref, xp_ref, xn_ref, mod_ref, g_ref, w_ref, ls_ref, o_ref, hext_ref, *, tm):
    t = pl.program_id(0)
    shift, scale, gate = mod_ref[0:1, :], mod_ref[1:2, :], mod_ref[2:3, :]
    g = g_ref[...]

    n_lat_tiles = ROWS_LAT // tm
    is_lat = t < n_lat_tiles
    tiles_in_seq = jnp.where(is_lat, SEQ // tm, CTX_LEN // tm)
    pos_tile = jnp.where(is_lat, t % (SEQ // tm), (t - n_lat_tiles) % (CTX_LEN // tm))
    first = pos_tile == 0
    last = pos_tile == tiles_in_seq - 1

    x = x_ref[...]
    hext_ref[POOL_HALO:POOL_HALO + tm, :] = _norm_mod(x, g, shift, scale)
    hext_ref[0:POOL_HALO, :] = jnp.where(first, 0.0, _norm_mod(xp_ref[...], g, shift, scale))
    hext_ref[POOL_HALO + tm:, :] = jnp.where(last, 0.0, _norm_mod(xn_ref[...], g, shift, scale))

    pos = pos_tile * tm + lax.broadcasted_iota(jnp.int32, (tm, 1), 0)
    n = tiles_in_seq * tm
    for k in range(POOL_GROUPS):
        w = POOL_WINDOWS[k]
        lanes = slice(k * POOL_GC, (k + 1) * POOL_GC)
        acc = hext_ref[POOL_HALO - w // 2:POOL_HALO - w // 2 + tm, lanes]
        for d in range(1 - w // 2, w - w // 2):
            acc = acc + hext_ref[POOL_HALO + d:POOL_HALO + d + tm, lanes]
        cnt = jnp.minimum(pos + (w - w // 2), n) - jnp.maximum(pos - w // 2, 0)
        pooled = acc / cnt.astype(F32) - hext_ref[POOL_HALO:POOL_HALO + tm, lanes]
        y = _dot(pooled.astype(BF16), w_ref[k]) * ls_ref[:, lanes]
        o_ref[:, lanes] = x[:, lanes] + gate[:, lanes] * y


def _pool_layer(x, mods_l, g, w, ls, n_rows):
    tm = 256
    nb8 = x.shape[0] // POOL_HALO
    return pl.pallas_call(
        functools.partial(_pool_kernel, tm=tm),
        grid=(n_rows // tm,),
        in_specs=[
            pl.BlockSpec((tm, D_MODEL), lambda t: (t, 0)),
            pl.BlockSpec((POOL_HALO, D_MODEL), lambda t: (jnp.maximum(t * (tm // POOL_HALO) - 1, 0), 0)),
            pl.BlockSpec((POOL_HALO, D_MODEL), lambda t: (jnp.minimum((t + 1) * (tm // POOL_HALO), nb8 - 1), 0)),
            pl.BlockSpec((None, 6, D_MODEL), lambda t: (_mod_row(t, tm), 0, 0)),
            pl.BlockSpec((1, D_MODEL), lambda t: (0, 0)),
            pl.BlockSpec((POOL_GROUPS, POOL_GC, POOL_GC), lambda t: (0, 0, 0)),
            pl.BlockSpec((1, D_MODEL), lambda t: (0, 0)),
        ],
        out_specs=pl.BlockSpec((tm, D_MODEL), lambda t: (t, 0)),
        out_shape=jax.ShapeDtypeStruct((n_rows, D_MODEL), F32),
        scratch_shapes=[pltpu.VMEM((tm + 2 * POOL_HALO, D_MODEL), F32)],
        compiler_params=_params("parallel"),
        name="pool_layer",
    )(x, x, x, mods_l, g.reshape(1, D_MODEL), w, ls.reshape(1, D_MODEL))


def _ffn_kernel(x_ref, mod_ref, g_ref, w1_ref, w3_ref, w2_ref, fg_ref, o_ref, h_ref, acc_ref, *, nf, final):
    f = pl.program_id(1)

    @pl.when(f == 0)
    def _():
        h = _norm_mod(x_ref[...], g_ref[...], mod_ref[3:4, :], mod_ref[4:5, :])
        h_ref[...] = h.astype(BF16)
        acc_ref[...] = jnp.zeros_like(acc_ref)

    h = h_ref[...]
    a = _dot(h, w1_ref[...])
    b = _dot(h, w3_ref[...])
    acc_ref[...] += _dot((_silu(a) * b).astype(BF16), w2_ref[...])

    @pl.when(f == nf - 1)
    def _():
        y = x_ref[...] + mod_ref[5:6, :] * acc_ref[...]
        if final:
            y = _rms(y, fg_ref[...])
        o_ref[...] = y


def _ffn_layer(x, mods_l, g, w1, w3, w2, final_g, n_rows, final):
    tm, tf = 512, 512
    nf = FFN_HIDDEN // tf
    return pl.pallas_call(
        functools.partial(_ffn_kernel, nf=nf, final=final),
        grid=(n_rows // tm, nf),
        in_specs=[
            pl.BlockSpec((tm, D_MODEL), lambda t, f: (t, 0)),
            pl.BlockSpec((None, 6, D_MODEL), lambda t, f: (_mod_row(t, tm), 0, 0)),
            pl.BlockSpec((1, D_MODEL), lambda t, f: (0, 0)),
            pl.BlockSpec((D_MODEL, tf), lambda t, f: (0, f)),
            pl.BlockSpec((D_MODEL, tf), lambda t, f: (0, f)),
            pl.BlockSpec((tf, D_MODEL), lambda t, f: (f, 0)),
            pl.BlockSpec((1, D_MODEL), lambda t, f: (0, 0)),
        ],
        out_specs=pl.BlockSpec((tm, D_MODEL), lambda t, f: (t, 0)),
        out_shape=jax.ShapeDtypeStruct((n_rows, D_MODEL), F32),
        scratch_shapes=[pltpu.VMEM((tm, D_MODEL), BF16), pltpu.VMEM((tm, D_MODEL), F32)],
        compiler_params=_params("parallel", "arbitrary"),
        name="ffn_layer",
    )(x, mods_l, g.reshape(1, D_MODEL), w1, w3, w2, final_g.reshape(1, D_MODEL))


def _attn_out_kernel(ol_ref, oc_ref, x_ref, mod_ref, wo_ref, o_ref, *, tm):
    t = pl.program_id(0)
    o = jnp.where(t < ROWS_LAT // tm, ol_ref[...], oc_ref[...])
    o_ref[...] = x_ref[...] + mod_ref[2:3, :] * _dot(o, wo_ref[...])


def _attn_out_layer(o_lat, o_ctx, x, mods_l, wo):
    tm = 256
    n_lat = ROWS_LAT // tm
    return pl.pallas_call(
        functools.partial(_attn_out_kernel, tm=tm),
        grid=(ROWS_ALL // tm,),
        in_specs=[
            pl.BlockSpec((tm, D_MODEL), lambda t: (jnp.minimum(t, n_lat - 1), 0)),
            pl.BlockSpec((tm, D_MODEL), lambda t: (jnp.maximum(t - n_lat, 0), 0)),
            pl.BlockSpec((tm, D_MODEL), lambda t: (t, 0)),
            pl.BlockSpec((None, 6, D_MODEL), lambda t: (_mod_row(t, tm), 0, 0)),
            pl.BlockSpec((D_MODEL, D_MODEL), lambda t: (0, 0)),
        ],
        out_specs=pl.BlockSpec((tm, D_MODEL), lambda t: (t, 0)),
        out_shape=jax.ShapeDtypeStruct((ROWS_ALL, D_MODEL), F32),
        compiler_params=_params("parallel"),
        name="attn_out",
    )(o_lat, o_ctx, x, mods_l, wo)


def _rope(xh, cosf, sinf, even):
    swapped = jnp.where(even, pltpu.roll(xh, HEAD_DIM - 1, 1), pltpu.roll(xh, 1, 1))
    return xh * cosf + swapped * sinf


def _gqa_proj_kernel(x_ref, mod_ref, g_ref, w_ref, qn_ref, kn_ref, cos_ref, sin_ref, q_ref, k_ref, v_ref, *, tm):
    h = _norm_mod(x_ref[...], g_ref[...], mod_ref[0:1, :], mod_ref[1:2, :]).astype(BF16)
    qkv = _dot(h, w_ref[...])
    cosf, sinf = cos_ref[...], sin_ref[...]
    even = lax.broadcasted_iota(jnp.int32, (tm, HEAD_DIM), 1) % 2 == 0
    for hd in range(GQA_HEADS):
        lanes = slice(hd * HEAD_DIM, (hd + 1) * HEAD_DIM)
        qh = _rope(_rms(qkv[:, lanes], qn_ref[...]), cosf, sinf, even)
        q_ref[:, lanes] = (qh * QK_PRESCALE).astype(BF16)
    k0 = GQA_HEADS * HEAD_DIM
    for hd in range(GQA_KV_HEADS):
        lanes = slice(hd * HEAD_DIM, (hd + 1) * HEAD_DIM)
        kh = qkv[:, k0 + hd * HEAD_DIM:k0 + (hd + 1) * HEAD_DIM]
        k_ref[:, lanes] = _rope(_rms(kh, kn_ref[...]), cosf, sinf, even).astype(BF16)
    v_ref[...] = qkv[:, k0 + GQA_KV_HEADS * HEAD_DIM:].astype(BF16)


def _gqa_proj(x, mods_l, g, w_qkv, qn, kn, cos_ext, sin_ext):
    tm = 256
    dq, dkv = GQA_HEADS * HEAD_DIM, GQA_KV_HEADS * HEAD_DIM
    n_lat = ROWS_LAT // tm
    rope_blk = lambda t: (jnp.where(t < n_lat, t % (SEQ // tm), SEQ // tm), 0)
    return pl.pallas_call(
        functools.partial(_gqa_proj_kernel, tm=tm),
        grid=(ROWS_ALL // tm,),
        in_specs=[
            pl.BlockSpec((tm, D_MODEL), lambda t: (t, 0)),
            pl.BlockSpec((None, 6, D_MODEL), lambda t: (_mod_row(t, tm), 0, 0)),
            pl.BlockSpec((1, D_MODEL), lambda t: (0, 0)),
            pl.BlockSpec((D_MODEL, dq + 2 * dkv), lambda t: (0, 0)),
            pl.BlockSpec((1, HEAD_DIM), lambda t: (0, 0)),
            pl.BlockSpec((1, HEAD_DIM), lambda t: (0, 0)),
            pl.BlockSpec((tm, HEAD_DIM), rope_blk),
            pl.BlockSpec((tm, HEAD_DIM), rope_blk),
        ],
        out_specs=[
            pl.BlockSpec((tm, dq), lambda t: (t, 0)),
            pl.BlockSpec((tm, dkv), lambda t: (t, 0)),
            pl.BlockSpec((tm, dkv), lambda t: (t, 0)),
        ],
        out_shape=[
            jax.ShapeDtypeStruct((ROWS_ALL, dq), BF16),
            jax.ShapeDtypeStruct((ROWS_ALL, dkv), BF16),
            jax.ShapeDtypeStruct((ROWS_ALL, dkv), BF16),
        ],
        compiler_params=_params("parallel"),
        name="gqa_proj",
    )(x, mods_l, g.reshape(1, D_MODEL), w_qkv, qn.reshape(1, HEAD_DIM), kn.reshape(1, HEAD_DIM), cos_ext, sin_ext)


def _softmax_rows(s):
    m = jnp.max(s, axis=-1, keepdims=True)
    p = jnp.exp2(s - m)
    return p, jnp.sum(p, axis=-1, keepdims=True)


def _stack_heads(ref, n):
    return jnp.concatenate([ref[:, g * HEAD_DIM:(g + 1) * HEAD_DIM] for g in range(n)], axis=0)


def _online_softmax_chunk(s_ref, p_ref, m_ref, l_ref, a_ref, n_rows, width):
    nslab = width // LANE
    blocks = [slice(i * SOFTMAX_ROWS, (i + 1) * SOFTMAX_ROWS) for i in range(n_rows // SOFTMAX_ROWS)]
    for rows in blocks:
        mc = functools.reduce(jnp.maximum, [s_ref[rows, c * LANE:(c + 1) * LANE] for c in range(nslab)])
        m_prev = m_ref[rows, :]
        m_new = jnp.maximum(m_prev, jnp.max(mc, axis=-1, keepdims=True))
        a_ref[rows, :] = jnp.exp2(m_prev - m_new)
        m_ref[rows, :] = m_new
    for rows in blocks:
        m_new = m_ref[rows, :]
        lsum = a_ref[rows, :] * l_ref[rows, :]
        for c in range(nslab):
            p = jnp.exp2(s_ref[rows, c * LANE:(c + 1) * LANE] - m_new)
            lsum = lsum + p
            p_ref[rows, c * LANE:(c + 1) * LANE] = p.astype(BF16)
        l_ref[rows, :] = lsum


def _gqa_attn_kernel(q_ref, qc_ref, k_ref, v_ref, kc_ref, vc_ref, ol_ref, oc_ref,
                     s_ref, p_ref, m_ref, l_ref, a_ref, acc_ref, *, tq, tk):
    kc, vc = kc_ref[...], vc_ref[...]
    n_rows = GQA_GROUP * tq

    @pl.when(pl.program_id(2) == 0)
    def _():
        p, l = _softmax_rows(_dot_nt(_stack_heads(qc_ref, GQA_GROUP), kc))
        oc = _dot(p.astype(BF16), vc) / l
        for g in range(GQA_GROUP):
            oc_ref[:, g * HEAD_DIM:(g + 1) * HEAD_DIM] = oc[g * CTX_LEN:(g + 1) * CTX_LEN].astype(BF16)

    qs = _stack_heads(q_ref, GQA_GROUP)
    m_ref[...] = jnp.full_like(m_ref, MASK_VALUE)
    l_ref[...] = jnp.zeros_like(l_ref)
    acc_ref[...] = jnp.zeros_like(acc_ref)

    chunks = [(kc_ref, vc_ref, 0, CTX_LEN)] + [(k_ref, v_ref, j * tk, tk) for j in range(SEQ // tk)]
    for n, (kr, vr, off, width) in enumerate(chunks):
        sb, pb = s_ref.at[n % 2], p_ref.at[n % 2]
        sb[:, :width] = _dot_nt(qs, kr[off:off + width, :])
        _online_softmax_chunk(sb, pb, m_ref, l_ref, a_ref, n_rows, width)
        acc_ref[...] = a_ref[...] * acc_ref[...] + _dot(pb[:, :width], vr[off:off + width, :])

    o = acc_ref[...] / jnp.sum(l_ref[...], axis=-1, keepdims=True)
    for g in range(GQA_GROUP):
        ol_ref[:, g * HEAD_DIM:(g + 1) * HEAD_DIM] = o[g * tq:(g + 1) * tq].astype(BF16)


def _gqa_attn(q, k, v):
    tq, tk = 256, 1024
    gw = GQA_GROUP * HEAD_DIM
    nq = SEQ // tq
    ctx_blk = ROWS_LAT // CTX_LEN
    return pl.pallas_call(
        functools.partial(_gqa_attn_kernel, tq=tq, tk=tk),
        grid=(BATCH, GQA_KV_HEADS, nq),
        in_specs=[
            pl.BlockSpec((tq, gw), lambda b, h, i: (b * nq + i, h)),
            pl.BlockSpec((CTX_LEN, gw), lambda b, h, i: (ctx_blk + b, h)),
            pl.BlockSpec((SEQ, HEAD_DIM), lambda b, h, i: (b, h)),
            pl.BlockSpec((SEQ, HEAD_DIM), lambda b, h, i: (b, h)),
            pl.BlockSpec((CTX_LEN, HEAD_DIM), lambda b, h, i: (ctx_blk + b, h)),
            pl.BlockSpec((CTX_LEN, HEAD_DIM), lambda b, h, i: (ctx_blk + b, h)),
        ],
        out_specs=[
            pl.BlockSpec((tq, gw), lambda b, h, i: (b * nq + i, h)),
            pl.BlockSpec((CTX_LEN, gw), lambda b, h, i: (b, h)),
        ],
        out_shape=[
            jax.ShapeDtypeStruct((ROWS_LAT, D_MODEL), BF16),
            jax.ShapeDtypeStruct((ROWS_CTX, D_MODEL), BF16),
        ],
        scratch_shapes=[
            pltpu.VMEM((2, GQA_GROUP * tq, tk), F32),
            pltpu.VMEM((2, GQA_GROUP * tq, tk), BF16),
            pltpu.VMEM((GQA_GROUP * tq, LANE), F32),
            pltpu.VMEM((GQA_GROUP * tq, LANE), F32),
            pltpu.VMEM((GQA_GROUP * tq, LANE), F32),
            pltpu.VMEM((GQA_GROUP * tq, HEAD_DIM), F32),
        ],
        compiler_params=_params("parallel", "parallel", "arbitrary"),
        name="gqa_attn",
    )(q, q, k, v, k, v)


def _nat_proj_kernel(x_ref, mod_ref, g_ref, w_ref, o_ref):
    h = _norm_mod(x_ref[...], g_ref[...], mod_ref[0:1, :], mod_ref[1:2, :]).astype(BF16)
    prescale = jnp.where(pl.program_id(0) == 0, QK_PRESCALE, 1.0)
    o_ref[...] = (_dot(h, w_ref[...]) * prescale).astype(BF16)


def _nat_proj(x, mods_l, g, w_qkv):
    tm = 512
    return pl.pallas_call(
        _nat_proj_kernel,
        grid=(3, ROWS_ALL // tm),
        in_specs=[
            pl.BlockSpec((tm, D_MODEL), lambda n, t: (t, 0)),
            pl.BlockSpec((None, 6, D_MODEL), lambda n, t: (_mod_row(t, tm), 0, 0)),
            pl.BlockSpec((1, D_MODEL), lambda n, t: (0, 0)),
            pl.BlockSpec((None, D_MODEL, D_MODEL), lambda n, t: (n, 0, 0)),
        ],
        out_specs=pl.BlockSpec((None, tm, D_MODEL), lambda n, t: (n, t, 0)),
        out_shape=jax.ShapeDtypeStruct((3, ROWS_ALL, D_MODEL), BF16),
        compiler_params=_params("parallel", "parallel"),
        name="nat_proj",
    )(x, mods_l, g.reshape(1, D_MODEL), w_qkv)


NAT_BLOCK_ROWS = 8
NAT_WIN_ROWS = 2 * NAT_BLOCK_ROWS
NAT_Q = NAT_BLOCK_ROWS * GRID_W
NAT_K = NAT_WIN_ROWS * GRID_W
NAT_SLABS = NAT_K // LANE
NAT_PAIR_TILES = 2 * NAT_KH - 2
NAT_TILE_LEFT_MASKED = NAT_PAIR_TILES
NAT_TILE_RIGHT_MASKED = NAT_PAIR_TILES + 1
NAT_TILES = NAT_PAIR_TILES + 2


def _nat_row_plan(cls, i):
    half = NAT_KH // 2
    if cls == "first":
        p, delta = (i, 0) if i < half else (half, i - half)
    elif cls == "mid":
        p, delta = half, i
    else:
        p, delta = (half, half + i) if i < half else (i, NAT_KH)
    if delta % 2 == 0:
        slabs = [delta // 2 + t for t in range(half)]
        tiles = [2 * t - p + NAT_KH - 1 for t in range(half)]
    else:
        assert p == half
        slabs = [(delta - 1) // 2 + t for t in range(half + 1)]
        tiles = ([NAT_TILE_LEFT_MASKED] + [2 * t - 1 - p + NAT_KH - 1 for t in range(1, half)]
                 + [NAT_TILE_RIGHT_MASKED])
    return slabs, tiles


def _nat_build_tiles(rpbw_ref, t2_ref):
    lane = lax.broadcasted_iota(jnp.int32, (GRID_W, LANE), 1)
    cq = lax.broadcasted_iota(jnp.int32, (GRID_W, LANE), 0)
    kc = lane % GRID_W
    cstart = jnp.clip(cq - NAT_KW // 2, 0, GRID_W - NAT_KW)
    inside = (kc >= cstart) & (kc < cstart + NAT_KW)
    for d in range(NAT_PAIR_TILES):
        w = jnp.broadcast_to(rpbw_ref[d:d + 1, :] * LOG2E, (GRID_W, LANE))
        t = pltpu.roll(w, LANE - (NAT_KW - 1), 1, stride=1, stride_axis=0)
        t2_ref[d] = jnp.where(inside, t, MASK_VALUE)
    half = NAT_KH // 2
    t2_ref[NAT_TILE_LEFT_MASKED] = jnp.where(lane < GRID_W, MASK_VALUE, t2_ref[half - 2])
    t2_ref[NAT_TILE_RIGHT_MASKED] = jnp.where(lane < GRID_W, t2_ref[NAT_KH + half - 2], MASK_VALUE)


def _nat_softmax_block(cls, s_ref, sc_ref, t2_ref, p_ref, pc_ref, m_ref, l_ref):
    n_ctx_slabs = CTX_LEN // LANE
    subs = []
    for i in range(NAT_BLOCK_ROWS):
        slabs, tiles = _nat_row_plan(cls, i)
        for a in range(NAT_SLABS):
            if a not in slabs:
                p_ref[i * GRID_W:(i + 1) * GRID_W, a * LANE:(a + 1) * LANE] = jnp.zeros((GRID_W, LANE), BF16)
        for sub in range(GRID_W // SOFTMAX_ROWS):
            trow = slice(sub * SOFTMAX_ROWS, (sub + 1) * SOFTMAX_ROWS)
            rows = slice(i * GRID_W + sub * SOFTMAX_ROWS, i * GRID_W + (sub + 1) * SOFTMAX_ROWS)
            subs.append((rows, trow, slabs, tiles))

    def scores(rows, trow, slabs, tiles):
        sv = [s_ref[rows, a * LANE:(a + 1) * LANE] + t2_ref[t, trow, :] for a, t in zip(slabs, tiles)]
        return sv + [sc_ref[rows, c * LANE:(c + 1) * LANE] for c in range(n_ctx_slabs)]

    for rows, trow, slabs, tiles in subs:
        m = jnp.max(functools.reduce(jnp.maximum, scores(rows, trow, slabs, tiles)), axis=-1, keepdims=True)
        m_ref[rows, :] = jnp.broadcast_to(m, (SOFTMAX_ROWS, LANE))
    for rows, trow, slabs, tiles in subs:
        m = m_ref[rows, :]
        pv = [jnp.exp2(s - m) for s in scores(rows, trow, slabs, tiles)]
        l_ref[rows, :] = functools.reduce(jnp.add, pv)
        for a, pa in zip(slabs, pv):
            p_ref[rows, a * LANE:(a + 1) * LANE] = pa.astype(BF16)
        for c in range(n_ctx_slabs):
            pc_ref[rows, c * LANE:(c + 1) * LANE] = pv[len(slabs) + c].astype(BF16)


def _nat_attn_kernel(q_ref, k_ref, v_ref, qc_ref, kc_ref, vc_ref, rpbw_ref, ol_ref, oc_ref,
                     t2_ref, s_ref, sc_ref, p_ref, pc_ref, m_ref, l_ref):
    kc, vc = kc_ref[...], vc_ref[...]
    p, l = _softmax_rows(_dot_nt(qc_ref[...], kc))
    oc_ref[...] = (_dot(p.astype(BF16), vc) / l).astype(BF16)

    _nat_build_tiles(rpbw_ref, t2_ref)

    def block_pair(specs):
        for n, (cls, q_tok, k_tok) in enumerate(specs):
            qb = q_ref[pl.ds(q_tok, NAT_Q), :]
            s_ref[n] = _dot_nt(qb, k_ref[pl.ds(k_tok, NAT_K), :])
            sc_ref[n] = _dot_nt(qb, kc)
        for n, (cls, q_tok, k_tok) in enumerate(specs):
            _nat_softmax_block(cls, s_ref.at[n], sc_ref.at[n], t2_ref, p_ref.at[n], pc_ref.at[n],
                               m_ref.at[n], l_ref.at[n])
        for n, (cls, q_tok, k_tok) in enumerate(specs):
            o = _dot(p_ref[n], v_ref[pl.ds(k_tok, NAT_K), :]) + _dot(pc_ref[n], vc)
            ol_ref[pl.ds(q_tok, NAT_Q), :] = (o / jnp.sum(l_ref[n], axis=-1, keepdims=True)).astype(BF16)

    block_pair([("first", 0, 0), ("last", SEQ - NAT_Q, SEQ - NAT_K)])
    half_win = (NAT_KH // 2) * GRID_W

    def body(j, carry):
        q0 = pl.multiple_of((2 * j + 1) * NAT_Q, NAT_Q)
        q1 = pl.multiple_of((2 * j + 2) * NAT_Q, NAT_Q)
        block_pair([("mid", q0, pl.multiple_of(q0 - half_win, half_win)),
                    ("mid", q1, pl.multiple_of(q1 - half_win, half_win))])
        return carry

    lax.fori_loop(0, (GRID_H // NAT_BLOCK_ROWS - 2) // 2, body, 0)


def _nat_attn(qkv, rpbw):
    ctx_blk = ROWS_LAT // CTX_LEN
    lat = lambda n: pl.BlockSpec((None, SEQ, HEAD_DIM), lambda b, h: (n, b, h))
    ctx = lambda n: pl.BlockSpec((None, CTX_LEN, HEAD_DIM), lambda b, h: (n, ctx_blk + b, h))
    return pl.pallas_call(
        _nat_attn_kernel,
        grid=(BATCH, NAT_HEADS),
        in_specs=[lat(0), lat(1), lat(2), ctx(0), ctx(1), ctx(2),
                  pl.BlockSpec((None, 2 * NAT_KH, LANE), lambda b, h: (h, 0, 0))],
        out_specs=[
            pl.BlockSpec((SEQ, HEAD_DIM), lambda b, h: (b, h)),
            pl.BlockSpec((CTX_LEN, HEAD_DIM), lambda b, h: (b, h)),
        ],
        out_shape=[
            jax.ShapeDtypeStruct((ROWS_LAT, D_MODEL), BF16),
            jax.ShapeDtypeStruct((ROWS_CTX, D_MODEL), BF16),
        ],
        scratch_shapes=[
            pltpu.VMEM((NAT_TILES, GRID_W, LANE), F32),
            pltpu.VMEM((2, NAT_Q, NAT_K), F32),
            pltpu.VMEM((2, NAT_Q, CTX_LEN), F32),
            pltpu.VMEM((2, NAT_Q, NAT_K), BF16),
            pltpu.VMEM((2, NAT_Q, CTX_LEN), BF16),
            pltpu.VMEM((2, NAT_Q, LANE), F32),
            pltpu.VMEM((2, NAT_Q, LANE), F32),
        ],
        compiler_params=_params("parallel", "parallel"),
        name="nat_attn",
    )(qkv, qkv, qkv, qkv, qkv, qkv, rpbw)


def _nat_rpb_pairs(rpb):
    padded = jnp.pad(rpb, ((0, 0), (0, 2), (0, GRID_W - (2 * NAT_KW - 1))))
    return jnp.concatenate([padded[:, :2 * NAT_KH], padded[:, 1:2 * NAT_KH + 1]], axis=-1)


def _rope_tables(tm):
    t = jnp.arange(SEQ)
    row = (t // GRID_W).astype(F32)
    col = (t % GRID_W).astype(F32)
    inv = ROPE_THETA ** (-jnp.arange(0, ROPE_AXIS_DIM, 2, dtype=F32) / ROPE_AXIS_DIM)
    ang = jnp.concatenate([row[:, None] * inv, col[:, None] * inv], axis=-1)
    cosf = jnp.repeat(jnp.cos(ang), 2, axis=-1)
    sign = jnp.tile(jnp.array([-1.0, 1.0], F32), HEAD_DIM // 2)
    sinf = jnp.repeat(jnp.sin(ang), 2, axis=-1) * sign
    cos_ext = jnp.concatenate([cosf, jnp.ones((tm, HEAD_DIM), F32)], axis=0)
    sin_ext = jnp.concatenate([sinf, jnp.zeros((tm, HEAD_DIM), F32)], axis=0)
    return cos_ext, sin_ext


def kernel(x, c, ctx, c_ctx, ada_w, ada_b, norm_g, ffn_w1, ffn_w3, ffn_w2, pool_w, pool_ls, gqa_wq, gqa_wk, gqa_wv, gqa_wo, gqa_qn, gqa_kn, nat_wq, nat_wk, nat_wv, nat_wo, nat_rpb, final_g):
    xa = jnp.concatenate([x.reshape(ROWS_LAT, D_MODEL), ctx.reshape(ROWS_CTX, D_MODEL)], axis=0)
    c8 = jnp.concatenate([c, c_ctx[None, :], jnp.zeros((8 - BATCH - 1, D_MODEL), F32)], axis=0)
    mods = _ada_all(c8, ada_w, ada_b)[:, :BATCH + 1].reshape(DEPTH, BATCH + 1, 6, D_MODEL)

    for i in range(DEPTH):
        kind, j = i % N_MIXERS, i // N_MIXERS
        last = i == DEPTH - 1
        n_rows = ROWS_LAT if last else ROWS_ALL
        if kind == 0:
            xa = _pool_layer(xa, mods[i], norm_g[i, 0], pool_w[j].astype(BF16), pool_ls[j], n_rows)
        elif kind == 1:
            w_qkv = jnp.concatenate([gqa_wq[j], gqa_wk[j], gqa_wv[j]], axis=1).astype(BF16)
            cos_ext, sin_ext = _rope_tables(256)
            q, k, v = _gqa_proj(xa, mods[i], norm_g[i, 0], w_qkv, gqa_qn[j], gqa_kn[j], cos_ext, sin_ext)
            o_lat, o_ctx = _gqa_attn(q, k, v)
            xa = _attn_out_layer(o_lat, o_ctx, xa, mods[i], gqa_wo[j].astype(BF16))
        else:
            w_qkv = jnp.stack([nat_wq[j], nat_wk[j], nat_wv[j]], axis=0).astype(BF16)
            qkv = _nat_proj(xa, mods[i], norm_g[i, 0], w_qkv)
            o_lat, o_ctx = _nat_attn(qkv, _nat_rpb_pairs(nat_rpb[j]))
            xa = _attn_out_layer(o_lat, o_ctx, xa, mods[i], nat_wo[j].astype(BF16))
        xa = _ffn_layer(xa, mods[i], norm_g[i, 1], ffn_w1[i].astype(BF16), ffn_w3[i].astype(BF16),
                        ffn_w2[i].astype(BF16), final_g, n_rows, last)
    return xa.reshape(BATCH, SEQ, D_MODEL)
```

```python
import functools

import jax
import jax.numpy as jnp
import numpy as np
from jax import lax
from jax.experimental import pallas as pl
from jax.experimental.pallas import tpu as pltpu

D_MODEL = 2048
BATCH = 2
SEQ = 4096
DEPTH = 4
GRID_W = 64
CTX_LEN = 256
N_MIXERS = 3
POOL_GROUPS = 4
POOL_WINDOWS = (2, 4, 8, 16)
POOL_GC = D_MODEL // POOL_GROUPS
HEAD_DIM = 128
GQA_HEADS = D_MODEL // HEAD_DIM
GQA_KV_HEADS = 4
GQA_GROUP = GQA_HEADS // GQA_KV_HEADS
ROPE_THETA = 10000.0
ROPE_AXIS_DIM = HEAD_DIM // 2
NAT_HEADS = D_MODEL // HEAD_DIM
NAT_KH = 8
NAT_KW = 16
FFN_HIDDEN = -(-8 * D_MODEL // (3 * 256)) * 256
NORM_EPS = 1e-6

ROWS_LAT = BATCH * SEQ
ROWS_CTX = BATCH * CTX_LEN
ROWS_ALL = ROWS_LAT + ROWS_CTX
GRID_H = SEQ // GRID_W
ATTN_SCALE = HEAD_DIM ** -0.5
LOG2E = 1.4426950408889634
QK_PRESCALE = ATTN_SCALE * LOG2E
MASK_VALUE = -1e30
LANE = 128
POOL_HALO = 8
SOFTMAX_ROWS = 32

VMEM_LIMIT_BYTES = 56 * 1024 * 1024

F32 = jnp.float32
BF16 = jnp.bfloat16


def _params(*sem):
    return pltpu.CompilerParams(dimension_semantics=sem, vmem_limit_bytes=VMEM_LIMIT_BYTES)


def _dot(a, b):
    return jnp.dot(a, b, preferred_element_type=F32)


def _dot_nt(a, b):
    return lax.dot_general(a, b, (((1,), (1,)), ((), ())), preferred_element_type=F32)


def _silu(a):
    return a * (1.0 / (1.0 + jnp.exp(-a)))


def _rms(x, g):
    return x * lax.rsqrt(jnp.mean(x * x, axis=-1, keepdims=True) + NORM_EPS) * g


def _norm_mod(x, g, shift, scale):
    return _rms(x, g) * (1.0 + scale) + shift


def _mod_row(t, tm):
    return jnp.where(t < ROWS_LAT // tm, t // (SEQ // tm), BATCH)


def _ada_kernel(c_ref, w_ref, b_ref, o_ref):
    s = _silu(c_ref[...])
    o_ref[...] = _dot(s.astype(BF16), w_ref[...].astype(BF16)) + b_ref[...]


def _ada_all(c8, ada_w, ada_b):
    tn = 1024
    return pl.pallas_call(
        _ada_kernel,
        grid=(DEPTH, 6 * D_MODEL // tn),
        in_specs=[
            pl.BlockSpec((8, D_MODEL), lambda l, n: (0, 0)),
            pl.BlockSpec((None, D_MODEL, tn), lambda l, n: (l, 0, n)),
            pl.BlockSpec((None, 1, tn), lambda l, n: (l, 0, n)),
        ],
        out_specs=pl.BlockSpec((None, 8, tn), lambda l, n: (l, 0, n)),
        out_shape=jax.ShapeDtypeStruct((DEPTH, 8, 6 * D_MODEL), F32),
        compiler_params=_params("parallel", "parallel"),
        name="ada",
    )(c8, ada_w, ada_b.reshape(DEPTH, 1, 6 * D_MODEL))


def _pool_bands(tm):
    t = np.arange(tm)[:, None]
    j = np.arange(tm + 2 * POOL_HALO)[None, :]
    pos = np.where(j < tm, j, np.where(j < tm + POOL_HALO, j - tm - POOL_HALO, j - POOL_HALO))
    bands = [(pos >= t - w // 2) & (pos < t + w - w // 2) for w in POOL_WINDOWS]
    return jnp.asarray(np.stack(bands).astype(np.float32), BF16)


def _split_bf16(v):
    hi = v.astype(BF16)
    return hi, (v - hi.astype(F32)).astype(BF16)


def _pool_kernel(xl_ref, xc_ref, xp_ref, xn_ref, mod_ref, g1_ref, g2_ref, band_ref, w_ref, ls_ref,
                 o_ref, h2_ref, hi_ref, lo_ref, *, tm):
    t = pl.program_id(0)
    shift, scale, gate = mod_ref[0:1, :], mod_ref[1:2, :], mod_ref[2:3, :]
    g = g1_ref[...]

    n_lat_tiles = ROWS_LAT // tm
    is_lat = t < n_lat_tiles
    tiles_in_seq = jnp.where(is_lat, SEQ // tm, CTX_LEN // tm)
    pos_tile = jnp.where(is_lat, t % (SEQ // tm), (t - n_lat_tiles) % (CTX_LEN // tm))
    first = pos_tile == 0
    last = pos_tile == tiles_in_seq - 1

    x = jnp.where(is_lat, xl_ref[...], xc_ref[...])
    h = _norm_mod(x, g, shift, scale)
    halo = jnp.concatenate([jnp.where(first, 0.0, _norm_mod(xp_ref[...], g, shift, scale)),
                            jnp.where(last, 0.0, _norm_mod(xn_ref[...], g, shift, scale))], axis=0)
    hi_ref[0:tm, :], lo_ref[0:tm, :] = _split_bf16(h)
    hi_ref[tm:, :], lo_ref[tm:, :] = _split_bf16(halo)

    pos = pos_tile * tm + lax.broadcasted_iota(jnp.int32, (tm, 1), 0)
    n = tiles_in_seq * tm
    for k in range(POOL_GROUPS):
        w = POOL_WINDOWS[k]
        lanes = slice(k * POOL_GC, (k + 1) * POOL_GC)
        wsum = _dot(band_ref[k], hi_ref[:, lanes]) + _dot(band_ref[k], lo_ref[:, lanes])
        cnt = jnp.minimum(pos + (w - w // 2), n) - jnp.maximum(pos - w // 2, 0)
        pooled = wsum / cnt.astype(F32) - h[:, lanes]
        y = _dot(pooled.astype(BF16), w_ref[k]) * ls_ref[:, lanes]
        o_ref[:, lanes] = x[:, lanes] + gate[:, lanes] * y
    h2_ref[...] = _norm_mod(o_ref[...], g2_ref[...], mod_ref[3:4, :], mod_ref[4:5, :]).astype(BF16)


def _pool_layer(x_lat, x_ctx, mods_l, g1, g2, w_all, j, ls, n_rows):
    tm = 256
    n_lat = ROWS_LAT // tm
    nb8 = x_lat.shape[0] // POOL_HALO
    ctx_blk0 = (x_ctx.shape[0] - ROWS_CTX) // tm
    halo_blk = lambda b: jnp.clip(b, 0, nb8 - 1)
    row = lambda v: v.reshape(1, D_MODEL)
    return pl.pallas_call(
        functools.partial(_pool_kernel, tm=tm),
        grid=(n_rows // tm,),
        in_specs=[
            pl.BlockSpec((tm, D_MODEL), lambda t: (jnp.minimum(t, n_lat - 1), 0)),
            pl.BlockSpec((tm, D_MODEL), lambda t: (ctx_blk0 + jnp.maximum(t - n_lat, 0), 0)),
            pl.BlockSpec((POOL_HALO, D_MODEL), lambda t: (halo_blk(t * (tm // POOL_HALO) - 1), 0)),
            pl.BlockSpec((POOL_HALO, D_MODEL), lambda t: (halo_blk((t + 1) * (tm // POOL_HALO)), 0)),
            pl.BlockSpec((None, 6, D_MODEL), lambda t: (_mod_row(t, tm), 0, 0)),
            pl.BlockSpec((1, D_MODEL), lambda t: (0, 0)),
            pl.BlockSpec((1, D_MODEL), lambda t: (0, 0)),
            pl.BlockSpec((POOL_GROUPS, tm, tm + 2 * POOL_HALO), lambda t: (0, 0, 0)),
            pl.BlockSpec((None, POOL_GROUPS, POOL_GC, POOL_GC), lambda t: (j, 0, 0, 0)),
            pl.BlockSpec((1, D_MODEL), lambda t: (0, 0)),
        ],
        out_specs=[pl.BlockSpec((tm, D_MODEL), lambda t: (t, 0)), pl.BlockSpec((tm, D_MODEL), lambda t: (t, 0))],
        out_shape=[jax.ShapeDtypeStruct((n_rows, D_MODEL), F32), jax.ShapeDtypeStruct((n_rows, D_MODEL), BF16)],
        scratch_shapes=[pltpu.VMEM((tm + 2 * POOL_HALO, D_MODEL), BF16),
                        pltpu.VMEM((tm + 2 * POOL_HALO, D_MODEL), BF16)],
        compiler_params=_params("parallel"),
        name="pool_layer",
    )(x_lat, x_ctx, x_lat, x_lat, mods_l, row(g1), row(g2), _pool_bands(tm), w_all, row(ls))


def _ffn_kernel(x_ref, h_ref, mod_ref, w1_ref, w3_ref, w2_ref, fg_ref, o_ref, acc_ref, *, nf, final):
    f = pl.program_id(1)

    @pl.when(f == 0)
    def _():
        acc_ref[...] = jnp.zeros_like(acc_ref)

    h = h_ref[...]
    a = _dot(h, w1_ref[...])
    b = _dot(h, w3_ref[...])
    acc_ref[...] += _dot((_silu(a) * b).astype(BF16), w2_ref[...])

    @pl.when(f == nf - 1)
    def _():
        y = x_ref[...] + mod_ref[5:6, :] * acc_ref[...]
        if final:
            y = _rms(y, fg_ref[...])
        o_ref[...] = y


def _ffn_layer(x, h, mods_l, w1_all, w3_all, w2_all, layer, final_g, n_rows, final):
    tm, tf = 512, 512
    nf = FFN_HIDDEN // tf
    return pl.pallas_call(
        functools.partial(_ffn_kernel, nf=nf, final=final),
        grid=(n_rows // tm, nf),
        in_specs=[
            pl.BlockSpec((tm, D_MODEL), lambda t, f: (t, 0)),
            pl.BlockSpec((tm, D_MODEL), lambda t, f: (t, 0)),
            pl.BlockSpec((None, 6, D_MODEL), lambda t, f: (_mod_row(t, tm), 0, 0)),
            pl.BlockSpec((None, D_MODEL, tf), lambda t, f: (layer, 0, f)),
            pl.BlockSpec((None, D_MODEL, tf), lambda t, f: (layer, 0, f)),
            pl.BlockSpec((None, tf, D_MODEL), lambda t, f: (layer, f, 0)),
            pl.BlockSpec((1, D_MODEL), lambda t, f: (0, 0)),
        ],
        out_specs=pl.BlockSpec((tm, D_MODEL), lambda t, f: (t, 0)),
        out_shape=jax.ShapeDtypeStruct((n_rows, D_MODEL), F32),
        scratch_shapes=[pltpu.VMEM((tm, D_MODEL), F32)],
        compiler_params=_params("parallel", "arbitrary"),
        name="ffn_layer",
    )(x, h, mods_l, w1_all, w3_all, w2_all, final_g.reshape(1, D_MODEL))


def _attn_out_kernel(ol_ref, oc_ref, x_ref, mod_ref, wo_ref, g2_ref, o_ref, h2_ref, *, tm):
    t = pl.program_id(0)
    o = jnp.where(t < ROWS_LAT // tm, ol_ref[...], oc_ref[...])
    y = x_ref[...] + mod_ref[2:3, :] * _dot(o, wo_ref[...])
    o_ref[...] = y
    h2_ref[...] = _norm_mod(y, g2_ref[...], mod_ref[3:4, :], mod_ref[4:5, :]).astype(BF16)


def _attn_out_layer(o_lat, o_ctx, x, mods_l, wo, g2):
    tm = 256
    n_lat = ROWS_LAT // tm
    return pl.pallas_call(
        functools.partial(_attn_out_kernel, tm=tm),
        grid=(ROWS_ALL // tm,),
        in_specs=[
            pl.BlockSpec((tm, D_MODEL), lambda t: (jnp.minimum(t, n_lat - 1), 0)),
            pl.BlockSpec((tm, D_MODEL), lambda t: (jnp.maximum(t - n_lat, 0), 0)),
            pl.BlockSpec((tm, D_MODEL), lambda t: (t, 0)),
            pl.BlockSpec((None, 6, D_MODEL), lambda t: (_mod_row(t, tm), 0, 0)),
            pl.BlockSpec((D_MODEL, D_MODEL), lambda t: (0, 0)),
            pl.BlockSpec((1, D_MODEL), lambda t: (0, 0)),
        ],
        out_specs=[pl.BlockSpec((tm, D_MODEL), lambda t: (t, 0)), pl.BlockSpec((tm, D_MODEL), lambda t: (t, 0))],
        out_shape=[jax.ShapeDtypeStruct((ROWS_ALL, D_MODEL), F32), jax.ShapeDtypeStruct((ROWS_ALL, D_MODEL), BF16)],
        compiler_params=_params("parallel"),
        name="attn_out",
    )(o_lat, o_ctx, x, mods_l, wo, g2.reshape(1, D_MODEL))


def _rope(xh, cosf, sinf, even):
    swapped = jnp.where(even, pltpu.roll(xh, HEAD_DIM - 1, 1), pltpu.roll(xh, 1, 1))
    return xh * cosf + swapped * sinf


def _gqa_proj_kernel(x_ref, mod_ref, g_ref, w_ref, qn_ref, kn_ref, cos_ref, sin_ref, q_ref, k_ref, v_ref, *, tm):
    h = _norm_mod(x_ref[...], g_ref[...], mod_ref[0:1, :], mod_ref[1:2, :]).astype(BF16)
    qkv = _dot(h, w_ref[...])
    cosf, sinf = cos_ref[...], sin_ref[...]
    even = lax.broadcasted_iota(jnp.int32, (tm, HEAD_DIM), 1) % 2 == 0
    for hd in range(GQA_HEADS):
        lanes = slice(hd * HEAD_DIM, (hd + 1) * HEAD_DIM)
        qh = _rope(_rms(qkv[:, lanes], qn_ref[...]), cosf, sinf, even)
        q_ref[:, lanes] = (qh * QK_PRESCALE).astype(BF16)
    k0 = GQA_HEADS * HEAD_DIM
    for hd in range(GQA_KV_HEADS):
        lanes = slice(hd * HEAD_DIM, (hd + 1) * HEAD_DIM)
        kh = qkv[:, k0 + hd * HEAD_DIM:k0 + (hd + 1) * HEAD_DIM]
        k_ref[:, lanes] = _rope(_rms(kh, kn_ref[...]), cosf, sinf, even).astype(BF16)
    v_ref[...] = qkv[:, k0 + GQA_KV_HEADS * HEAD_DIM:].astype(BF16)


def _gqa_proj(x, mods_l, g, w_qkv, qn, kn, cos_ext, sin_ext):
    tm = 256
    dq, dkv = GQA_HEADS * HEAD_DIM, GQA_KV_HEADS * HEAD_DIM
    n_lat = ROWS_LAT // tm
    rope_blk = lambda t: (jnp.where(t < n_lat, t % (SEQ // tm), SEQ // tm), 0)
    return pl.pallas_call(
        functools.partial(_gqa_proj_kernel, tm=tm),
        grid=(ROWS_ALL // tm,),
        in_specs=[
            pl.BlockSpec((tm, D_MODEL), lambda t: (t, 0)),
            pl.BlockSpec((None, 6, D_MODEL), lambda t: (_mod_row(t, tm), 0, 0)),
            pl.BlockSpec((1, D_MODEL), lambda t: (0, 0)),
            pl.BlockSpec((D_MODEL, dq + 2 * dkv), lambda t: (0, 0)),
            pl.BlockSpec((1, HEAD_DIM), lambda t: (0, 0)),
            pl.BlockSpec((1, HEAD_DIM), lambda t: (0, 0)),
            pl.BlockSpec((tm, HEAD_DIM), rope_blk),
            pl.BlockSpec((tm, HEAD_DIM), rope_blk),
        ],
        out_specs=[
            pl.BlockSpec((tm, dq), lambda t: (t, 0)),
            pl.BlockSpec((tm, dkv), lambda t: (t, 0)),
            pl.BlockSpec((tm, dkv), lambda t: (t, 0)),
        ],
        out_shape=[
            jax.ShapeDtypeStruct((ROWS_ALL, dq), BF16),
            jax.ShapeDtypeStruct((ROWS_ALL, dkv), BF16),
            jax.ShapeDtypeStruct((ROWS_ALL, dkv), BF16),
        ],
        compiler_params=_params("parallel"),
        name="gqa_proj",
    )(x, mods_l, g.reshape(1, D_MODEL), w_qkv, qn.reshape(1, HEAD_DIM), kn.reshape(1, HEAD_DIM), cos_ext, sin_ext)


def _softmax_rows(s):
    m = jnp.max(s, axis=-1, keepdims=True)
    p = jnp.exp2(s - m)
    return p, jnp.sum(p, axis=-1, keepdims=True)


def _stack_heads(ref, n):
    return jnp.concatenate([ref[:, g * HEAD_DIM:(g + 1) * HEAD_DIM] for g in range(n)], axis=0)


def _online_softmax_chunk(s_ref, p_ref, m_ref, l_ref, a_ref, n_rows, width):
    nslab = width // LANE
    blocks = [slice(i * SOFTMAX_ROWS, (i + 1) * SOFTMAX_ROWS) for i in range(n_rows // SOFTMAX_ROWS)]
    for rows in blocks:
        mc = functools.reduce(jnp.maximum, [s_ref[rows, c * LANE:(c + 1) * LANE] for c in range(nslab)])
        m_prev = m_ref[rows, :]
        m_new = jnp.maximum(m_prev, jnp.max(mc, axis=-1, keepdims=True))
        a_ref[rows, :] = jnp.exp2(m_prev - m_new)
        m_ref[rows, :] = m_new
    for rows in blocks:
        m_new = m_ref[rows, :]
        lsum = a_ref[rows, :] * l_ref[rows, :]
        for c in range(nslab):
            p = jnp.exp2(s_ref[rows, c * LANE:(c + 1) * LANE] - m_new)
            lsum = lsum + p
            p_ref[rows, c * LANE:(c + 1) * LANE] = p.astype(BF16)
        l_ref[rows, :] = lsum


def _gqa_attn_kernel(q_ref, qc_ref, k_ref, v_ref, kc_ref, vc_ref, ol_ref, oc_ref,
                     s_ref, p_ref, m_ref, l_ref, a_ref, acc_ref, *, tq, tk):
    kc, vc = kc_ref[...], vc_ref[...]
    n_rows = GQA_GROUP * tq

    @pl.when(pl.program_id(2) == 0)
    def _():
        p, l = _softmax_rows(_dot_nt(_stack_heads(qc_ref, GQA_GROUP), kc))
        oc = _dot(p.astype(BF16), vc) / l
        for g in range(GQA_GROUP):
            oc_ref[:, g * HEAD_DIM:(g + 1) * HEAD_DIM] = oc[g * CTX_LEN:(g + 1) * CTX_LEN].astype(BF16)

    qs = _stack_heads(q_ref, GQA_GROUP)
    m_ref[...] = jnp.full_like(m_ref, MASK_VALUE)
    l_ref[...] = jnp.zeros_like(l_ref)
    acc_ref[...] = jnp.zeros_like(acc_ref)

    chunks = [(kc_ref, vc_ref, 0, CTX_LEN)] + [(k_ref, v_ref, j * tk, tk) for j in range(SEQ // tk)]
    for n, (kr, vr, off, width) in enumerate(chunks):
        sb, pb = s_ref.at[n % 2], p_ref.at[n % 2]
        sb[:, :width] = _dot_nt(qs, kr[off:off + width, :])
        _online_softmax_chunk(sb, pb, m_ref, l_ref, a_ref, n_rows, width)
        acc_ref[...] = a_ref[...] * acc_ref[...] + _dot(pb[:, :width], vr[off:off + width, :])

    o = acc_ref[...] / jnp.sum(l_ref[...], axis=-1, keepdims=True)
    for g in range(GQA_GROUP):
        ol_ref[:, g * HEAD_DIM:(g + 1) * HEAD_DIM] = o[g * tq:(g + 1) * tq].astype(BF16)


def _gqa_attn(q, k, v):
    tq, tk = 256, 1024
    gw = GQA_GROUP * HEAD_DIM
    nq = SEQ // tq
    ctx_blk = ROWS_LAT // CTX_LEN
    return pl.pallas_call(
        functools.partial(_gqa_attn_kernel, tq=tq, tk=tk),
        grid=(BATCH, GQA_KV_HEADS, nq),
        in_specs=[
            pl.BlockSpec((tq, gw), lambda b, h, i: (b * nq + i, h)),
            pl.BlockSpec((CTX_LEN, gw), lambda b, h, i: (ctx_blk + b, h)),
            pl.BlockSpec((SEQ, HEAD_DIM), lambda b, h, i: (b, h)),
            pl.BlockSpec((SEQ, HEAD_DIM), lambda b, h, i: (b, h)),
            pl.BlockSpec((CTX_LEN, HEAD_DIM), lambda b, h, i: (ctx_blk + b, h)),
            pl.BlockSpec((CTX_LEN, HEAD_DIM), lambda b, h, i: (ctx_blk + b, h)),
        ],
        out_specs=[
            pl.BlockSpec((tq, gw), lambda b, h, i: (b * nq + i, h)),
            pl.BlockSpec((CTX_LEN, gw), lambda b, h, i: (b, h)),
        ],
        out_shape=[
            jax.ShapeDtypeStruct((ROWS_LAT, D_MODEL), BF16),
            jax.ShapeDtypeStruct((ROWS_CTX, D_MODEL), BF16),
        ],
        scratch_shapes=[
            pltpu.VMEM((2, GQA_GROUP * tq, tk), F32),
            pltpu.VMEM((2, GQA_GROUP * tq, tk), BF16),
            pltpu.VMEM((GQA_GROUP * tq, LANE), F32),
            pltpu.VMEM((GQA_GROUP * tq, LANE), F32),
            pltpu.VMEM((GQA_GROUP * tq, LANE), F32),
            pltpu.VMEM((GQA_GROUP * tq, HEAD_DIM), F32),
        ],
        compiler_params=_params("parallel", "parallel", "arbitrary"),
        name="gqa_attn",
    )(q, q, k, v, k, v)


def _nat_proj_kernel(x_ref, mod_ref, g_ref, w_ref, o_ref):
    h = _norm_mod(x_ref[...], g_ref[...], mod_ref[0:1, :], mod_ref[1:2, :]).astype(BF16)
    prescale = jnp.where(pl.program_id(0) == 0, QK_PRESCALE, 1.0)
    o_ref[...] = (_dot(h, w_ref[...]) * prescale).astype(BF16)


def _nat_proj(x, mods_l, g, w_qkv):
    tm = 512
    return pl.pallas_call(
        _nat_proj_kernel,
        grid=(3, ROWS_ALL // tm),
        in_specs=[
            pl.BlockSpec((tm, D_MODEL), lambda n, t: (t, 0)),
            pl.BlockSpec((None, 6, D_MODEL), lambda n, t: (_mod_row(t, tm), 0, 0)),
            pl.BlockSpec((1, D_MODEL), lambda n, t: (0, 0)),
            pl.BlockSpec((None, D_MODEL, D_MODEL), lambda n, t: (n, 0, 0)),
        ],
        out_specs=pl.BlockSpec((None, tm, D_MODEL), lambda n, t: (n, t, 0)),
        out_shape=jax.ShapeDtypeStruct((3, ROWS_ALL, D_MODEL), BF16),
        compiler_params=_params("parallel", "parallel"),
        name="nat_proj",
    )(x, mods_l, g.reshape(1, D_MODEL), w_qkv)


NAT_BLOCK_ROWS = 8
NAT_WIN_ROWS = 2 * NAT_BLOCK_ROWS
NAT_Q = NAT_BLOCK_ROWS * GRID_W
NAT_K = NAT_WIN_ROWS * GRID_W
NAT_SLABS = NAT_K // LANE
NAT_PAIR_TILES = 2 * NAT_KH - 2
NAT_TILE_LEFT_MASKED = NAT_PAIR_TILES
NAT_TILE_RIGHT_MASKED = NAT_PAIR_TILES + 1
NAT_TILES = NAT_PAIR_TILES + 2


def _nat_row_plan(cls, i):
    half = NAT_KH // 2
    if cls == "first":
        p, delta = (i, 0) if i < half else (half, i - half)
    elif cls == "mid":
        p, delta = half, i
    else:
        p, delta = (half, half + i) if i < half else (i, NAT_KH)
    if delta % 2 == 0:
        slabs = [delta // 2 + t for t in range(half)]
        tiles = [2 * t - p + NAT_KH - 1 for t in range(half)]
    else:
        assert p == half
        slabs = [(delta - 1) // 2 + t for t in range(half + 1)]
        tiles = ([NAT_TILE_LEFT_MASKED] + [2 * t - 1 - p + NAT_KH - 1 for t in range(1, half)]
                 + [NAT_TILE_RIGHT_MASKED])
    return slabs, tiles


def _nat_build_tiles(rpbw_ref, t2_ref):
    lane = lax.broadcasted_iota(jnp.int32, (GRID_W, LANE), 1)
    cq = lax.broadcasted_iota(jnp.int32, (GRID_W, LANE), 0)
    kc = lane % GRID_W
    cstart = jnp.clip(cq - NAT_KW // 2, 0, GRID_W - NAT_KW)
    inside = (kc >= cstart) & (kc < cstart + NAT_KW)
    for d in range(NAT_PAIR_TILES):
        w = jnp.broadcast_to(rpbw_ref[d:d + 1, :] * LOG2E, (GRID_W, LANE))
        t = pltpu.roll(w, LANE - (NAT_KW - 1), 1, stride=1, stride_axis=0)
        t2_ref[d] = jnp.where(inside, t, MASK_VALUE)
    half = NAT_KH // 2
    t2_ref[NAT_TILE_LEFT_MASKED] = jnp.where(lane < GRID_W, MASK_VALUE, t2_ref[half - 2])
    t2_ref[NAT_TILE_RIGHT_MASKED] = jnp.where(lane < GRID_W, t2_ref[NAT_KH + half - 2], MASK_VALUE)


def _nat_softmax_block(cls, s_ref, sc_ref, t2_ref, p_ref, pc_ref, m_ref, l_ref):
    n_ctx_slabs = CTX_LEN // LANE
    subs = []
    for i in range(NAT_BLOCK_ROWS):
        slabs, tiles = _nat_row_plan(cls, i)
        for a in range(NAT_SLABS):
            if a not in slabs:
                p_ref[i * GRID_W:(i + 1) * GRID_W, a * LANE:(a + 1) * LANE] = jnp.zeros((GRID_W, LANE), BF16)
        for sub in range(GRID_W // SOFTMAX_ROWS):
            trow = slice(sub * SOFTMAX_ROWS, (sub + 1) * SOFTMAX_ROWS)
            rows = slice(i * GRID_W + sub * SOFTMAX_ROWS, i * GRID_W + (sub + 1) * SOFTMAX_ROWS)
            subs.append((rows, trow, slabs, tiles))

    def scores(rows, trow, slabs, tiles):
        sv = [s_ref[rows, a * LANE:(a + 1) * LANE] + t2_ref[t, trow, :] for a, t in zip(slabs, tiles)]
        return sv + [sc_ref[rows, c * LANE:(c + 1) * LANE] for c in range(n_ctx_slabs)]

    for rows, trow, slabs, tiles in subs:
        m = jnp.max(functools.reduce(jnp.maximum, scores(rows, trow, slabs, tiles)), axis=-1, keepdims=True)
        m_ref[rows, :] = jnp.broadcast_to(m, (SOFTMAX_ROWS, LANE))
    for rows, trow, slabs, tiles in subs:
        m = m_ref[rows, :]
        pv = [jnp.exp2(s - m) for s in scores(rows, trow, slabs, tiles)]
        l_ref[rows, :] = functools.reduce(jnp.add, pv)
        for a, pa in zip(slabs, pv):
            p_ref[rows, a * LANE:(a + 1) * LANE] = pa.astype(BF16)
        for c in range(n_ctx_slabs):
            pc_ref[rows, c * LANE:(c + 1) * LANE] = pv[len(slabs) + c].astype(BF16)


def _nat_attn_kernel(q_ref, k_ref, v_ref, qc_ref, kc_ref, vc_ref, rpbw_ref, ol_ref, oc_ref,
                     t2_ref, s_ref, sc_ref, p_ref, pc_ref, m_ref, l_ref):
    kc, vc = kc_ref[...], vc_ref[...]
    p, l = _softmax_rows(_dot_nt(qc_ref[...], kc))
    oc_ref[...] = (_dot(p.astype(BF16), vc) / l).astype(BF16)

    _nat_build_tiles(rpbw_ref, t2_ref)

    def block_pair(specs):
        for n, (cls, q_tok, k_tok) in enumerate(specs):
            qb = q_ref[pl.ds(q_tok, NAT_Q), :]
            s_ref[n] = _dot_nt(qb, k_ref[pl.ds(k_tok, NAT_K), :])
            sc_ref[n] = _dot_nt(qb, kc)
        for n, (cls, q_tok, k_tok) in enumerate(specs):
            _nat_softmax_block(cls, s_ref.at[n], sc_ref.at[n], t2_ref, p_ref.at[n], pc_ref.at[n],
                               m_ref.at[n], l_ref.at[n])
        for n, (cls, q_tok, k_tok) in enumerate(specs):
            o = _dot(p_ref[n], v_ref[pl.ds(k_tok, NAT_K), :]) + _dot(pc_ref[n], vc)
            ol_ref[pl.ds(q_tok, NAT_Q), :] = (o / jnp.sum(l_ref[n], axis=-1, keepdims=True)).astype(BF16)

    block_pair([("first", 0, 0), ("last", SEQ - NAT_Q, SEQ - NAT_K)])
    half_win = (NAT_KH // 2) * GRID_W

    def body(j, carry):
        q0 = pl.multiple_of((2 * j + 1) * NAT_Q, NAT_Q)
        q1 = pl.multiple_of((2 * j + 2) * NAT_Q, NAT_Q)
        block_pair([("mid", q0, pl.multiple_of(q0 - half_win, half_win)),
                    ("mid", q1, pl.multiple_of(q1 - half_win, half_win))])
        return carry

    lax.fori_loop(0, (GRID_H // NAT_BLOCK_ROWS - 2) // 2, body, 0)


def _nat_attn(qkv, rpbw):
    ctx_blk = ROWS_LAT // CTX_LEN
    lat = lambda n: pl.BlockSpec((None, SEQ, HEAD_DIM), lambda b, h: (n, b, h))
    ctx = lambda n: pl.BlockSpec((None, CTX_LEN, HEAD_DIM), lambda b, h: (n, ctx_blk + b, h))
    return pl.pallas_call(
        _nat_attn_kernel,
        grid=(BATCH, NAT_HEADS),
        in_specs=[lat(0), lat(1), lat(2), ctx(0), ctx(1), ctx(2),
                  pl.BlockSpec((None, 2 * NAT_KH, LANE), lambda b, h: (h, 0, 0))],
        out_specs=[
            pl.BlockSpec((SEQ, HEAD_DIM), lambda b, h: (b, h)),
            pl.BlockSpec((CTX_LEN, HEAD_DIM), lambda b, h: (b, h)),
        ],
        out_shape=[
            jax.ShapeDtypeStruct((ROWS_LAT, D_MODEL), BF16),
            jax.ShapeDtypeStruct((ROWS_CTX, D_MODEL), BF16),
        ],
        scratch_shapes=[
            pltpu.VMEM((NAT_TILES, GRID_W, LANE), F32),
            pltpu.VMEM((2, NAT_Q, NAT_K), F32),
            pltpu.VMEM((2, NAT_Q, CTX_LEN), F32),
            pltpu.VMEM((2, NAT_Q, NAT_K), BF16),
            pltpu.VMEM((2, NAT_Q, CTX_LEN), BF16),
            pltpu.VMEM((2, NAT_Q, LANE), F32),
            pltpu.VMEM((2, NAT_Q, LANE), F32),
        ],
        compiler_params=_params("parallel", "parallel"),
        name="nat_attn",
    )(qkv, qkv, qkv, qkv, qkv, qkv, rpbw)


def _nat_rpb_pairs(rpb):
    padded = jnp.pad(rpb, ((0, 0), (0, 2), (0, GRID_W - (2 * NAT_KW - 1))))
    return jnp.concatenate([padded[:, :2 * NAT_KH], padded[:, 1:2 * NAT_KH + 1]], axis=-1)


def _rope_tables(tm):
    t = jnp.arange(SEQ)
    row = (t // GRID_W).astype(F32)
    col = (t % GRID_W).astype(F32)
    inv = ROPE_THETA ** (-jnp.arange(0, ROPE_AXIS_DIM, 2, dtype=F32) / ROPE_AXIS_DIM)
    ang = jnp.concatenate([row[:, None] * inv, col[:, None] * inv], axis=-1)
    cosf = jnp.repeat(jnp.cos(ang), 2, axis=-1)
    sign = jnp.tile(jnp.array([-1.0, 1.0], F32), HEAD_DIM // 2)
    sinf = jnp.repeat(jnp.sin(ang), 2, axis=-1) * sign
    cos_ext = jnp.concatenate([cosf, jnp.ones((tm, HEAD_DIM), F32)], axis=0)
    sin_ext = jnp.concatenate([sinf, jnp.zeros((tm, HEAD_DIM), F32)], axis=0)
    return cos_ext, sin_ext


def kernel(x, c, ctx, c_ctx, ada_w, ada_b, norm_g, ffn_w1, ffn_w3, ffn_w2, pool_w, pool_ls, gqa_wq, gqa_wk, gqa_wv, gqa_wo, gqa_qn, gqa_kn, nat_wq, nat_wk, nat_wv, nat_wo, nat_rpb, final_g):
    c8 =jnp.concatenate([c, c_ctx[None, :], jnp.zeros((8 - BATCH - 1, D_MODEL), F32)], axis=0)
    mods = _ada_all(c8, ada_w, ada_b)[:, :BATCH + 1].reshape(DEPTH, BATCH + 1, 6, D_MODEL)
    w1_all, w3_all, w2_all = ffn_w1.astype(BF16), ffn_w3.astype(BF16), ffn_w2.astype(BF16)
    pool_w_all = pool_w.astype(BF16)

    xa = None
    for i in range(DEPTH):
        kind, j = i % N_MIXERS, i // N_MIXERS
        last = i == DEPTH - 1
        n_rows = ROWS_LAT if last else ROWS_ALL
        g1, g2 = norm_g[i, 0], norm_g[i, 1]
        if kind == 0:
            x_lat, x_ctx = (x.reshape(ROWS_LAT, D_MODEL), ctx.reshape(ROWS_CTX, D_MODEL)) if i == 0 else (xa, xa)
            xa, h2 = _pool_layer(x_lat, x_ctx, mods[i], g1, g2, pool_w_all, j, pool_ls[j], n_rows)
        elif kind == 1:
            w_qkv = jnp.concatenate([gqa_wq[j], gqa_wk[j], gqa_wv[j]], axis=1).astype(BF16)
            cos_ext, sin_ext = _rope_tables(256)
            q, k, v = _gqa_proj(xa, mods[i], g1, w_qkv, gqa_qn[j], gqa_kn[j], cos_ext, sin_ext)
            o_lat, o_ctx = _gqa_attn(q, k, v)
            xa, h2 = _attn_out_layer(o_lat, o_ctx, xa, mods[i], gqa_wo[j].astype(BF16), g2)
        else:
            w_qkv = jnp.stack([nat_wq[j], nat_wk[j], nat_wv[j]], axis=0).astype(BF16)
            qkv = _nat_proj(xa, mods[i], g1, w_qkv)
            o_lat, o_ctx = _nat_attn(qkv, _nat_rpb_pairs(nat_rpb[j]))
            xa, h2 = _attn_out_layer(o_lat, o_ctx, xa, mods[i], nat_wo[j].astype(BF16), g2)
        xa = _ffn_layer(xa, h2, mods[i], w1_all, w3_all, w2_all, i, final_g, n_rows, last)
    return xa.reshape(BATCH, SEQ, D_MODEL)
```

```python
import functools

import jax
import jax.numpy as jnp
import numpy as np
from jax import lax
from jax.experimental import pallas as pl
from jax.experimental.pallas import tpu as pltpu

D_MODEL = 2048
BATCH = 2
SEQ = 4096
DEPTH = 4
GRID_W = 64
CTX_LEN = 256
N_MIXERS = 3
POOL_GROUPS = 4
POOL_WINDOWS = (2, 4, 8, 16)
POOL_GC = D_MODEL // POOL_GROUPS
HEAD_DIM = 128
GQA_HEADS = D_MODEL // HEAD_DIM
GQA_KV_HEADS = 4
GQA_GROUP = GQA_HEADS // GQA_KV_HEADS
ROPE_THETA = 10000.0
ROPE_AXIS_DIM = HEAD_DIM // 2
NAT_HEADS = D_MODEL // HEAD_DIM
NAT_KH = 8
NAT_KW = 16
FFN_HIDDEN = -(-8 * D_MODEL // (3 * 256)) * 256
NORM_EPS = 1e-6

ROWS_LAT = BATCH * SEQ
ROWS_CTX = BATCH * CTX_LEN
ROWS_ALL = ROWS_LAT + ROWS_CTX
GRID_H = SEQ // GRID_W
ATTN_SCALE = HEAD_DIM ** -0.5
LOG2E = 1.4426950408889634
QK_PRESCALE = ATTN_SCALE * LOG2E
MASK_VALUE = -1e30
LANE = 128
POOL_HALO = 8
SOFTMAX_ROWS = 32

VMEM_LIMIT_BYTES = 56 * 1024 * 1024

F32 = jnp.float32
BF16 = jnp.bfloat16


def _params(*sem):
    return pltpu.CompilerParams(dimension_semantics=sem, vmem_limit_bytes=VMEM_LIMIT_BYTES)


def _dot(a, b):
    return jnp.dot(a, b, preferred_element_type=F32)


def _dot_nt(a, b):
    return lax.dot_general(a, b, (((1,), (1,)), ((), ())), preferred_element_type=F32)


def _silu(a):
    return a * (1.0 / (1.0 + jnp.exp(-a)))


def _rms(x, g):
    return x * lax.rsqrt(jnp.mean(x * x, axis=-1, keepdims=True) + NORM_EPS) * g


def _norm_mod(x, g, shift, scale):
    return _rms(x, g) * (1.0 + scale) + shift


def _mod_row(t, tm):
    return jnp.where(t < ROWS_LAT // tm, t // (SEQ // tm), BATCH)


def _ada_kernel(c_ref, w_ref, b_ref, o_ref):
    s = _silu(c_ref[...])
    o_ref[...] = _dot(s.astype(BF16), w_ref[...].astype(BF16)) + b_ref[...]


def _ada_all(c8, ada_w, ada_b):
    tn = 1024
    return pl.pallas_call(
        _ada_kernel,
        grid=(DEPTH, 6 * D_MODEL // tn),
        in_specs=[
            pl.BlockSpec((8, D_MODEL), lambda l, n: (0, 0)),
            pl.BlockSpec((None, D_MODEL, tn), lambda l, n: (l, 0, n)),
            pl.BlockSpec((None, 1, tn), lambda l, n: (l, 0, n)),
        ],
        out_specs=pl.BlockSpec((None, 8, tn), lambda l, n: (l, 0, n)),
        out_shape=jax.ShapeDtypeStruct((DEPTH, 8, 6 * D_MODEL), F32),
        compiler_params=_params("parallel", "parallel"),
        name="ada",
    )(c8, ada_w, ada_b.reshape(DEPTH, 1, 6 * D_MODEL))


def _pool_bands(tm):
    t = np.arange(tm)[:, None]
    j = np.arange(tm + 2 * POOL_HALO)[None, :]
    pos = np.where(j < tm, j, np.where(j < tm + POOL_HALO, j - tm - POOL_HALO, j - POOL_HALO))
    bands = [(pos >= t - w // 2) & (pos < t + w - w // 2) for w in POOL_WINDOWS]
    return jnp.asarray(np.stack(bands).astype(np.float32), BF16)


def _split_bf16(v):
    hi = v.astype(BF16)
    return hi, (v - hi.astype(F32)).astype(BF16)


def _pool_kernel(xl_ref, xc_ref, xp_ref, xn_ref, mod_ref, g1_ref, g2_ref, band_ref, w_ref, ls_ref,
                 o_ref, h2_ref, hi_ref, lo_ref, *, tm):
    t = pl.program_id(0)
    shift, scale, gate = mod_ref[0:1, :], mod_ref[1:2, :], mod_ref[2:3, :]
    g = g1_ref[...]

    n_lat_tiles = ROWS_LAT // tm
    is_lat = t < n_lat_tiles
    tiles_in_seq = jnp.where(is_lat, SEQ // tm, CTX_LEN // tm)
    pos_tile = jnp.where(is_lat, t % (SEQ // tm), (t - n_lat_tiles) % (CTX_LEN // tm))
    first = pos_tile == 0
    last = pos_tile == tiles_in_seq - 1

    x = jnp.where(is_lat, xl_ref[...], xc_ref[...])
    h = _norm_mod(x, g, shift, scale)
    halo = jnp.concatenate([jnp.where(first, 0.0, _norm_mod(xp_ref[...], g, shift, scale)),
                            jnp.where(last, 0.0, _norm_mod(xn_ref[...], g, shift, scale))], axis=0)
    hi_ref[0:tm, :], lo_ref[0:tm, :] = _split_bf16(h)
    hi_ref[tm:, :], lo_ref[tm:, :] = _split_bf16(halo)

    pos = pos_tile * tm + lax.broadcasted_iota(jnp.int32, (tm, 1), 0)
    n = tiles_in_seq * tm
    for k in range(POOL_GROUPS):
        w = POOL_WINDOWS[k]
        lanes = slice(k * POOL_GC, (k + 1) * POOL_GC)
        wsum = _dot(band_ref[k], hi_ref[:, lanes]) + _dot(band_ref[k], lo_ref[:, lanes])
        cnt = jnp.minimum(pos + (w - w // 2), n) - jnp.maximum(pos - w // 2, 0)
        pooled = wsum / cnt.astype(F32) - h[:, lanes]
        y = _dot(pooled.astype(BF16), w_ref[k]) * ls_ref[:, lanes]
        o_ref[:, lanes] = x[:, lanes] + gate[:, lanes] * y
    h2_ref[...] = _norm_mod(o_ref[...], g2_ref[...], mod_ref[3:4, :], mod_ref[4:5, :]).astype(BF16)


def _pool_layer(x_lat, x_ctx, mods_l, g1, g2, w_all, j, ls, n_rows):
    tm = 256
    n_lat = ROWS_LAT // tm
    nb8 = x_lat.shape[0] // POOL_HALO
    ctx_blk0 = (x_ctx.shape[0] - ROWS_CTX) // tm
    halo_blk = lambda b: jnp.clip(b, 0, nb8 - 1)
    row = lambda v: v.reshape(1, D_MODEL)
    return pl.pallas_call(
        functools.partial(_pool_kernel, tm=tm),
        grid=(n_rows // tm,),
        in_specs=[
            pl.BlockSpec((tm, D_MODEL), lambda t: (jnp.minimum(t, n_lat - 1), 0)),
            pl.BlockSpec((tm, D_MODEL), lambda t: (ctx_blk0 + jnp.maximum(t - n_lat, 0), 0)),
            pl.BlockSpec((POOL_HALO, D_MODEL), lambda t: (halo_blk(t * (tm // POOL_HALO) - 1), 0)),
            pl.BlockSpec((POOL_HALO, D_MODEL), lambda t: (halo_blk((t + 1) * (tm // POOL_HALO)), 0)),
            pl.BlockSpec((None, 6, D_MODEL), lambda t: (_mod_row(t, tm), 0, 0)),
            pl.BlockSpec((1, D_MODEL), lambda t: (0, 0)),
            pl.BlockSpec((1, D_MODEL), lambda t: (0, 0)),
            pl.BlockSpec((POOL_GROUPS, tm, tm + 2 * POOL_HALO), lambda t: (0, 0, 0)),
            pl.BlockSpec((None, POOL_GROUPS, POOL_GC, POOL_GC), lambda t: (j, 0, 0, 0)),
            pl.BlockSpec((1, D_MODEL), lambda t: (0, 0)),
        ],
        out_specs=[pl.BlockSpec((tm, D_MODEL), lambda t: (t, 0)), pl.BlockSpec((tm, D_MODEL), lambda t: (t, 0))],
        out_shape=[jax.ShapeDtypeStruct((n_rows, D_MODEL), F32), jax.ShapeDtypeStruct((n_rows, D_MODEL), BF16)],
        scratch_shapes=[pltpu.VMEM((tm + 2 * POOL_HALO, D_MODEL), BF16),
                        pltpu.VMEM((tm + 2 * POOL_HALO, D_MODEL), BF16)],
        compiler_params=_params("parallel"),
        name="pool_layer",
    )(x_lat, x_ctx, x_lat, x_lat, mods_l, row(g1), row(g2), _pool_bands(tm), w_all, row(ls))


def _ffn_kernel(x_ref, h_ref, mod_ref, w1_ref, w3_ref, w2_ref, fg_ref, o_ref, acc_ref, *, nf, final):
    f = pl.program_id(1)

    @pl.when(f == 0)
    def _():
        acc_ref[...] = jnp.zeros_like(acc_ref)

    h = h_ref[...]
    a = _dot(h, w1_ref[...])
    b = _dot(h, w3_ref[...])
    acc_ref[...] += _dot((_silu(a) * b).astype(BF16), w2_ref[...])

    @pl.when(f == nf - 1)
    def _():
        y = x_ref[...] + mod_ref[5:6, :] * acc_ref[...]
        if final:
            y = _rms(y, fg_ref[...])
        o_ref[...] = y


def _ffn_layer(x, h, mods_l, w1_all, w3_all, w2_all, layer, final_g, n_rows, final):
    tm, tf = 512, 512
    nf = FFN_HIDDEN // tf
    return pl.pallas_call(
        functools.partial(_ffn_kernel, nf=nf, final=final),
        grid=(n_rows // tm, nf),
        in_specs=[
            pl.BlockSpec((tm, D_MODEL), lambda t, f: (t, 0)),
            pl.BlockSpec((tm, D_MODEL), lambda t, f: (t, 0)),
            pl.BlockSpec((None, 6, D_MODEL), lambda t, f: (_mod_row(t, tm), 0, 0)),
            pl.BlockSpec((None, D_MODEL, tf), lambda t, f: (layer, 0, f)),
            pl.BlockSpec((None, D_MODEL, tf), lambda t, f: (layer, 0, f)),
            pl.BlockSpec((None, tf, D_MODEL), lambda t, f: (layer, f, 0)),
            pl.BlockSpec((1, D_MODEL), lambda t, f: (0, 0)),
        ],
        out_specs=pl.BlockSpec((tm, D_MODEL), lambda t, f: (t, 0)),
        out_shape=jax.ShapeDtypeStruct((n_rows, D_MODEL), F32),
        scratch_shapes=[pltpu.VMEM((tm, D_MODEL), F32)],
        compiler_params=_params("parallel", "arbitrary"),
        name="ffn_layer",
    )(x, h, mods_l, w1_all, w3_all, w2_all, final_g.reshape(1, D_MODEL))


def _attn_out_kernel(ol_ref, oc_ref, x_ref, mod_ref, wo_ref, g2_ref, o_ref, h2_ref, *, tm):
    t = pl.program_id(0)
    o = jnp.where(t < ROWS_LAT // tm, ol_ref[...], oc_ref[...])
    y = x_ref[...] + mod_ref[2:3, :] * _dot(o, wo_ref[...])
    o_ref[...] = y
    h2_ref[...] = _norm_mod(y, g2_ref[...], mod_ref[3:4, :], mod_ref[4:5, :]).astype(BF16)


def _attn_out_layer(o_lat, o_ctx, x, mods_l, wo, g2):
    tm = 256
    n_lat = ROWS_LAT // tm
    return pl.pallas_call(
        functools.partial(_attn_out_kernel, tm=tm),
        grid=(ROWS_ALL // tm,),
        in_specs=[
            pl.BlockSpec((tm, D_MODEL), lambda t: (jnp.minimum(t, n_lat - 1), 0)),
            pl.BlockSpec((tm, D_MODEL), lambda t: (jnp.maximum(t - n_lat, 0), 0)),
            pl.BlockSpec((tm, D_MODEL), lambda t: (t, 0)),
            pl.BlockSpec((None, 6, D_MODEL), lambda t: (_mod_row(t, tm), 0, 0)),
            pl.BlockSpec((D_MODEL, D_MODEL), lambda t: (0, 0)),
            pl.BlockSpec((1, D_MODEL), lambda t: (0, 0)),
        ],
        out_specs=[pl.BlockSpec((tm, D_MODEL), lambda t: (t, 0)), pl.BlockSpec((tm, D_MODEL), lambda t: (t, 0))],
        out_shape=[jax.ShapeDtypeStruct((ROWS_ALL, D_MODEL), F32), jax.ShapeDtypeStruct((ROWS_ALL, D_MODEL), BF16)],
        compiler_params=_params("parallel"),
        name="attn_out",
    )(o_lat, o_ctx, x, mods_l, wo, g2.reshape(1, D_MODEL))


def _rope(xh, cosf, sinf, even):
    swapped = jnp.where(even, pltpu.roll(xh, HEAD_DIM - 1, 1), pltpu.roll(xh, 1, 1))
    return xh * cosf + swapped * sinf


def _gqa_proj_kernel(x_ref, mod_ref, g_ref, w_ref, qn_ref, kn_ref, cos_ref, sin_ref, q_ref, k_ref, v_ref, *, tm):
    h = _norm_mod(x_ref[...], g_ref[...], mod_ref[0:1, :], mod_ref[1:2, :]).astype(BF16)
    qkv = _dot(h, w_ref[...])
    cosf, sinf = cos_ref[...], sin_ref[...]
    even = lax.broadcasted_iota(jnp.int32, (tm, HEAD_DIM), 1) % 2 == 0
    for hd in range(GQA_HEADS):
        lanes = slice(hd * HEAD_DIM, (hd + 1) * HEAD_DIM)
        qh = _rope(_rms(qkv[:, lanes], qn_ref[...]), cosf, sinf, even)
        q_ref[:, lanes] = (qh * QK_PRESCALE).astype(BF16)
    k0 = GQA_HEADS * HEAD_DIM
    for hd in range(GQA_KV_HEADS):
        lanes = slice(hd * HEAD_DIM, (hd + 1) * HEAD_DIM)
        kh = qkv[:, k0 + hd * HEAD_DIM:k0 + (hd + 1) * HEAD_DIM]
        k_ref[:, lanes] = _rope(_rms(kh, kn_ref[...]), cosf, sinf, even).astype(BF16)
    v_ref[...] = qkv[:, k0 + GQA_KV_HEADS * HEAD_DIM:].astype(BF16)


def _gqa_proj(x, mods_l, g, w_qkv, qn, kn, cos_ext, sin_ext):
    tm = 256
    dq, dkv = GQA_HEADS * HEAD_DIM, GQA_KV_HEADS * HEAD_DIM
    n_lat = ROWS_LAT // tm
    rope_blk = lambda t: (jnp.where(t < n_lat, t % (SEQ // tm), SEQ // tm), 0)
    return pl.pallas_call(
        functools.partial(_gqa_proj_kernel, tm=tm),
        grid=(ROWS_ALL // tm,),
        in_specs=[
            pl.BlockSpec((tm, D_MODEL), lambda t: (t, 0)),
            pl.BlockSpec((None, 6, D_MODEL), lambda t: (_mod_row(t, tm), 0, 0)),
            pl.BlockSpec((1, D_MODEL), lambda t: (0, 0)),
            pl.BlockSpec((D_MODEL, dq + 2 * dkv), lambda t: (0, 0)),
            pl.BlockSpec((1, HEAD_DIM), lambda t: (0, 0)),
            pl.BlockSpec((1, HEAD_DIM), lambda t: (0, 0)),
            pl.BlockSpec((tm, HEAD_DIM), rope_blk),
            pl.BlockSpec((tm, HEAD_DIM), rope_blk),
        ],
        out_specs=[
            pl.BlockSpec((tm, dq), lambda t: (t, 0)),
            pl.BlockSpec((tm, dkv), lambda t: (t, 0)),
            pl.BlockSpec((tm, dkv), lambda t: (t, 0)),
        ],
        out_shape=[
            jax.ShapeDtypeStruct((ROWS_ALL, dq), BF16),
            jax.ShapeDtypeStruct((ROWS_ALL, dkv), BF16),
            jax.ShapeDtypeStruct((ROWS_ALL, dkv), BF16),
        ],
        compiler_params=_params("parallel"),
        name="gqa_proj",
    )(x, mods_l, g.reshape(1, D_MODEL), w_qkv, qn.reshape(1, HEAD_DIM), kn.reshape(1, HEAD_DIM), cos_ext, sin_ext)


def _softmax_rows(s):
    m = jnp.max(s, axis=-1, keepdims=True)
    p = jnp.exp2(s - m)
    return p, jnp.sum(p, axis=-1, keepdims=True)


def _stack_heads(ref, n):
    return jnp.concatenate([ref[:, g * HEAD_DIM:(g + 1) * HEAD_DIM] for g in range(n)], axis=0)


def _online_softmax_chunk(s_ref, p_ref, m_ref, l_ref, a_ref, n_rows, width):
    nslab = width // LANE
    blocks = [slice(i * SOFTMAX_ROWS, (i + 1) * SOFTMAX_ROWS) for i in range(n_rows // SOFTMAX_ROWS)]
    for rows in blocks:
        mc = functools.reduce(jnp.maximum, [s_ref[rows, c * LANE:(c + 1) * LANE] for c in range(nslab)])
        m_prev = m_ref[rows, :]
        m_new = jnp.maximum(m_prev, jnp.max(mc, axis=-1, keepdims=True))
        a_ref[rows, :] = jnp.exp2(m_prev - m_new)
        m_ref[rows, :] = m_new
    for rows in blocks:
        m_new = m_ref[rows, :]
        lsum = a_ref[rows, :] * l_ref[rows, :]
        for c in range(nslab):
            p = jnp.exp2(s_ref[rows, c * LANE:(c + 1) * LANE] - m_new)
            lsum = lsum + p
            p_ref[rows, c * LANE:(c + 1) * LANE] = p.astype(BF16)
        l_ref[rows, :] = lsum


FFN_WEIGHT_SHAPES = ((D_MODEL, FFN_HIDDEN), (D_MODEL, FFN_HIDDEN), (FFN_HIDDEN, D_MODEL))


def _weight_cast_specs(first_layer, n_layers, n_steps, step_id):
    per_layer = n_steps // n_layers
    in_specs, out_specs, out_shapes = [], [], []
    for rows, cols in FFN_WEIGHT_SHAPES:
        blk = (None, rows // per_layer, cols)
        in_specs.append(pl.BlockSpec(
            blk, lambda *g: (first_layer + step_id(*g) // per_layer, step_id(*g) % per_layer, 0)))
        out_specs.append(pl.BlockSpec(blk, lambda *g: (step_id(*g) // per_layer, step_id(*g) % per_layer, 0)))
        out_shapes.append(jax.ShapeDtypeStruct((n_layers, rows, cols), BF16))
    return in_specs, out_specs, out_shapes


def _cast_weight_slabs(src_refs, dst_refs):
    for src, dst in zip(src_refs, dst_refs):
        dst[...] = src[...].astype(BF16)


def _gqa_attn_kernel(q_ref, qc_ref, k_ref, v_ref, kc_ref, vc_ref, w1f_ref, w3f_ref, w2f_ref,
                     ol_ref, oc_ref, w1b_ref, w3b_ref, w2b_ref,
                     s_ref, p_ref, m_ref, l_ref, a_ref, acc_ref, *, tq, tk, n_sub):
    _cast_weight_slabs((w1f_ref, w3f_ref, w2f_ref), (w1b_ref, w3b_ref, w2b_ref))
    kc, vc = kc_ref[...], vc_ref[...]
    sub_q = tq // n_sub
    n_rows = GQA_GROUP * sub_q

    @pl.when(pl.program_id(2) == 0)
    def _():
        p, l = _softmax_rows(_dot_nt(_stack_heads(qc_ref, GQA_GROUP), kc))
        oc = _dot(p.astype(BF16), vc) / l
        for g in range(GQA_GROUP):
            oc_ref[:, g * HEAD_DIM:(g + 1) * HEAD_DIM] = oc[g * CTX_LEN:(g + 1) * CTX_LEN].astype(BF16)

    qs = [_stack_heads(q_ref.at[u * sub_q:(u + 1) * sub_q, :], GQA_GROUP) for u in range(n_sub)]
    m_ref[...] = jnp.full_like(m_ref, MASK_VALUE)
    l_ref[...] = jnp.zeros_like(l_ref)
    acc_ref[...] = jnp.zeros_like(acc_ref)

    chunks = [(k_ref, v_ref, j * tk, tk) for j in range(SEQ // tk)] + [(kc_ref, vc_ref, 0, CTX_LEN)]
    for n, (kr, vr, off, width) in enumerate(chunks):
        for u in range(n_sub):
            sb, pb = s_ref.at[u, n % 2], p_ref.at[u, n % 2]
            sb[:, :width] = _dot_nt(qs[u], kr[off:off + width, :])
            _online_softmax_chunk(sb, pb, m_ref.at[u], l_ref.at[u], a_ref.at[u], n_rows, width)
            acc_ref[u] = a_ref[u] * acc_ref[u] + _dot(pb[:, :width], vr[off:off + width, :])

    for u in range(n_sub):
        o = acc_ref[u] / jnp.sum(l_ref[u], axis=-1, keepdims=True)
        for g in range(GQA_GROUP):
            ol_ref[u * sub_q:(u + 1) * sub_q, g * HEAD_DIM:(g + 1) * HEAD_DIM] = (
                o[g * sub_q:(g + 1) * sub_q].astype(BF16))


def _gqa_attn(q, k, v, ffn_weights, first_layer, n_layers):
    tq, tk, n_sub = 512, 1024, 2
    sub_rows = GQA_GROUP * tq // n_sub
    gw = GQA_GROUP * HEAD_DIM
    nq = SEQ // tq
    ctx_blk = ROWS_LAT // CTX_LEN
    w_in, w_out, w_shapes = _weight_cast_specs(first_layer, n_layers, BATCH * GQA_KV_HEADS * nq,
                                               lambda b, h, i: (b * GQA_KV_HEADS + h) * nq + i)
    return pl.pallas_call(
        functools.partial(_gqa_attn_kernel, tq=tq, tk=tk, n_sub=n_sub),
        grid=(BATCH, GQA_KV_HEADS, nq),
        in_specs=[
            pl.BlockSpec((tq, gw), lambda b, h, i: (b * nq + i, h)),
            pl.BlockSpec((CTX_LEN, gw), lambda b, h, i: (ctx_blk + b, h)),
            pl.BlockSpec((SEQ, HEAD_DIM), lambda b, h, i: (b, h)),
            pl.BlockSpec((SEQ, HEAD_DIM), lambda b, h, i: (b, h)),
            pl.BlockSpec((CTX_LEN, HEAD_DIM), lambda b, h, i: (ctx_blk + b, h)),
            pl.BlockSpec((CTX_LEN, HEAD_DIM), lambda b, h, i: (ctx_blk + b, h)),
        ] + w_in,
        out_specs=[
            pl.BlockSpec((tq, gw), lambda b, h, i: (b * nq + i, h)),
            pl.BlockSpec((CTX_LEN, gw), lambda b, h, i: (b, h)),
        ] + w_out,
        out_shape=[
            jax.ShapeDtypeStruct((ROWS_LAT, D_MODEL), BF16),
            jax.ShapeDtypeStruct((ROWS_CTX, D_MODEL), BF16),
        ] + w_shapes,
        scratch_shapes=[
            pltpu.VMEM((n_sub, 2, sub_rows, tk), F32),
            pltpu.VMEM((n_sub, 2, sub_rows, tk), BF16),
            pltpu.VMEM((n_sub, sub_rows, LANE), F32),
            pltpu.VMEM((n_sub, sub_rows, LANE), F32),
            pltpu.VMEM((n_sub, sub_rows, LANE), F32),
            pltpu.VMEM((n_sub, sub_rows, HEAD_DIM), F32),
        ],
        compiler_params=_params("parallel", "parallel", "arbitrary"),
        name="gqa_attn",
    )(q, q, k, v, k, v, *ffn_weights)


def _nat_proj_kernel(x_ref, mod_ref, g_ref, w_ref, o_ref, h_ref):
    n = pl.program_id(1)

    @pl.when(n == 0)
    def _():
        h_ref[...] = _norm_mod(x_ref[...], g_ref[...], mod_ref[0:1, :], mod_ref[1:2, :]).astype(BF16)

    prescale = jnp.where(n == 0, QK_PRESCALE, 1.0)
    o_ref[...] = (_dot(h_ref[...], w_ref[...]) * prescale).astype(BF16)


def _nat_proj(x, mods_l, g, w_qkv):
    tm = 512
    return pl.pallas_call(
        _nat_proj_kernel,
        grid=(ROWS_ALL // tm, 3),
        in_specs=[
            pl.BlockSpec((tm, D_MODEL), lambda t, n: (t, 0)),
            pl.BlockSpec((None, 6, D_MODEL), lambda t, n: (_mod_row(t, tm), 0, 0)),
            pl.BlockSpec((1, D_MODEL), lambda t, n: (0, 0)),
            pl.BlockSpec((None, D_MODEL, D_MODEL), lambda t, n: (n, 0, 0)),
        ],
        out_specs=pl.BlockSpec((None, tm, D_MODEL), lambda t, n: (n, t, 0)),
        out_shape=jax.ShapeDtypeStruct((3, ROWS_ALL, D_MODEL), BF16),
        scratch_shapes=[pltpu.VMEM((tm, D_MODEL), BF16)],
        compiler_params=_params("parallel", "arbitrary"),
        name="nat_proj",
    )(x, mods_l, g.reshape(1, D_MODEL), w_qkv)


NAT_BLOCK_ROWS = 8
NAT_WIN_ROWS = 2 * NAT_BLOCK_ROWS
NAT_Q = NAT_BLOCK_ROWS * GRID_W
NAT_K = NAT_WIN_ROWS * GRID_W
NAT_SLABS = NAT_K // LANE
NAT_PAIR_TILES = 2 * NAT_KH - 2
NAT_TILE_LEFT_MASKED = NAT_PAIR_TILES
NAT_TILE_RIGHT_MASKED = NAT_PAIR_TILES + 1
NAT_TILES = NAT_PAIR_TILES + 2
NAT_BUFFERS = 3


def _nat_row_plan(cls, i):
    half = NAT_KH // 2
    if cls == "first":
        p, delta = (i, 0) if i < half else (half, i - half)
    elif cls == "mid":
        p, delta = half, i
    else:
        p, delta = (half, half + i) if i < half else (i, NAT_KH)
    if delta % 2 == 0:
        slabs = [delta // 2 + t for t in range(half)]
        tiles = [2 * t - p + NAT_KH - 1 for t in range(half)]
    else:
        assert p == half
        slabs = [(delta - 1) // 2 + t for t in range(half + 1)]
        tiles = ([NAT_TILE_LEFT_MASKED] + [2 * t - 1 - p + NAT_KH - 1 for t in range(1, half)]
                 + [NAT_TILE_RIGHT_MASKED])
    return slabs, tiles


def _nat_build_tiles(rpbw_ref, t2_ref):
    lane = lax.broadcasted_iota(jnp.int32, (GRID_W, LANE), 1)
    cq = lax.broadcasted_iota(jnp.int32, (GRID_W, LANE), 0)
    kc = lane % GRID_W
    cstart = jnp.clip(cq - NAT_KW // 2, 0, GRID_W - NAT_KW)
    inside = (kc >= cstart) & (kc < cstart + NAT_KW)
    for d in range(NAT_PAIR_TILES):
        w = jnp.broadcast_to(rpbw_ref[d:d + 1, :] * LOG2E, (GRID_W, LANE))
        t = pltpu.roll(w, LANE - (NAT_KW - 1), 1, stride=1, stride_axis=0)
        t2_ref[d] = jnp.where(inside, t, MASK_VALUE)
    half = NAT_KH // 2
    t2_ref[NAT_TILE_LEFT_MASKED] = jnp.where(lane < GRID_W, MASK_VALUE, t2_ref[half - 2])
    t2_ref[NAT_TILE_RIGHT_MASKED] = jnp.where(lane < GRID_W, t2_ref[NAT_KH + half - 2], MASK_VALUE)


def _nat_softmax_block(cls, s_ref, sc_ref, t2_ref, p_ref, pc_ref, m_ref, l_ref):
    n_ctx_slabs = CTX_LEN // LANE
    subs = []
    for i in range(NAT_BLOCK_ROWS):
        slabs, tiles = _nat_row_plan(cls, i)
        for a in range(NAT_SLABS):
            if a not in slabs:
                p_ref[i * GRID_W:(i + 1) * GRID_W, a * LANE:(a + 1) * LANE] = jnp.zeros((GRID_W, LANE), BF16)
        for sub in range(GRID_W // SOFTMAX_ROWS):
            trow = slice(sub * SOFTMAX_ROWS, (sub + 1) * SOFTMAX_ROWS)
            rows = slice(i * GRID_W + sub * SOFTMAX_ROWS, i * GRID_W + (sub + 1) * SOFTMAX_ROWS)
            subs.append((rows, trow, slabs, tiles))

    def scores(rows, trow, slabs, tiles):
        sv = [s_ref[rows, a * LANE:(a + 1) * LANE] + t2_ref[t, trow, :] for a, t in zip(slabs, tiles)]
        return sv + [sc_ref[rows, c * LANE:(c + 1) * LANE] for c in range(n_ctx_slabs)]

    for rows, trow, slabs, tiles in subs:
        m = jnp.max(functools.reduce(jnp.maximum, scores(rows, trow, slabs, tiles)), axis=-1, keepdims=True)
        m_ref[rows, :] = jnp.broadcast_to(m, (SOFTMAX_ROWS, LANE))
    for rows, trow, slabs, tiles in subs:
        m = m_ref[rows, :]
        pv = [jnp.exp2(s - m) for s in scores(rows, trow, slabs, tiles)]
        l_ref[rows, :] = functools.reduce(jnp.add, pv)
        for a, pa in zip(slabs, pv):
            p_ref[rows, a * LANE:(a + 1) * LANE] = pa.astype(BF16)
        for c in range(n_ctx_slabs):
            pc_ref[rows, c * LANE:(c + 1) * LANE] = pv[len(slabs) + c].astype(BF16)


def _nat_attn_kernel(q_ref, k_ref, v_ref, qc_ref, kc_ref, vc_ref, rpbw_ref, w1f_ref, w3f_ref, w2f_ref,
                     ol_ref, oc_ref, w1b_ref, w3b_ref, w2b_ref,
                     t2_ref, s_ref, sc_ref, p_ref, pc_ref, m_ref, l_ref):
    _cast_weight_slabs((w1f_ref, w3f_ref, w2f_ref), (w1b_ref, w3b_ref, w2b_ref))
    kc, vc = kc_ref[...], vc_ref[...]
    p, l = _softmax_rows(_dot_nt(qc_ref[...], kc))
    oc_ref[...] = (_dot(p.astype(BF16), vc) / l).astype(BF16)

    _nat_build_tiles(rpbw_ref, t2_ref)

    n_blocks = GRID_H // NAT_BLOCK_ROWS
    half_win = (NAT_KH // 2) * GRID_W
    specs = [("first", 0, 0)]
    specs += [("mid", j * NAT_Q, j * NAT_Q - half_win) for j in range(1, n_blocks - 1)]
    specs += [("last", SEQ - NAT_Q, SEQ - NAT_K)]

    def scores(n):
        _, q_tok, k_tok = specs[n]
        qb = q_ref[q_tok:q_tok + NAT_Q, :]
        s_ref[n % NAT_BUFFERS] = _dot_nt(qb, k_ref[k_tok:k_tok + NAT_K, :])
        sc_ref[n % NAT_BUFFERS] = _dot_nt(qb, kc)

    def finish(n):
        cls, q_tok, k_tok = specs[n]
        b = n % NAT_BUFFERS
        _nat_softmax_block(cls, s_ref.at[b], sc_ref.at[b], t2_ref, p_ref.at[b], pc_ref.at[b], m_ref.at[b], l_ref.at[b])
        o = _dot(p_ref[b], v_ref[k_tok:k_tok + NAT_K, :]) + _dot(pc_ref[b], vc)
        ol_ref[q_tok:q_tok + NAT_Q, :] = (o / jnp.sum(l_ref[b], axis=-1, keepdims=True)).astype(BF16)

    scores(0)
    for n in range(n_blocks):
        if n + 1 < n_blocks:
            scores(n + 1)
        finish(n)


def _nat_attn(qkv, rpbw, ffn_weights, first_layer, n_layers):
    ctx_blk = ROWS_LAT // CTX_LEN
    lat = lambda n: pl.BlockSpec((None, SEQ, HEAD_DIM), lambda b, h: (n, b, h))
    ctx = lambda n: pl.BlockSpec((None, CTX_LEN, HEAD_DIM), lambda b, h: (n, ctx_blk + b, h))
    w_in, w_out, w_shapes = _weight_cast_specs(first_layer, n_layers, BATCH * NAT_HEADS,
                                               lambda b, h: b * NAT_HEADS + h)
    return pl.pallas_call(
        _nat_attn_kernel,
        grid=(BATCH, NAT_HEADS),
        in_specs=[lat(0), lat(1), lat(2), ctx(0), ctx(1), ctx(2),
                  pl.BlockSpec((None, 2 * NAT_KH, LANE), lambda b, h: (h, 0, 0))] + w_in,
        out_specs=[
            pl.BlockSpec((SEQ, HEAD_DIM), lambda b, h: (b, h)),
            pl.BlockSpec((CTX_LEN, HEAD_DIM), lambda b, h: (b, h)),
        ] + w_out,
        out_shape=[
            jax.ShapeDtypeStruct((ROWS_LAT, D_MODEL), BF16),
            jax.ShapeDtypeStruct((ROWS_CTX, D_MODEL), BF16),
        ] + w_shapes,
        scratch_shapes=[
            pltpu.VMEM((NAT_TILES, GRID_W, LANE), F32),
            pltpu.VMEM((NAT_BUFFERS, NAT_Q, NAT_K), F32),
            pltpu.VMEM((NAT_BUFFERS, NAT_Q, CTX_LEN), F32),
            pltpu.VMEM((NAT_BUFFERS, NAT_Q, NAT_K), BF16),
            pltpu.VMEM((NAT_BUFFERS, NAT_Q, CTX_LEN), BF16),
            pltpu.VMEM((NAT_BUFFERS, NAT_Q, LANE), F32),
            pltpu.VMEM((NAT_BUFFERS, NAT_Q, LANE), F32),
        ],
        compiler_params=_params("parallel", "parallel"),
        name="nat_attn",
    )(qkv, qkv, qkv, qkv, qkv, qkv, rpbw, *ffn_weights)


def _nat_rpb_pairs(rpb):
    padded = jnp.pad(rpb, ((0, 0), (0, 2), (0, GRID_W - (2 * NAT_KW - 1))))
    return jnp.concatenate([padded[:, :2 * NAT_KH], padded[:, 1:2 * NAT_KH + 1]], axis=-1)


def _rope_tables(tm):
    t = jnp.arange(SEQ)
    row = (t // GRID_W).astype(F32)
    col = (t % GRID_W).astype(F32)
    inv = ROPE_THETA ** (-jnp.arange(0, ROPE_AXIS_DIM, 2, dtype=F32) / ROPE_AXIS_DIM)
    ang = jnp.concatenate([row[:, None] * inv, col[:, None] * inv], axis=-1)
    cosf = jnp.repeat(jnp.cos(ang), 2, axis=-1)
    sign = jnp.tile(jnp.array([-1.0, 1.0], F32), HEAD_DIM // 2)
    sinf = jnp.repeat(jnp.sin(ang), 2, axis=-1) * sign
    cos_ext = jnp.concatenate([cosf, jnp.ones((tm, HEAD_DIM), F32)], axis=0)
    sin_ext = jnp.concatenate([sinf, jnp.zeros((tm, HEAD_DIM), F32)], axis=0)
    return cos_ext, sin_ext


def kernel(x, c, ctx, c_ctx, ada_w, ada_b, norm_g, ffn_w1, ffn_w3, ffn_w2, pool_w, pool_ls, gqa_wq, gqa_wk, gqa_wv, gqa_wo, gqa_qn, gqa_kn, nat_wq, nat_wk, nat_wv, nat_wo, nat_rpb, final_g):
    c8 =jnp.concatenate([c, c_ctx[None, :], jnp.zeros((8 - BATCH - 1, D_MODEL), F32)], axis=0)
    mods = _ada_all(c8, ada_w, ada_b)[:, :BATCH + 1].reshape(DEPTH, BATCH + 1, 6, D_MODEL)
    pool_w_all = pool_w.astype(BF16)
    ffn_f32 = (ffn_w1, ffn_w3, ffn_w2)
    ffn_bf16 = {0: (tuple(w[0:1].astype(BF16) for w in ffn_f32), 0)}
    gqa_cast_layers, nat_cast_layers = (1, 2), (3, 1)

    xa = None
    for i in range(DEPTH):
        kind, j = i % N_MIXERS, i // N_MIXERS
        last = i == DEPTH - 1
        n_rows = ROWS_LAT if last else ROWS_ALL
        g1, g2 = norm_g[i, 0], norm_g[i, 1]
        if kind == 0:
            x_lat, x_ctx = (x.reshape(ROWS_LAT, D_MODEL), ctx.reshape(ROWS_CTX, D_MODEL)) if i == 0 else (xa, xa)
            xa, h2 = _pool_layer(x_lat, x_ctx, mods[i], g1, g2, pool_w_all, j, pool_ls[j], n_rows)
        else:
            if kind == 1:
                w_qkv = jnp.concatenate([gqa_wq[j], gqa_wk[j], gqa_wv[j]], axis=1).astype(BF16)
                cos_ext, sin_ext = _rope_tables(256)
                q, k, v = _gqa_proj(xa, mods[i], g1, w_qkv, gqa_qn[j], gqa_kn[j], cos_ext, sin_ext)
                first, count = gqa_cast_layers
                o_lat, o_ctx, *cast = _gqa_attn(q, k, v, ffn_f32, first, count)
                wo = gqa_wo[j]
            else:
                w_qkv = jnp.stack([nat_wq[j], nat_wk[j], nat_wv[j]], axis=0).astype(BF16)
                qkv = _nat_proj(xa, mods[i], g1, w_qkv)
                first, count = nat_cast_layers
                o_lat, o_ctx, *cast = _nat_attn(qkv, _nat_rpb_pairs(nat_rpb[j]), ffn_f32, first, count)
                wo = nat_wo[j]
            for n in range(count):
                ffn_bf16[first + n] = (tuple(cast), n)
            xa, h2 = _attn_out_layer(o_lat, o_ctx, xa, mods[i], wo.astype(BF16), g2)
        (w1_all, w3_all, w2_all), idx = ffn_bf16[i]
        xa = _ffn_layer(xa, h2, mods[i], w1_all, w3_all, w2_all, idx, final_g, n_rows, last)
    return xa.reshape(BATCH, SEQ, D_MODEL)
```

```python
import functools

import jax
import jax.numpy as jnp
import numpy as np
from jax import lax
from jax.experimental import pallas as pl
from jax.experimental.pallas import tpu as pltpu

D_MODEL = 2048
BATCH = 2
SEQ = 4096
DEPTH = 4
GRID_W = 64
CTX_LEN = 256
N_MIXERS = 3
POOL_GROUPS = 4
POOL_WINDOWS = (2, 4, 8, 16)
POOL_GC = D_MODEL // POOL_GROUPS
HEAD_DIM = 128
GQA_HEADS = D_MODEL // HEAD_DIM
GQA_KV_HEADS = 4
GQA_GROUP = GQA_HEADS // GQA_KV_HEADS
ROPE_THETA = 10000.0
ROPE_AXIS_DIM = HEAD_DIM // 2
NAT_HEADS = D_MODEL // HEAD_DIM
NAT_KH = 8
NAT_KW = 16
FFN_HIDDEN = -(-8 * D_MODEL // (3 * 256)) * 256
NORM_EPS = 1e-6

ROWS_LAT = BATCH * SEQ
ROWS_CTX = BATCH * CTX_LEN
ROWS_ALL = ROWS_LAT + ROWS_CTX
GRID_H = SEQ // GRID_W
ATTN_SCALE = HEAD_DIM ** -0.5
LOG2E = 1.4426950408889634
QK_PRESCALE = ATTN_SCALE * LOG2E
MASK_VALUE = -1e30
LANE = 128
POOL_HALO = 8
SOFTMAX_ROWS = 32
GQA_PROJ_ROWS = 256

VMEM_LIMIT_BYTES = 56 * 1024 * 1024

F32 = jnp.float32
BF16 = jnp.bfloat16


def _params(*sem):
    return pltpu.CompilerParams(dimension_semantics=sem, vmem_limit_bytes=VMEM_LIMIT_BYTES)


def _dot(a, b):
    return jnp.dot(a, b, preferred_element_type=F32)


def _dot_nt(a, b):
    return lax.dot_general(a, b, (((1,), (1,)), ((), ())), preferred_element_type=F32)


def _silu(a):
    return a * (1.0 / (1.0 + jnp.exp(-a)))


def _rms(x, g):
    return x * lax.rsqrt(jnp.mean(x * x, axis=-1, keepdims=True) + NORM_EPS) * g


def _norm_mod(x, g, shift, scale):
    return _rms(x, g) * (1.0 + scale) + shift


def _mod_row(t, tm):
    return jnp.where(t < ROWS_LAT // tm, t // (SEQ // tm), BATCH)


def _ada_kernel(c_ref, w_ref, b_ref, o_ref):
    s = _silu(c_ref[...])
    o_ref[...] = _dot(s.astype(BF16), w_ref[...].astype(BF16)) + b_ref[...]


def _ada_all(c8, ada_w, ada_b):
    tn = 1024
    return pl.pallas_call(
        _ada_kernel,
        grid=(DEPTH, 6 * D_MODEL // tn),
        in_specs=[
            pl.BlockSpec((8, D_MODEL), lambda l, n: (0, 0)),
            pl.BlockSpec((None, D_MODEL, tn), lambda l, n: (l, 0, n)),
            pl.BlockSpec((None, 1, tn), lambda l, n: (l, 0, n)),
        ],
        out_specs=pl.BlockSpec((None, 8, tn), lambda l, n: (l, 0, n)),
        out_shape=jax.ShapeDtypeStruct((DEPTH, 8, 6 * D_MODEL), F32),
        compiler_params=_params("parallel", "parallel"),
        name="ada",
    )(c8, ada_w, ada_b.reshape(DEPTH, 1, 6 * D_MODEL))


def _pool_bands(tm):
    t = np.arange(tm)[:, None]
    j = np.arange(tm + 2 * POOL_HALO)[None, :]
    pos = np.where(j < tm, j, np.where(j < tm + POOL_HALO, j - tm - POOL_HALO, j - POOL_HALO))
    bands = [(pos >= t - w // 2) & (pos < t + w - w // 2) for w in POOL_WINDOWS]
    return jnp.asarray(np.stack(bands).astype(np.float32), BF16)


def _split_bf16(v):
    hi = v.astype(BF16)
    return hi, (v - hi.astype(F32)).astype(BF16)


def _pool_kernel(xl_ref, xc_ref, xp_ref, xn_ref, mod_ref, g1_ref, g2_ref, band_ref, w_ref, ls_ref,
                 o_ref, h2_ref, hi_ref, lo_ref, *, tm):
    t = pl.program_id(0)
    shift, scale, gate = mod_ref[0:1, :], mod_ref[1:2, :], mod_ref[2:3, :]
    g = g1_ref[...]

    n_lat_tiles = ROWS_LAT // tm
    is_lat = t < n_lat_tiles
    tiles_in_seq = jnp.where(is_lat, SEQ // tm, CTX_LEN // tm)
    pos_tile = jnp.where(is_lat, t % (SEQ // tm), (t - n_lat_tiles) % (CTX_LEN // tm))
    first = pos_tile == 0
    last = pos_tile == tiles_in_seq - 1

    x = jnp.where(is_lat, xl_ref[...], xc_ref[...])
    h = _norm_mod(x, g, shift, scale)
    halo = jnp.concatenate([jnp.where(first, 0.0, _norm_mod(xp_ref[...], g, shift, scale)),
                            jnp.where(last, 0.0, _norm_mod(xn_ref[...], g, shift, scale))], axis=0)
    hi_ref[0:tm, :], lo_ref[0:tm, :] = _split_bf16(h)
    hi_ref[tm:, :], lo_ref[tm:, :] = _split_bf16(halo)

    pos = pos_tile * tm + lax.broadcasted_iota(jnp.int32, (tm, 1), 0)
    n = tiles_in_seq * tm
    for k in range(POOL_GROUPS):
        w = POOL_WINDOWS[k]
        lanes = slice(k * POOL_GC, (k + 1) * POOL_GC)
        wsum = _dot(band_ref[k], hi_ref[:, lanes]) + _dot(band_ref[k], lo_ref[:, lanes])
        cnt = jnp.minimum(pos + (w - w // 2), n) - jnp.maximum(pos - w // 2, 0)
        pooled = wsum / cnt.astype(F32) - h[:, lanes]
        y = _dot(pooled.astype(BF16), w_ref[k]) * ls_ref[:, lanes]
        o_ref[:, lanes] = x[:, lanes] + gate[:, lanes] * y
    h2_ref[...] = _norm_mod(o_ref[...], g2_ref[...], mod_ref[3:4, :], mod_ref[4:5, :]).astype(BF16)


def _pool_layer(x_lat, x_ctx, mods_l, g1, g2, w_all, j, ls, n_rows):
    tm = 256
    n_lat = ROWS_LAT // tm
    nb8 = x_lat.shape[0] // POOL_HALO
    ctx_blk0 = (x_ctx.shape[0] - ROWS_CTX) // tm
    halo_blk = lambda b: jnp.clip(b, 0, nb8 - 1)
    row = lambda v: v.reshape(1, D_MODEL)
    return pl.pallas_call(
        functools.partial(_pool_kernel, tm=tm),
        grid=(n_rows // tm,),
        in_specs=[
            pl.BlockSpec((tm, D_MODEL), lambda t: (jnp.minimum(t, n_lat - 1), 0)),
            pl.BlockSpec((tm, D_MODEL), lambda t: (ctx_blk0 + jnp.maximum(t - n_lat, 0), 0)),
            pl.BlockSpec((POOL_HALO, D_MODEL), lambda t: (halo_blk(t * (tm // POOL_HALO) - 1), 0)),
            pl.BlockSpec((POOL_HALO, D_MODEL), lambda t: (halo_blk((t + 1) * (tm // POOL_HALO)), 0)),
            pl.BlockSpec((None, 6, D_MODEL), lambda t: (_mod_row(t, tm), 0, 0)),
            pl.BlockSpec((1, D_MODEL), lambda t: (0, 0)),
            pl.BlockSpec((1, D_MODEL), lambda t: (0, 0)),
            pl.BlockSpec((POOL_GROUPS, tm, tm + 2 * POOL_HALO), lambda t: (0, 0, 0)),
            pl.BlockSpec((None, POOL_GROUPS, POOL_GC, POOL_GC), lambda t: (j, 0, 0, 0)),
            pl.BlockSpec((1, D_MODEL), lambda t: (0, 0)),
        ],
        out_specs=[pl.BlockSpec((tm, D_MODEL), lambda t: (t, 0)), pl.BlockSpec((tm, D_MODEL), lambda t: (t, 0))],
        out_shape=[jax.ShapeDtypeStruct((n_rows, D_MODEL), F32), jax.ShapeDtypeStruct((n_rows, D_MODEL), BF16)],
        scratch_shapes=[pltpu.VMEM((tm + 2 * POOL_HALO, D_MODEL), BF16),
                        pltpu.VMEM((tm + 2 * POOL_HALO, D_MODEL), BF16)],
        compiler_params=_params("parallel"),
        name="pool_layer",
    )(x_lat, x_ctx, x_lat, x_lat, mods_l, row(g1), row(g2), _pool_bands(tm), w_all, row(ls))


def _ffn_kernel(x_ref, h_ref, mod_ref, w1_ref, w3_ref, w2_ref, fg_ref, *rest, nf, final, emit_next_h):
    if emit_next_h:
        gn_ref, modn_ref, o_ref, hn_ref, acc_ref = rest
    else:
        o_ref, acc_ref = rest
    f = pl.program_id(1)

    @pl.when(f == 0)
    def _():
        acc_ref[...] = jnp.zeros_like(acc_ref)

    h = h_ref[...]
    a = _dot(h, w1_ref[...])
    b = _dot(h, w3_ref[...])
    acc_ref[...] += _dot((_silu(a) * b).astype(BF16), w2_ref[...])

    @pl.when(f == nf - 1)
    def _():
        y = x_ref[...] + mod_ref[5:6, :] * acc_ref[...]
        if final:
            y = _rms(y, fg_ref[...])
        o_ref[...] = y
        if emit_next_h:
            hn_ref[...] = _norm_mod(y, gn_ref[...], modn_ref[0:1, :], modn_ref[1:2, :]).astype(BF16)


def _ffn_layer(x, h, mods_l, w1_all, w3_all, w2_all, layer, final_g, n_rows, final, next_norm=None):
    tm, tf = 512, 512
    nf = FFN_HIDDEN // tf
    emit_next_h = next_norm is not None
    tile = pl.BlockSpec((tm, D_MODEL), lambda t, f: (t, 0))
    mod_spec = pl.BlockSpec((None, 6, D_MODEL), lambda t, f: (_mod_row(t, tm), 0, 0))
    row_spec = pl.BlockSpec((1, D_MODEL), lambda t, f: (0, 0))
    in_specs = [
        tile, tile, mod_spec,
        pl.BlockSpec((None, D_MODEL, tf), lambda t, f: (layer, 0, f)),
        pl.BlockSpec((None, D_MODEL, tf), lambda t, f: (layer, 0, f)),
        pl.BlockSpec((None, tf, D_MODEL), lambda t, f: (layer, f, 0)),
        row_spec,
    ]
    args = [x, h, mods_l, w1_all, w3_all, w2_all, final_g.reshape(1, D_MODEL)]
    out_specs, out_shape = tile, jax.ShapeDtypeStruct((n_rows, D_MODEL), F32)
    if emit_next_h:
        in_specs += [row_spec, mod_spec]
        args += [next_norm[0].reshape(1, D_MODEL), next_norm[1]]
        out_specs, out_shape = [tile, tile], [out_shape, jax.ShapeDtypeStruct((n_rows, D_MODEL), BF16)]
    return pl.pallas_call(
        functools.partial(_ffn_kernel, nf=nf, final=final, emit_next_h=emit_next_h),
        grid=(n_rows // tm, nf),
        in_specs=in_specs,
        out_specs=out_specs,
        out_shape=out_shape,
        scratch_shapes=[pltpu.VMEM((tm, D_MODEL), F32)],
        compiler_params=_params("parallel", "arbitrary"),
        name="ffn_layer",
    )(*args)


def _attn_out_kernel(ol_ref, oc_ref, x_ref, mod_ref, wo_ref, g2_ref, o_ref, h2_ref, *, tm):
    t = pl.program_id(0)
    o = jnp.where(t < ROWS_LAT // tm, ol_ref[...], oc_ref[...])
    y = x_ref[...] + mod_ref[2:3, :] * _dot(o, wo_ref[...])
    o_ref[...] = y
    h2_ref[...] = _norm_mod(y, g2_ref[...], mod_ref[3:4, :], mod_ref[4:5, :]).astype(BF16)


def _attn_out_layer(o_lat, o_ctx, x, mods_l, wo, g2):
    tm = 512
    n_lat = ROWS_LAT // tm
    return pl.pallas_call(
        functools.partial(_attn_out_kernel, tm=tm),
        grid=(ROWS_ALL // tm,),
        in_specs=[
            pl.BlockSpec((tm, D_MODEL), lambda t: (jnp.minimum(t, n_lat - 1), 0)),
            pl.BlockSpec((tm, D_MODEL), lambda t: (jnp.maximum(t - n_lat, 0), 0)),
            pl.BlockSpec((tm, D_MODEL), lambda t: (t, 0)),
            pl.BlockSpec((None, 6, D_MODEL), lambda t: (_mod_row(t, tm), 0, 0)),
            pl.BlockSpec((D_MODEL, D_MODEL), lambda t: (0, 0)),
            pl.BlockSpec((1, D_MODEL), lambda t: (0, 0)),
        ],
        out_specs=[pl.BlockSpec((tm, D_MODEL), lambda t: (t, 0)), pl.BlockSpec((tm, D_MODEL), lambda t: (t, 0))],
        out_shape=[jax.ShapeDtypeStruct((ROWS_ALL, D_MODEL), F32), jax.ShapeDtypeStruct((ROWS_ALL, D_MODEL), BF16)],
        compiler_params=_params("parallel"),
        name="attn_out",
    )(o_lat, o_ctx, x, mods_l, wo, g2.reshape(1, D_MODEL))


def _rope(xh, cosf, sinf, even):
    swapped = jnp.where(even, pltpu.roll(xh, HEAD_DIM - 1, 1), pltpu.roll(xh, 1, 1))
    return xh * cosf + swapped * sinf


def _gqa_proj_kernel(x_ref, mod_ref, g_ref, w_ref, qn_ref, kn_ref, cos_ref, sin_ref, q_ref, k_ref, v_ref, *, tm):
    h = _norm_mod(x_ref[...], g_ref[...], mod_ref[0:1, :], mod_ref[1:2, :]).astype(BF16)
    qkv = _dot(h, w_ref[...])
    cosf, sinf = cos_ref[...], sin_ref[...]
    even = lax.broadcasted_iota(jnp.int32, (tm, HEAD_DIM), 1) % 2 == 0
    for hd in range(GQA_HEADS):
        lanes = slice(hd * HEAD_DIM, (hd + 1) * HEAD_DIM)
        qh = _rope(_rms(qkv[:, lanes], qn_ref[...]), cosf, sinf, even)
        q_ref[:, lanes] = (qh * QK_PRESCALE).astype(BF16)
    k0 = GQA_HEADS * HEAD_DIM
    for hd in range(GQA_KV_HEADS):
        lanes = slice(hd * HEAD_DIM, (hd + 1) * HEAD_DIM)
        kh = qkv[:, k0 + hd * HEAD_DIM:k0 + (hd + 1) * HEAD_DIM]
        k_ref[:, lanes] = _rope(_rms(kh, kn_ref[...]), cosf, sinf, even).astype(BF16)
    v_ref[...] = qkv[:, k0 + GQA_KV_HEADS * HEAD_DIM:].astype(BF16)


def _gqa_proj(x, mods_l, g, w_qkv, qn, kn, cos_ext, sin_ext):
    tm = GQA_PROJ_ROWS
    dq, dkv = GQA_HEADS * HEAD_DIM, GQA_KV_HEADS * HEAD_DIM
    n_lat = ROWS_LAT // tm
    rope_blk = lambda t: (jnp.where(t < n_lat, t % (SEQ // tm), SEQ // tm), 0)
    return pl.pallas_call(
        functools.partial(_gqa_proj_kernel, tm=tm),
        grid=(ROWS_ALL // tm,),
        in_specs=[
            pl.BlockSpec((tm, D_MODEL), lambda t: (t, 0)),
            pl.BlockSpec((None, 6, D_MODEL), lambda t: (_mod_row(t, tm), 0, 0)),
            pl.BlockSpec((1, D_MODEL), lambda t: (0, 0)),
            pl.BlockSpec((D_MODEL, dq + 2 * dkv), lambda t: (0, 0)),
            pl.BlockSpec((1, HEAD_DIM), lambda t: (0, 0)),
            pl.BlockSpec((1, HEAD_DIM), lambda t: (0, 0)),
            pl.BlockSpec((tm, HEAD_DIM), rope_blk),
            pl.BlockSpec((tm, HEAD_DIM), rope_blk),
        ],
        out_specs=[
            pl.BlockSpec((tm, dq), lambda t: (t, 0)),
            pl.BlockSpec((tm, dkv), lambda t: (t, 0)),
            pl.BlockSpec((tm, dkv), lambda t: (t, 0)),
        ],
        out_shape=[
            jax.ShapeDtypeStruct((ROWS_ALL, dq), BF16),
            jax.ShapeDtypeStruct((ROWS_ALL, dkv), BF16),
            jax.ShapeDtypeStruct((ROWS_ALL, dkv), BF16),
        ],
        compiler_params=_params("parallel"),
        name="gqa_proj",
    )(x, mods_l, g.reshape(1, D_MODEL), w_qkv, qn.reshape(1, HEAD_DIM), kn.reshape(1, HEAD_DIM), cos_ext, sin_ext)


def _softmax_rows(s):
    m = jnp.max(s, axis=-1, keepdims=True)
    p = jnp.exp2(s - m)
    return p, jnp.sum(p, axis=-1, keepdims=True)


def _stack_heads(ref, n):
    return jnp.concatenate([ref[:, g * HEAD_DIM:(g + 1) * HEAD_DIM] for g in range(n)], axis=0)


def _online_softmax_chunk(s_ref, p_ref, m_ref, a_ref, n_rows, width):
    nslab = width // LANE
    blocks = [slice(i * SOFTMAX_ROWS, (i + 1) * SOFTMAX_ROWS) for i in range(n_rows // SOFTMAX_ROWS)]
    for rows in blocks:
        mc = functools.reduce(jnp.maximum, [s_ref[rows, c * LANE:(c + 1) * LANE] for c in range(nslab)])
        m_prev = m_ref[rows, :]
        m_new = jnp.maximum(m_prev, jnp.max(mc, axis=-1, keepdims=True))
        a_ref[rows, :] = jnp.exp2(m_prev - m_new)
        m_ref[rows, :] = m_new
    for rows in blocks:
        m_new = m_ref[rows, :]
        for c in range(nslab):
            p_ref[rows, c * LANE:(c + 1) * LANE] = jnp.exp2(s_ref[rows, c * LANE:(c + 1) * LANE] - m_new).astype(BF16)


FFN_WEIGHT_SHAPES = ((D_MODEL, FFN_HIDDEN), (D_MODEL, FFN_HIDDEN), (FFN_HIDDEN, D_MODEL))


def _weight_cast_specs(first_layer, n_layers, n_steps, step_id):
    per_layer = n_steps // n_layers
    in_specs, out_specs, out_shapes = [], [], []
    for rows, cols in FFN_WEIGHT_SHAPES:
        blk = (None, rows // per_layer, cols)
        in_specs.append(pl.BlockSpec(
            blk, lambda *g: (first_layer + step_id(*g) // per_layer, step_id(*g) % per_layer, 0)))
        out_specs.append(pl.BlockSpec(blk, lambda *g: (step_id(*g) // per_layer, step_id(*g) % per_layer, 0)))
        out_shapes.append(jax.ShapeDtypeStruct((n_layers, rows, cols), BF16))
    return in_specs, out_specs, out_shapes


def _cast_weight_slabs(src_refs, dst_refs):
    for src, dst in zip(src_refs, dst_refs):
        dst[...] = src[...].astype(BF16)


def _gqa_attn_kernel(q_ref, qc_ref, k_ref, v_ref, kc_ref, vc_ref, w1f_ref, w3f_ref, w2f_ref,
                     ol_ref, oc_ref, w1b_ref, w3b_ref, w2b_ref,
                     s_ref, p_ref, m_ref, a_ref, acc_ref, vaug_ref, *, tq, tk, n_sub):
    _cast_weight_slabs((w1f_ref, w3f_ref, w2f_ref), (w1b_ref, w3b_ref, w2b_ref))
    kc, vc = kc_ref[...], vc_ref[...]
    sub_q = tq // n_sub
    n_rows = GQA_GROUP * sub_q

    @pl.when(pl.program_id(2) == 0)
    def _():
        p, l = _softmax_rows(_dot_nt(_stack_heads(qc_ref, GQA_GROUP), kc))
        oc = _dot(p.astype(BF16), vc) / l
        for g in range(GQA_GROUP):
            oc_ref[:, g * HEAD_DIM:(g + 1) * HEAD_DIM] = oc[g * CTX_LEN:(g + 1) * CTX_LEN].astype(BF16)
        vaug_ref[0:SEQ, 0:HEAD_DIM] = v_ref[...]
        vaug_ref[SEQ:, 0:HEAD_DIM] = vc
        vaug_ref[:, HEAD_DIM:] = jnp.ones((SEQ + CTX_LEN, HEAD_DIM), BF16)

    qs = [_stack_heads(q_ref.at[u * sub_q:(u + 1) * sub_q, :], GQA_GROUP) for u in range(n_sub)]
    m_ref[...] = jnp.full_like(m_ref, MASK_VALUE)
    acc_ref[...] = jnp.zeros_like(acc_ref)

    chunks = [(k_ref, j * tk, j * tk, tk) for j in range(SEQ // tk)] + [(kc_ref, 0, SEQ, CTX_LEN)]
    for n, (kr, off, voff, width) in enumerate(chunks):
        for u in range(n_sub):
            sb, pb = s_ref.at[u, n % 2], p_ref.at[u, n % 2]
            sb[:, :width] = _dot_nt(qs[u], kr[off:off + width, :])
            _online_softmax_chunk(sb, pb, m_ref.at[u], a_ref.at[u], n_rows, width)
            pv = _dot(pb[:, :width], vaug_ref[voff:voff + width, :])
            for half in (slice(0, HEAD_DIM), slice(HEAD_DIM, 2 * HEAD_DIM)):
                acc_ref[u, :, half] = a_ref[u] * acc_ref[u, :, half] + pv[:, half]

    for u in range(n_sub):
        o = acc_ref[u, :, 0:HEAD_DIM] / acc_ref[u, :, HEAD_DIM:]
        for g in range(GQA_GROUP):
            ol_ref[u * sub_q:(u + 1) * sub_q, g * HEAD_DIM:(g + 1) * HEAD_DIM] = (
                o[g * sub_q:(g + 1) * sub_q].astype(BF16))


def _gqa_attn(q, k, v, ffn_weights, first_layer, n_layers):
    tq, tk, n_sub = 512, 1024, 2
    sub_rows = GQA_GROUP * tq // n_sub
    gw = GQA_GROUP * HEAD_DIM
    nq = SEQ // tq
    ctx_blk = ROWS_LAT // CTX_LEN
    w_in, w_out, w_shapes = _weight_cast_specs(first_layer, n_layers, BATCH * GQA_KV_HEADS * nq,
                                               lambda b, h, i: (b * GQA_KV_HEADS + h) * nq + i)
    return pl.pallas_call(
        functools.partial(_gqa_attn_kernel, tq=tq, tk=tk, n_sub=n_sub),
        grid=(BATCH, GQA_KV_HEADS, nq),
        in_specs=[
            pl.BlockSpec((tq, gw), lambda b, h, i: (b * nq + i, h)),
            pl.BlockSpec((CTX_LEN, gw), lambda b, h, i: (ctx_blk + b, h)),
            pl.BlockSpec((SEQ, HEAD_DIM), lambda b, h, i: (b, h)),
            pl.BlockSpec((SEQ, HEAD_DIM), lambda b, h, i: (b, h)),
            pl.BlockSpec((CTX_LEN, HEAD_DIM), lambda b, h, i: (ctx_blk + b, h)),
            pl.BlockSpec((CTX_LEN, HEAD_DIM), lambda b, h, i: (ctx_blk + b, h)),
        ] + w_in,
        out_specs=[
            pl.BlockSpec((tq, gw), lambda b, h, i: (b * nq + i, h)),
            pl.BlockSpec((CTX_LEN, gw), lambda b, h, i: (b, h)),
        ] + w_out,
        out_shape=[
            jax.ShapeDtypeStruct((ROWS_LAT, D_MODEL), BF16),
            jax.ShapeDtypeStruct((ROWS_CTX, D_MODEL), BF16),
        ] + w_shapes,
        scratch_shapes=[
            pltpu.VMEM((n_sub, 2, sub_rows, tk), F32),
            pltpu.VMEM((n_sub, 2, sub_rows, tk), BF16),
            pltpu.VMEM((n_sub, sub_rows, LANE), F32),
            pltpu.VMEM((n_sub, sub_rows, LANE), F32),
            pltpu.VMEM((n_sub, sub_rows, 2 * HEAD_DIM), F32),
            pltpu.VMEM((SEQ + CTX_LEN, 2 * HEAD_DIM), BF16),
        ],
        compiler_params=_params("parallel", "parallel", "arbitrary"),
        name="gqa_attn",
    )(q, q, k, v, k, v, *ffn_weights)


def _nat_proj_kernel(h_ref, w_ref, o_ref):
    prescale = jnp.where(pl.program_id(0) == 0, QK_PRESCALE, 1.0)
    o_ref[...] = (_dot(h_ref[...], w_ref[...]) * prescale).astype(BF16)


def _nat_proj(h, w_qkv):
    tm = 512
    return pl.pallas_call(
        _nat_proj_kernel,
        grid=(3, ROWS_ALL // tm),
        in_specs=[
            pl.BlockSpec((tm, D_MODEL), lambda n, t: (t, 0)),
            pl.BlockSpec((None, D_MODEL, D_MODEL), lambda n, t: (n, 0, 0)),
        ],
        out_specs=pl.BlockSpec((None, tm, D_MODEL), lambda n, t: (n, t, 0)),
        out_shape=jax.ShapeDtypeStruct((3, ROWS_ALL, D_MODEL), BF16),
        compiler_params=_params("parallel", "parallel"),
        name="nat_proj",
    )(h, w_qkv)


NAT_BLOCK_ROWS = 8
NAT_WIN_ROWS = 2 * NAT_BLOCK_ROWS
NAT_Q = NAT_BLOCK_ROWS * GRID_W
NAT_K = NAT_WIN_ROWS * GRID_W
NAT_SLABS = NAT_K // LANE
NAT_PAIR_TILES = 2 * NAT_KH - 2
NAT_TILE_LEFT_MASKED = NAT_PAIR_TILES
NAT_TILE_RIGHT_MASKED = NAT_PAIR_TILES + 1
NAT_TILES = NAT_PAIR_TILES + 2
NAT_BUFFERS = 3


def _nat_row_plan(cls, i):
    half = NAT_KH // 2
    if cls == "first":
        p, delta = (i, 0) if i < half else (half, i - half)
    elif cls == "mid":
        p, delta = half, i
    else:
        p, delta = (half, half + i) if i < half else (i, NAT_KH)
    if delta % 2 == 0:
        slabs = [delta // 2 + t for t in range(half)]
        tiles = [2 * t - p + NAT_KH - 1 for t in range(half)]
    else:
        assert p == half
        slabs = [(delta - 1) // 2 + t for t in range(half + 1)]
        tiles = ([NAT_TILE_LEFT_MASKED] + [2 * t - 1 - p + NAT_KH - 1 for t in range(1, half)]
                 + [NAT_TILE_RIGHT_MASKED])
    return slabs, tiles


def _nat_build_tiles(rpbw_ref, t2_ref):
    lane = lax.broadcasted_iota(jnp.int32, (GRID_W, LANE), 1)
    cq = lax.broadcasted_iota(jnp.int32, (GRID_W, LANE), 0)
    kc = lane % GRID_W
    cstart = jnp.clip(cq - NAT_KW // 2, 0, GRID_W - NAT_KW)
    inside = (kc >= cstart) & (kc < cstart + NAT_KW)
    for d in range(NAT_PAIR_TILES):
        w = jnp.broadcast_to(rpbw_ref[d:d + 1, :] * LOG2E, (GRID_W, LANE))
        t = pltpu.roll(w, LANE - (NAT_KW - 1), 1, stride=1, stride_axis=0)
        t2_ref[d] = jnp.where(inside, t, MASK_VALUE)
    half = NAT_KH // 2
    t2_ref[NAT_TILE_LEFT_MASKED] = jnp.where(lane < GRID_W, MASK_VALUE, t2_ref[half - 2])
    t2_ref[NAT_TILE_RIGHT_MASKED] = jnp.where(lane < GRID_W, t2_ref[NAT_KH + half - 2], MASK_VALUE)


def _nat_softmax_block(cls, s_ref, sc_ref, t2_ref, p_ref, pc_ref, m_ref, l_ref):
    n_ctx_slabs = CTX_LEN // LANE
    subs = []
    for i in range(NAT_BLOCK_ROWS):
        slabs, tiles = _nat_row_plan(cls, i)
        for a in range(NAT_SLABS):
            if a not in slabs:
                p_ref[i * GRID_W:(i + 1) * GRID_W, a * LANE:(a + 1) * LANE] = jnp.zeros((GRID_W, LANE), BF16)
        for sub in range(GRID_W // SOFTMAX_ROWS):
            trow = slice(sub * SOFTMAX_ROWS, (sub + 1) * SOFTMAX_ROWS)
            rows = slice(i * GRID_W + sub * SOFTMAX_ROWS, i * GRID_W + (sub + 1) * SOFTMAX_ROWS)
            subs.append((rows, trow, slabs, tiles))

    def scores(rows, trow, slabs, tiles):
        sv = [s_ref[rows, a * LANE:(a + 1) * LANE] + t2_ref[t, trow, :] for a, t in zip(slabs, tiles)]
        return sv + [sc_ref[rows, c * LANE:(c + 1) * LANE] for c in range(n_ctx_slabs)]

    for rows, trow, slabs, tiles in subs:
        m = jnp.max(functools.reduce(jnp.maximum, scores(rows, trow, slabs, tiles)), axis=-1, keepdims=True)
        m_ref[rows, :] = jnp.broadcast_to(m, (SOFTMAX_ROWS, LANE))
    for rows, trow, slabs, tiles in subs:
        m = m_ref[rows, :]
        pv = [jnp.exp2(s - m) for s in scores(rows, trow, slabs, tiles)]
        l_ref[rows, :] = functools.reduce(jnp.add, pv)
        for a, pa in zip(slabs, pv):
            p_ref[rows, a * LANE:(a + 1) * LANE] = pa.astype(BF16)
        for c in range(n_ctx_slabs):
            pc_ref[rows, c * LANE:(c + 1) * LANE] = pv[len(slabs) + c].astype(BF16)


def _nat_attn_kernel(q_ref, k_ref, v_ref, qc_ref, kc_ref, vc_ref, rpbw_ref, w1f_ref, w3f_ref, w2f_ref,
                     ol_ref, oc_ref, w1b_ref, w3b_ref, w2b_ref,
                     t2_ref, s_ref, sc_ref, p_ref, pc_ref, m_ref, l_ref):
    _cast_weight_slabs((w1f_ref, w3f_ref, w2f_ref), (w1b_ref, w3b_ref, w2b_ref))
    kc, vc = kc_ref[...], vc_ref[...]
    p, l = _softmax_rows(_dot_nt(qc_ref[...], kc))
    oc_ref[...] = (_dot(p.astype(BF16), vc) / l).astype(BF16)

    _nat_build_tiles(rpbw_ref, t2_ref)

    n_blocks = GRID_H // NAT_BLOCK_ROWS
    half_win = (NAT_KH // 2) * GRID_W
    specs = [("first", 0, 0)]
    specs += [("mid", j * NAT_Q, j * NAT_Q - half_win) for j in range(1, n_blocks - 1)]
    specs += [("last", SEQ - NAT_Q, SEQ - NAT_K)]

    def scores(n):
        _, q_tok, k_tok = specs[n]
        qb = q_ref[q_tok:q_tok + NAT_Q, :]
        s_ref[n % NAT_BUFFERS] = _dot_nt(qb, k_ref[k_tok:k_tok + NAT_K, :])
        sc_ref[n % NAT_BUFFERS] = _dot_nt(qb, kc)

    def finish(n):
        cls, q_tok, k_tok = specs[n]
        b = n % NAT_BUFFERS
        _nat_softmax_block(cls, s_ref.at[b], sc_ref.at[b], t2_ref, p_ref.at[b], pc_ref.at[b], m_ref.at[b], l_ref.at[b])
        o = _dot(p_ref[b], v_ref[k_tok:k_tok + NAT_K, :]) + _dot(pc_ref[b], vc)
        ol_ref[q_tok:q_tok + NAT_Q, :] = (o / jnp.sum(l_ref[b], axis=-1, keepdims=True)).astype(BF16)

    scores(0)
    for n in range(n_blocks):
        if n + 1 < n_blocks:
            scores(n + 1)
        finish(n)


def _nat_attn(qkv, rpbw, ffn_weights, first_layer, n_layers):
    ctx_blk = ROWS_LAT // CTX_LEN
    lat = lambda n: pl.BlockSpec((None, SEQ, HEAD_DIM), lambda b, h: (n, b, h))
    ctx = lambda n: pl.BlockSpec((None, CTX_LEN, HEAD_DIM), lambda b, h: (n, ctx_blk + b, h))
    w_in, w_out, w_shapes = _weight_cast_specs(first_layer, n_layers, BATCH * NAT_HEADS,
                                               lambda b, h: b * NAT_HEADS + h)
    return pl.pallas_call(
        _nat_attn_kernel,
        grid=(BATCH, NAT_HEADS),
        in_specs=[lat(0), lat(1), lat(2), ctx(0), ctx(1), ctx(2),
                  pl.BlockSpec((None, 2 * NAT_KH, LANE), lambda b, h: (h, 0, 0))] + w_in,
        out_specs=[
            pl.BlockSpec((SEQ, HEAD_DIM), lambda b, h: (b, h)),
            pl.BlockSpec((CTX_LEN, HEAD_DIM), lambda b, h: (b, h)),
        ] + w_out,
        out_shape=[
            jax.ShapeDtypeStruct((ROWS_LAT, D_MODEL), BF16),
            jax.ShapeDtypeStruct((ROWS_CTX, D_MODEL), BF16),
        ] + w_shapes,
        scratch_shapes=[
            pltpu.VMEM((NAT_TILES, GRID_W, LANE), F32),
            pltpu.VMEM((NAT_BUFFERS, NAT_Q, NAT_K), F32),
            pltpu.VMEM((NAT_BUFFERS, NAT_Q, CTX_LEN), F32),
            pltpu.VMEM((NAT_BUFFERS, NAT_Q, NAT_K), BF16),
            pltpu.VMEM((NAT_BUFFERS, NAT_Q, CTX_LEN), BF16),
            pltpu.VMEM((NAT_BUFFERS, NAT_Q, LANE), F32),
            pltpu.VMEM((NAT_BUFFERS, NAT_Q, LANE), F32),
        ],
        compiler_params=_params("parallel", "parallel"),
        name="nat_attn",
    )(qkv, qkv, qkv, qkv, qkv, qkv, rpbw, *ffn_weights)


def _nat_rpb_pairs(rpb):
    padded = jnp.pad(rpb, ((0, 0), (0, 2), (0, GRID_W - (2 * NAT_KW - 1))))
    return jnp.concatenate([padded[:, :2 * NAT_KH], padded[:, 1:2 * NAT_KH + 1]], axis=-1)


def _rope_tables(tm):
    t = jnp.arange(SEQ)
    row = (t // GRID_W).astype(F32)
    col = (t % GRID_W).astype(F32)
    inv = ROPE_THETA ** (-jnp.arange(0, ROPE_AXIS_DIM, 2, dtype=F32) / ROPE_AXIS_DIM)
    ang = jnp.concatenate([row[:, None] * inv, col[:, None] * inv], axis=-1)
    cosf = jnp.repeat(jnp.cos(ang), 2, axis=-1)
    sign = jnp.tile(jnp.array([-1.0, 1.0], F32), HEAD_DIM // 2)
    sinf = jnp.repeat(jnp.sin(ang), 2, axis=-1) * sign
    cos_ext = jnp.concatenate([cosf, jnp.ones((tm, HEAD_DIM), F32)], axis=0)
    sin_ext = jnp.concatenate([sinf, jnp.zeros((tm, HEAD_DIM), F32)], axis=0)
    return cos_ext, sin_ext


def kernel(x, c, ctx, c_ctx, ada_w, ada_b, norm_g, ffn_w1, ffn_w3, ffn_w2, pool_w, pool_ls, gqa_wq, gqa_wk, gqa_wv, gqa_wo, gqa_qn, gqa_kn, nat_wq, nat_wk, nat_wv, nat_wo, nat_rpb, final_g):
    c8 =jnp.concatenate([c, c_ctx[None, :], jnp.zeros((8 - BATCH - 1, D_MODEL), F32)], axis=0)
    mods = _ada_all(c8, ada_w, ada_b)[:, :BATCH + 1].reshape(DEPTH, BATCH + 1, 6, D_MODEL)
    pool_w_all = pool_w.astype(BF16)
    ffn_f32 = (ffn_w1, ffn_w3, ffn_w2)
    ffn_bf16 = {0: (tuple(w[0:1].astype(BF16) for w in ffn_f32), 0)}
    gqa_cast_layers, nat_cast_layers = (1, 2), (3, 1)

    xa = None
    for i in range(DEPTH):
        kind, j = i % N_MIXERS, i // N_MIXERS
        last = i == DEPTH - 1
        n_rows = ROWS_LAT if last else ROWS_ALL
        g1, g2 = norm_g[i, 0], norm_g[i, 1]
        if kind == 0:
            x_lat, x_ctx = (x.reshape(ROWS_LAT, D_MODEL), ctx.reshape(ROWS_CTX, D_MODEL)) if i == 0 else (xa, xa)
            xa, h2 = _pool_layer(x_lat, x_ctx, mods[i], g1, g2, pool_w_all, j, pool_ls[j], n_rows)
        else:
            if kind == 1:
                w_qkv = jnp.concatenate([gqa_wq[j], gqa_wk[j], gqa_wv[j]], axis=1).astype(BF16)
                cos_ext, sin_ext = _rope_tables(GQA_PROJ_ROWS)
                q, k, v = _gqa_proj(xa, mods[i], g1, w_qkv, gqa_qn[j], gqa_kn[j], cos_ext, sin_ext)
                first, count = gqa_cast_layers
                o_lat, o_ctx, *cast = _gqa_attn(q, k, v, ffn_f32, first, count)
                wo = gqa_wo[j]
            else:
                w_qkv = jnp.stack([nat_wq[j], nat_wk[j], nat_wv[j]], axis=0).astype(BF16)
                qkv = _nat_proj(h1, w_qkv)
                first, count = nat_cast_layers
                o_lat, o_ctx, *cast = _nat_attn(qkv, _nat_rpb_pairs(nat_rpb[j]), ffn_f32, first, count)
                wo = nat_wo[j]
            for n in range(count):
                ffn_bf16[first + n] = (tuple(cast), n)
            xa, h2 = _attn_out_layer(o_lat, o_ctx, xa, mods[i], wo.astype(BF16), g2)
        (w1_all, w3_all, w2_all), idx = ffn_bf16[i]
        if not last and (i + 1) % N_MIXERS == 2:
            xa, h1 = _ffn_layer(xa, h2, mods[i], w1_all, w3_all, w2_all, idx, final_g, n_rows, last,
                                next_norm=(norm_g[i + 1, 0], mods[i + 1]))
        else:
            xa = _ffn_layer(xa, h2, mods[i], w1_all, w3_all, w2_all, idx, final_g, n_rows, last)
    return xa.reshape(BATCH, SEQ, D_MODEL)
```

```python
import functools

import jax
import jax.numpy as jnp
import numpy as np
from jax import lax
from jax.experimental import pallas as pl
from jax.experimental.pallas import tpu as pltpu

D_MODEL = 2048
BATCH = 2
SEQ = 4096
DEPTH = 4
GRID_W = 64
CTX_LEN = 256
N_MIXERS = 3
POOL_GROUPS = 4
POOL_WINDOWS = (2, 4, 8, 16)
POOL_GC = D_MODEL // POOL_GROUPS
HEAD_DIM = 128
GQA_HEADS = D_MODEL // HEAD_DIM
GQA_KV_HEADS = 4
GQA_GROUP = GQA_HEADS // GQA_KV_HEADS
ROPE_THETA = 10000.0
ROPE_AXIS_DIM = HEAD_DIM // 2
NAT_HEADS = D_MODEL // HEAD_DIM
NAT_KH = 8
NAT_KW = 16
FFN_HIDDEN = -(-8 * D_MODEL // (3 * 256)) * 256
NORM_EPS = 1e-6

ROWS_LAT = BATCH * SEQ
ROWS_CTX = BATCH * CTX_LEN
ROWS_ALL = ROWS_LAT + ROWS_CTX
GRID_H = SEQ // GRID_W
ATTN_SCALE = HEAD_DIM ** -0.5
LOG2E = 1.4426950408889634
QK_PRESCALE = ATTN_SCALE * LOG2E
MASK_VALUE = -1e30
LANE = 128
POOL_HALO = 8
SOFTMAX_ROWS = 32
GQA_PROJ_ROWS = 256

VMEM_LIMIT_BYTES = 56 * 1024 * 1024

F32 = jnp.float32
BF16 = jnp.bfloat16


def _params(*sem):
    return pltpu.CompilerParams(dimension_semantics=sem, vmem_limit_bytes=VMEM_LIMIT_BYTES)


def _dot(a, b):
    return jnp.dot(a, b, preferred_element_type=F32)


def _dot_nt(a, b):
    return lax.dot_general(a, b, (((1,), (1,)), ((), ())), preferred_element_type=F32)


def _silu(a):
    return a * (1.0 / (1.0 + jnp.exp(-a)))


def _rms(x, g):
    return x * lax.rsqrt(jnp.mean(x * x, axis=-1, keepdims=True) + NORM_EPS) * g


def _norm_mod(x, g, shift, scale):
    return x * lax.rsqrt(jnp.mean(x * x, axis=-1, keepdims=True) + NORM_EPS) * (g * (1.0 + scale)) + shift


def _mod_row(t, tm):
    return jnp.where(t < ROWS_LAT // tm, t // (SEQ // tm), BATCH)


def _ada_kernel(c_ref, w_ref, b_ref, o_ref):
    s = _silu(c_ref[...])
    o_ref[...] = _dot(s.astype(BF16), w_ref[...].astype(BF16)) + b_ref[...]


def _ada_all(c8, ada_w, ada_b):
    tn = 1024
    return pl.pallas_call(
        _ada_kernel,
        grid=(DEPTH, 6 * D_MODEL // tn),
        in_specs=[
            pl.BlockSpec((8, D_MODEL), lambda l, n: (0, 0)),
            pl.BlockSpec((None, D_MODEL, tn), lambda l, n: (l, 0, n)),
            pl.BlockSpec((None, 1, tn), lambda l, n: (l, 0, n)),
        ],
        out_specs=pl.BlockSpec((None, 8, tn), lambda l, n: (l, 0, n)),
        out_shape=jax.ShapeDtypeStruct((DEPTH, 8, 6 * D_MODEL), F32),
        compiler_params=_params("parallel", "parallel"),
        name="ada",
    )(c8, ada_w, ada_b.reshape(DEPTH, 1, 6 * D_MODEL))


def _pool_bands(tm):
    t = np.arange(tm)[:, None]
    j = np.arange(tm + 2 * POOL_HALO)[None, :]
    pos = np.where(j < tm, j, np.where(j < tm + POOL_HALO, j - tm - POOL_HALO, j - POOL_HALO))
    bands = [(pos >= t - w // 2) & (pos < t + w - w // 2) for w in POOL_WINDOWS]
    return jnp.asarray(np.stack(bands).astype(np.float32), BF16)


def _split_bf16(v):
    hi = v.astype(BF16)
    return hi, (v - hi.astype(F32)).astype(BF16)


def _pool_kernel(xl_ref, xc_ref, xp_ref, xn_ref, mod_ref, g1_ref, g2_ref, band_ref, w_ref, ls_ref,
                 o_ref, h2_ref, hi_ref, lo_ref, *, tm):
    t = pl.program_id(0)
    shift, scale, gate = mod_ref[0:1, :], mod_ref[1:2, :], mod_ref[2:3, :]
    g = g1_ref[...]

    n_lat_tiles = ROWS_LAT // tm
    is_lat = t < n_lat_tiles
    tiles_in_seq = jnp.where(is_lat, SEQ // tm, CTX_LEN // tm)
    pos_tile = jnp.where(is_lat, t % (SEQ // tm), (t - n_lat_tiles) % (CTX_LEN // tm))
    first = pos_tile == 0
    last = pos_tile == tiles_in_seq - 1

    x = jnp.where(is_lat, xl_ref[...], xc_ref[...])
    h = _norm_mod(x, g, shift, scale)
    halo = jnp.concatenate([jnp.where(first, 0.0, _norm_mod(xp_ref[...], g, shift, scale)),
                            jnp.where(last, 0.0, _norm_mod(xn_ref[...], g, shift, scale))], axis=0)
    hi_ref[0:tm, :], lo_ref[0:tm, :] = _split_bf16(h)
    hi_ref[tm:, :], lo_ref[tm:, :] = _split_bf16(halo)

    pos = pos_tile * tm + lax.broadcasted_iota(jnp.int32, (tm, 1), 0)
    n = tiles_in_seq * tm
    for k in range(POOL_GROUPS):
        w = POOL_WINDOWS[k]
        lanes = slice(k * POOL_GC, (k + 1) * POOL_GC)
        wsum = _dot(band_ref[k], hi_ref[:, lanes]) + _dot(band_ref[k], lo_ref[:, lanes])
        cnt = jnp.minimum(pos + (w - w // 2), n) - jnp.maximum(pos - w // 2, 0)
        pooled = wsum / cnt.astype(F32) - h[:, lanes]
        y = _dot(pooled.astype(BF16), w_ref[k]) * ls_ref[:, lanes]
        o_ref[:, lanes] = x[:, lanes] + gate[:, lanes] * y
    h2_ref[...] = _norm_mod(o_ref[...], g2_ref[...], mod_ref[3:4, :], mod_ref[4:5, :]).astype(BF16)


def _pool_layer(x_lat, x_ctx, mods_l, g1, g2, w_all, j, ls, n_rows):
    tm = 256
    n_lat = ROWS_LAT // tm
    nb8 = x_lat.shape[0] // POOL_HALO
    ctx_blk0 = (x_ctx.shape[0] - ROWS_CTX) // tm
    halo_blk = lambda b: jnp.clip(b, 0, nb8 - 1)
    row = lambda v: v.reshape(1, D_MODEL)
    return pl.pallas_call(
        functools.partial(_pool_kernel, tm=tm),
        grid=(n_rows // tm,),
        in_specs=[
            pl.BlockSpec((tm, D_MODEL), lambda t: (jnp.minimum(t, n_lat - 1), 0)),
            pl.BlockSpec((tm, D_MODEL), lambda t: (ctx_blk0 + jnp.maximum(t - n_lat, 0), 0)),
            pl.BlockSpec((POOL_HALO, D_MODEL), lambda t: (halo_blk(t * (tm // POOL_HALO) - 1), 0)),
            pl.BlockSpec((POOL_HALO, D_MODEL), lambda t: (halo_blk((t + 1) * (tm // POOL_HALO)), 0)),
            pl.BlockSpec((None, 6, D_MODEL), lambda t: (_mod_row(t, tm), 0, 0)),
            pl.BlockSpec((1, D_MODEL), lambda t: (0, 0)),
            pl.BlockSpec((1, D_MODEL), lambda t: (0, 0)),
            pl.BlockSpec((POOL_GROUPS, tm, tm + 2 * POOL_HALO), lambda t: (0, 0, 0)),
            pl.BlockSpec((None, POOL_GROUPS, POOL_GC, POOL_GC), lambda t: (j, 0, 0, 0)),
            pl.BlockSpec((1, D_MODEL), lambda t: (0, 0)),
        ],
        out_specs=[pl.BlockSpec((tm, D_MODEL), lambda t: (t, 0)), pl.BlockSpec((tm, D_MODEL), lambda t: (t, 0))],
        out_shape=[jax.ShapeDtypeStruct((n_rows, D_MODEL), F32), jax.ShapeDtypeStruct((n_rows, D_MODEL), BF16)],
        scratch_shapes=[pltpu.VMEM((tm + 2 * POOL_HALO, D_MODEL), BF16),
                        pltpu.VMEM((tm + 2 * POOL_HALO, D_MODEL), BF16)],
        compiler_params=_params("parallel"),
        name="pool_layer",
    )(x_lat, x_ctx, x_lat, x_lat, mods_l, row(g1), row(g2), _pool_bands(tm), w_all, row(ls))


def _ffn_kernel(x_ref, h_ref, mod_ref, w1_ref, w3_ref, w2_ref, fg_ref, *rest, nf, final, emit_next_h):
    if emit_next_h:
        gn_ref, modn_ref, o_ref, hn_ref, acc_ref = rest
    else:
        o_ref, acc_ref = rest
    f = pl.program_id(1)

    @pl.when(f == 0)
    def _():
        acc_ref[...] = jnp.zeros_like(acc_ref)

    h = h_ref[...]
    a = _dot(h, w1_ref[...])
    b = _dot(h, w3_ref[...])
    acc_ref[...] += _dot((_silu(a) * b).astype(BF16), w2_ref[...])

    @pl.when(f == nf - 1)
    def _():
        y = x_ref[...] + mod_ref[5:6, :] * acc_ref[...]
        if final:
            y = _rms(y, fg_ref[...])
        o_ref[...] = y
        if emit_next_h:
            hn_ref[...] = _norm_mod(y, gn_ref[...], modn_ref[0:1, :], modn_ref[1:2, :]).astype(BF16)


def _ffn_layer(x, h, mods_l, w1_all, w3_all, w2_all, layer, final_g, n_rows, final, next_norm=None):
    tm, tf = 512, 512
    nf = FFN_HIDDEN // tf
    emit_next_h = next_norm is not None
    tile = pl.BlockSpec((tm, D_MODEL), lambda t, f: (t, 0))
    mod_spec = pl.BlockSpec((None, 6, D_MODEL), lambda t, f: (_mod_row(t, tm), 0, 0))
    row_spec = pl.BlockSpec((1, D_MODEL), lambda t, f: (0, 0))
    in_specs = [
        tile, tile, mod_spec,
        pl.BlockSpec((None, D_MODEL, tf), lambda t, f: (layer, 0, f)),
        pl.BlockSpec((None, D_MODEL, tf), lambda t, f: (layer, 0, f)),
        pl.BlockSpec((None, tf, D_MODEL), lambda t, f: (layer, f, 0)),
        row_spec,
    ]
    args = [x, h, mods_l, w1_all, w3_all, w2_all, final_g.reshape(1, D_MODEL)]
    out_specs, out_shape = tile, jax.ShapeDtypeStruct((n_rows, D_MODEL), F32)
    if emit_next_h:
        in_specs += [row_spec, mod_spec]
        args += [next_norm[0].reshape(1, D_MODEL), next_norm[1]]
        out_specs, out_shape = [tile, tile], [out_shape, jax.ShapeDtypeStruct((n_rows, D_MODEL), BF16)]
    return pl.pallas_call(
        functools.partial(_ffn_kernel, nf=nf, final=final, emit_next_h=emit_next_h),
        grid=(n_rows // tm, nf),
        in_specs=in_specs,
        out_specs=out_specs,
        out_shape=out_shape,
        scratch_shapes=[pltpu.VMEM((tm, D_MODEL), F32)],
        compiler_params=_params("parallel", "arbitrary"),
        name="ffn_layer",
    )(*args)


def _attn_out_kernel(ol_ref, oc_ref, x_ref, mod_ref, wo_ref, g2_ref, o_ref, h2_ref, *, tm):
    t = pl.program_id(0)
    o = jnp.where(t < ROWS_LAT // tm, ol_ref[...], oc_ref[...])
    y = x_ref[...] + mod_ref[2:3, :] * _dot(o, wo_ref[...])
    o_ref[...] = y
    h2_ref[...] = _norm_mod(y, g2_ref[...], mod_ref[3:4, :], mod_ref[4:5, :]).astype(BF16)


def _attn_out_layer(o_lat, o_ctx, x, mods_l, wo, g2):
    tm = 512
    n_lat = ROWS_LAT // tm
    return pl.pallas_call(
        functools.partial(_attn_out_kernel, tm=tm),
        grid=(ROWS_ALL // tm,),
        in_specs=[
            pl.BlockSpec((tm, D_MODEL), lambda t: (jnp.minimum(t, n_lat - 1), 0)),
            pl.BlockSpec((tm, D_MODEL), lambda t: (jnp.maximum(t - n_lat, 0), 0)),
            pl.BlockSpec((tm, D_MODEL), lambda t: (t, 0)),
            pl.BlockSpec((None, 6, D_MODEL), lambda t: (_mod_row(t, tm), 0, 0)),
            pl.BlockSpec((D_MODEL, D_MODEL), lambda t: (0, 0)),
            pl.BlockSpec((1, D_MODEL), lambda t: (0, 0)),
        ],
        out_specs=[pl.BlockSpec((tm, D_MODEL), lambda t: (t, 0)), pl.BlockSpec((tm, D_MODEL), lambda t: (t, 0))],
        out_shape=[jax.ShapeDtypeStruct((ROWS_ALL, D_MODEL), F32), jax.ShapeDtypeStruct((ROWS_ALL, D_MODEL), BF16)],
        compiler_params=_params("parallel"),
        name="attn_out",
    )(o_lat, o_ctx, x, mods_l, wo, g2.reshape(1, D_MODEL))


def _rope(xh, cosf, sinf, even):
    swapped = jnp.where(even, pltpu.roll(xh, HEAD_DIM - 1, 1), pltpu.roll(xh, 1, 1))
    return xh * cosf + swapped * sinf


def _gqa_proj_kernel(x_ref, mod_ref, g_ref, w_ref, qn_ref, kn_ref, cos_ref, sin_ref, q_ref, k_ref, v_ref, *, tm):
    h = _norm_mod(x_ref[...], g_ref[...], mod_ref[0:1, :], mod_ref[1:2, :]).astype(BF16)
    qkv = _dot(h, w_ref[...])
    cosf, sinf = cos_ref[...], sin_ref[...]
    even = lax.broadcasted_iota(jnp.int32, (tm, HEAD_DIM), 1) % 2 == 0
    for hd in range(GQA_HEADS):
        lanes = slice(hd * HEAD_DIM, (hd + 1) * HEAD_DIM)
        qh = _rope(_rms(qkv[:, lanes], qn_ref[...]), cosf, sinf, even)
        q_ref[:, lanes] = (qh * QK_PRESCALE).astype(BF16)
    k0 = GQA_HEADS * HEAD_DIM
    for hd in range(GQA_KV_HEADS):
        lanes = slice(hd * HEAD_DIM, (hd + 1) * HEAD_DIM)
        kh = qkv[:, k0 + hd * HEAD_DIM:k0 + (hd + 1) * HEAD_DIM]
        k_ref[:, lanes] = _rope(_rms(kh, kn_ref[...]), cosf, sinf, even).astype(BF16)
    v_ref[...] = qkv[:, k0 + GQA_KV_HEADS * HEAD_DIM:].astype(BF16)


def _gqa_proj(x, mods_l, g, w_qkv, qn, kn, cos_ext, sin_ext):
    tm = GQA_PROJ_ROWS
    dq, dkv = GQA_HEADS * HEAD_DIM, GQA_KV_HEADS * HEAD_DIM
    n_lat = ROWS_LAT // tm
    rope_blk = lambda t: (jnp.where(t < n_lat, t % (SEQ // tm), SEQ // tm), 0)
    return pl.pallas_call(
        functools.partial(_gqa_proj_kernel, tm=tm),
        grid=(ROWS_ALL // tm,),
        in_specs=[
            pl.BlockSpec((tm, D_MODEL), lambda t: (t, 0)),
            pl.BlockSpec((None, 6, D_MODEL), lambda t: (_mod_row(t, tm), 0, 0)),
            pl.BlockSpec((1, D_MODEL), lambda t: (0, 0)),
            pl.BlockSpec((D_MODEL, dq + 2 * dkv), lambda t: (0, 0)),
            pl.BlockSpec((1, HEAD_DIM), lambda t: (0, 0)),
            pl.BlockSpec((1, HEAD_DIM), lambda t: (0, 0)),
            pl.BlockSpec((tm, HEAD_DIM), rope_blk),
            pl.BlockSpec((tm, HEAD_DIM), rope_blk),
        ],
        out_specs=[
            pl.BlockSpec((tm, dq), lambda t: (t, 0)),
            pl.BlockSpec((tm, dkv), lambda t: (t, 0)),
            pl.BlockSpec((tm, dkv), lambda t: (t, 0)),
        ],
        out_shape=[
            jax.ShapeDtypeStruct((ROWS_ALL, dq), BF16),
            jax.ShapeDtypeStruct((ROWS_ALL, dkv), BF16),
            jax.ShapeDtypeStruct((ROWS_ALL, dkv), BF16),
        ],
        compiler_params=_params("parallel"),
        name="gqa_proj",
    )(x, mods_l, g.reshape(1, D_MODEL), w_qkv, qn.reshape(1, HEAD_DIM), kn.reshape(1, HEAD_DIM), cos_ext, sin_ext)


def _softmax_rows(s):
    m = jnp.max(s, axis=-1, keepdims=True)
    p = jnp.exp2(s - m)
    return p, jnp.sum(p, axis=-1, keepdims=True)


def _stack_heads(ref, n):
    return jnp.concatenate([ref[:, g * HEAD_DIM:(g + 1) * HEAD_DIM] for g in range(n)], axis=0)


def _online_softmax_chunk(s_ref, p_ref, m_ref, a_ref, n_rows, width):
    nslab = width // LANE
    blocks = [slice(i * SOFTMAX_ROWS, (i + 1) * SOFTMAX_ROWS) for i in range(n_rows // SOFTMAX_ROWS)]
    for rows in blocks:
        mc = functools.reduce(jnp.maximum, [s_ref[rows, c * LANE:(c + 1) * LANE] for c in range(nslab)])
        m_prev = m_ref[rows, :]
        m_new = jnp.maximum(m_prev, jnp.max(mc, axis=-1, keepdims=True))
        a_ref[rows, :] = jnp.exp2(m_prev - m_new)
        m_ref[rows, :] = m_new
    for rows in blocks:
        m_new = m_ref[rows, :]
        for c in range(nslab):
            p_ref[rows, c * LANE:(c + 1) * LANE] = jnp.exp2(s_ref[rows, c * LANE:(c + 1) * LANE] - m_new).astype(BF16)


FFN_WEIGHT_SHAPES = ((D_MODEL, FFN_HIDDEN), (D_MODEL, FFN_HIDDEN), (FFN_HIDDEN, D_MODEL))


def _weight_cast_specs(first_layer, n_layers, n_steps, step_id):
    per_layer = n_steps // n_layers
    in_specs, out_specs, out_shapes = [], [], []
    for rows, cols in FFN_WEIGHT_SHAPES:
        blk = (None, rows // per_layer, cols)
        in_specs.append(pl.BlockSpec(
            blk, lambda *g: (first_layer + step_id(*g) // per_layer, step_id(*g) % per_layer, 0)))
        out_specs.append(pl.BlockSpec(blk, lambda *g: (step_id(*g) // per_layer, step_id(*g) % per_layer, 0)))
        out_shapes.append(jax.ShapeDtypeStruct((n_layers, rows, cols), BF16))
    return in_specs, out_specs, out_shapes


def _cast_weight_slabs(src_refs, dst_refs):
    for src, dst in zip(src_refs, dst_refs):
        dst[...] = src[...].astype(BF16)


def _gqa_attn_kernel(q_ref, qc_ref, k_ref, v_ref, kc_ref, vc_ref, w1f_ref, w3f_ref, w2f_ref,
                     ol_ref, oc_ref, w1b_ref, w3b_ref, w2b_ref,
                     s_ref, p_ref, m_ref, a_ref, acc_ref, vaug_ref, *, tq, tk, n_sub):
    _cast_weight_slabs((w1f_ref, w3f_ref, w2f_ref), (w1b_ref, w3b_ref, w2b_ref))
    kc, vc = kc_ref[...], vc_ref[...]
    sub_q = tq // n_sub
    n_rows = GQA_GROUP * sub_q

    @pl.when(pl.program_id(2) == 0)
    def _():
        p, l = _softmax_rows(_dot_nt(_stack_heads(qc_ref, GQA_GROUP), kc))
        oc = _dot(p.astype(BF16), vc) / l
        for g in range(GQA_GROUP):
            oc_ref[:, g * HEAD_DIM:(g + 1) * HEAD_DIM] = oc[g * CTX_LEN:(g + 1) * CTX_LEN].astype(BF16)
        vaug_ref[0:SEQ, 0:HEAD_DIM] = v_ref[...]
        vaug_ref[SEQ:, 0:HEAD_DIM] = vc
        vaug_ref[:, HEAD_DIM:] = jnp.ones((SEQ + CTX_LEN, HEAD_DIM), BF16)

    qs = [_stack_heads(q_ref.at[u * sub_q:(u + 1) * sub_q, :], GQA_GROUP) for u in range(n_sub)]
    m_ref[...] = jnp.full_like(m_ref, MASK_VALUE)
    acc_ref[...] = jnp.zeros_like(acc_ref)

    chunks = [(k_ref, j * tk, j * tk, tk) for j in range(SEQ // tk)] + [(kc_ref, 0, SEQ, CTX_LEN)]
    for n, (kr, off, voff, width) in enumerate(chunks):
        for u in range(n_sub):
            sb, pb = s_ref.at[u, n % 2], p_ref.at[u, n % 2]
            sb[:, :width] = _dot_nt(qs[u], kr[off:off + width, :])
            _online_softmax_chunk(sb, pb, m_ref.at[u], a_ref.at[u], n_rows, width)
            pv = _dot(pb[:, :width], vaug_ref[voff:voff + width, :])
            for half in (slice(0, HEAD_DIM), slice(HEAD_DIM, 2 * HEAD_DIM)):
                acc_ref[u, :, half] = a_ref[u] * acc_ref[u, :, half] + pv[:, half]

    for u in range(n_sub):
        o = acc_ref[u, :, 0:HEAD_DIM] / acc_ref[u, :, HEAD_DIM:]
        for g in range(GQA_GROUP):
            ol_ref[u * sub_q:(u + 1) * sub_q, g * HEAD_DIM:(g + 1) * HEAD_DIM] = (
                o[g * sub_q:(g + 1) * sub_q].astype(BF16))


def _gqa_attn(q, k, v, ffn_weights, first_layer, n_layers):
    tq, tk, n_sub = 512, 1024, 2
    sub_rows = GQA_GROUP * tq // n_sub
    gw = GQA_GROUP * HEAD_DIM
    nq = SEQ // tq
    ctx_blk = ROWS_LAT // CTX_LEN
    w_in, w_out, w_shapes = _weight_cast_specs(first_layer, n_layers, BATCH * GQA_KV_HEADS * nq,
                                               lambda b, h, i: (b * GQA_KV_HEADS + h) * nq + i)
    return pl.pallas_call(
        functools.partial(_gqa_attn_kernel, tq=tq, tk=tk, n_sub=n_sub),
        grid=(BATCH, GQA_KV_HEADS, nq),
        in_specs=[
            pl.BlockSpec((tq, gw), lambda b, h, i: (b * nq + i, h)),
            pl.BlockSpec((CTX_LEN, gw), lambda b, h, i: (ctx_blk + b, h)),
            pl.BlockSpec((SEQ, HEAD_DIM), lambda b, h, i: (b, h)),
            pl.BlockSpec((SEQ, HEAD_DIM), lambda b, h, i: (b, h)),
            pl.BlockSpec((CTX_LEN, HEAD_DIM), lambda b, h, i: (ctx_blk + b, h)),
            pl.BlockSpec((CTX_LEN, HEAD_DIM), lambda b, h, i: (ctx_blk + b, h)),
        ] + w_in,
        out_specs=[
            pl.BlockSpec((tq, gw), lambda b, h, i: (b * nq + i, h)),
            pl.BlockSpec((CTX_LEN, gw), lambda b, h, i: (b, h)),
        ] + w_out,
        out_shape=[
            jax.ShapeDtypeStruct((ROWS_LAT, D_MODEL), BF16),
            jax.ShapeDtypeStruct((ROWS_CTX, D_MODEL), BF16),
        ] + w_shapes,
        scratch_shapes=[
            pltpu.VMEM((n_sub, 2, sub_rows, tk), F32),
            pltpu.VMEM((n_sub, 2, sub_rows, tk), BF16),
            pltpu.VMEM((n_sub, sub_rows, LANE), F32),
            pltpu.VMEM((n_sub, sub_rows, LANE), F32),
            pltpu.VMEM((n_sub, sub_rows, 2 * HEAD_DIM), F32),
            pltpu.VMEM((SEQ + CTX_LEN, 2 * HEAD_DIM), BF16),
        ],
        compiler_params=_params("parallel", "parallel", "arbitrary"),
        name="gqa_attn",
    )(q, q, k, v, k, v, *ffn_weights)


def _nat_proj_kernel(h_ref, w_ref, o_ref):
    prescale = jnp.where(pl.program_id(0) == 0, QK_PRESCALE, 1.0)
    o_ref[...] = (_dot(h_ref[...], w_ref[...]) * prescale).astype(BF16)


def _nat_proj(h, w_qkv):
    tm = 512
    return pl.pallas_call(
        _nat_proj_kernel,
        grid=(3, ROWS_ALL // tm),
        in_specs=[
            pl.BlockSpec((tm, D_MODEL), lambda n, t: (t, 0)),
            pl.BlockSpec((None, D_MODEL, D_MODEL), lambda n, t: (n, 0, 0)),
        ],
        out_specs=pl.BlockSpec((None, tm, D_MODEL), lambda n, t: (n, t, 0)),
        out_shape=jax.ShapeDtypeStruct((3, ROWS_ALL, D_MODEL), BF16),
        compiler_params=_params("parallel", "parallel"),
        name="nat_proj",
    )(h, w_qkv)


NAT_BLOCK_ROWS = 8
NAT_WIN_ROWS = NAT_BLOCK_ROWS + NAT_KH
NAT_Q = NAT_BLOCK_ROWS * GRID_W
NAT_K = NAT_WIN_ROWS * GRID_W
NAT_SLABS = NAT_K // LANE
NAT_PAIR_TILES = 2 * NAT_KH - 2
NAT_TILE_LEFT_MASKED = NAT_PAIR_TILES
NAT_TILE_RIGHT_MASKED = NAT_PAIR_TILES + 1
NAT_TILES = NAT_PAIR_TILES + 2
NAT_BUFFERS = 3


def _nat_row_plan(cls, i):
    half = NAT_KH // 2
    if cls == "first":
        r, ws = i, 0
    elif cls == "mid":
        r, ws = NAT_BLOCK_ROWS + i, NAT_BLOCK_ROWS - half
    else:
        r, ws = GRID_H - NAT_BLOCK_ROWS + i, GRID_H - NAT_WIN_ROWS
    rstart = min(max(r - half, 0), GRID_H - NAT_KH)
    p, delta = r - rstart, rstart - ws
    assert 0 <= delta and delta + NAT_KH <= NAT_WIN_ROWS
    if delta % 2 == 0:
        slabs = [delta // 2 + t for t in range(half)]
        tiles = [2 * t - p + NAT_KH - 1 for t in range(half)]
    else:
        assert p == half
        slabs = [(delta - 1) // 2 + t for t in range(half + 1)]
        tiles = ([NAT_TILE_LEFT_MASKED] + [2 * t - 1 - p + NAT_KH - 1 for t in range(1, half)]
                 + [NAT_TILE_RIGHT_MASKED])
    return slabs, tiles


def _nat_build_tiles(rpbw_ref, t2_ref):
    lane = lax.broadcasted_iota(jnp.int32, (GRID_W, LANE), 1)
    cq = lax.broadcasted_iota(jnp.int32, (GRID_W, LANE), 0)
    kc = lane % GRID_W
    cstart = jnp.clip(cq - NAT_KW // 2, 0, GRID_W - NAT_KW)
    inside = (kc >= cstart) & (kc < cstart + NAT_KW)
    for d in range(NAT_PAIR_TILES):
        w = jnp.broadcast_to(rpbw_ref[d:d + 1, :] * LOG2E, (GRID_W, LANE))
        t = pltpu.roll(w, LANE - (NAT_KW - 1), 1, stride=1, stride_axis=0)
        t2_ref[d] = jnp.where(inside, t, MASK_VALUE)
    half = NAT_KH // 2
    t2_ref[NAT_TILE_LEFT_MASKED] = jnp.where(lane < GRID_W, MASK_VALUE, t2_ref[half - 2])
    t2_ref[NAT_TILE_RIGHT_MASKED] = jnp.where(lane < GRID_W, t2_ref[NAT_KH + half - 2], MASK_VALUE)


def _nat_softmax_block(cls, s_ref, sc_ref, t2_ref, p_ref, pc_ref, m_ref):
    n_ctx_slabs = CTX_LEN // LANE
    subs = []
    for i in range(NAT_BLOCK_ROWS):
        slabs, tiles = _nat_row_plan(cls, i)
        for a in range(NAT_SLABS):
            if a not in slabs:
                p_ref[i * GRID_W:(i + 1) * GRID_W, a * LANE:(a + 1) * LANE] = jnp.zeros((GRID_W, LANE), BF16)
        for sub in range(GRID_W // SOFTMAX_ROWS):
            trow = slice(sub * SOFTMAX_ROWS, (sub + 1) * SOFTMAX_ROWS)
            rows = slice(i * GRID_W + sub * SOFTMAX_ROWS, i * GRID_W + (sub + 1) * SOFTMAX_ROWS)
            subs.append((rows, trow, slabs, tiles))

    def scores(rows, trow, slabs, tiles):
        sv = [s_ref[rows, a * LANE:(a + 1) * LANE] + t2_ref[t, trow, :] for a, t in zip(slabs, tiles)]
        return sv + [sc_ref[rows, c * LANE:(c + 1) * LANE] for c in range(n_ctx_slabs)]

    for rows, trow, slabs, tiles in subs:
        m = jnp.max(functools.reduce(jnp.maximum, scores(rows, trow, slabs, tiles)), axis=-1, keepdims=True)
        m_ref[rows, :] = jnp.broadcast_to(m, (SOFTMAX_ROWS, LANE))
    for rows, trow, slabs, tiles in subs:
        m = m_ref[rows, :]
        pv = [jnp.exp2(s - m) for s in scores(rows, trow, slabs, tiles)]
        for a, pa in zip(slabs, pv):
            p_ref[rows, a * LANE:(a + 1) * LANE] = pa.astype(BF16)
        for c in range(n_ctx_slabs):
            pc_ref[rows, c * LANE:(c + 1) * LANE] = pv[len(slabs) + c].astype(BF16)


def _nat_attn_kernel(q_ref, k_ref, v_ref, qc_ref, kc_ref, vc_ref, rpbw_ref, w1f_ref, w3f_ref, w2f_ref,
                     ol_ref, oc_ref, w1b_ref, w3b_ref, w2b_ref,
                     t2_ref, s_ref, sc_ref, p_ref, pc_ref, m_ref, vaug_ref):
    _cast_weight_slabs((w1f_ref, w3f_ref, w2f_ref), (w1b_ref, w3b_ref, w2b_ref))
    kc, vc = kc_ref[...], vc_ref[...]
    p, l = _softmax_rows(_dot_nt(qc_ref[...], kc))
    oc_ref[...] = (_dot(p.astype(BF16), vc) / l).astype(BF16)

    _nat_build_tiles(rpbw_ref, t2_ref)
    vaug_ref[0:SEQ, 0:HEAD_DIM] = v_ref[...]
    vaug_ref[SEQ:, 0:HEAD_DIM] = vc
    vaug_ref[:, HEAD_DIM:] = jnp.ones((SEQ + CTX_LEN, HEAD_DIM), BF16)

    n_blocks = GRID_H // NAT_BLOCK_ROWS
    half_win = (NAT_KH // 2) * GRID_W
    specs = [("first", 0, 0)]
    specs += [("mid", j * NAT_Q, j * NAT_Q - half_win) for j in range(1, n_blocks - 1)]
    specs += [("last", SEQ - NAT_Q, SEQ - NAT_K)]

    def scores(n):
        _, q_tok, k_tok = specs[n]
        qb = q_ref[q_tok:q_tok + NAT_Q, :]
        s_ref[n % NAT_BUFFERS] = _dot_nt(qb, k_ref[k_tok:k_tok + NAT_K, :])
        sc_ref[n % NAT_BUFFERS] = _dot_nt(qb, kc)

    def finish(n):
        cls, q_tok, k_tok = specs[n]
        b = n % NAT_BUFFERS
        _nat_softmax_block(cls, s_ref.at[b], sc_ref.at[b], t2_ref, p_ref.at[b], pc_ref.at[b], m_ref.at[b])
        o = _dot(p_ref[b], vaug_ref[k_tok:k_tok + NAT_K, :]) + _dot(pc_ref[b], vaug_ref[SEQ:, :])
        ol_ref[q_tok:q_tok + NAT_Q, :] = (o[:, 0:HEAD_DIM] / o[:, HEAD_DIM:]).astype(BF16)

    scores(0)
    for n in range(n_blocks):
        if n + 1 < n_blocks:
            scores(n + 1)
        finish(n)


def _nat_attn(qkv, rpbw, ffn_weights, first_layer, n_layers):
    ctx_blk = ROWS_LAT // CTX_LEN
    lat = lambda n: pl.BlockSpec((None, SEQ, HEAD_DIM), lambda b, h: (n, b, h))
    ctx = lambda n: pl.BlockSpec((None, CTX_LEN, HEAD_DIM), lambda b, h: (n, ctx_blk + b, h))
    w_in, w_out, w_shapes = _weight_cast_specs(first_layer, n_layers, BATCH * NAT_HEADS,
                                               lambda b, h: b * NAT_HEADS + h)
    return pl.pallas_call(
        _nat_attn_kernel,
        grid=(BATCH, NAT_HEADS),
        in_specs=[lat(0), lat(1), lat(2), ctx(0), ctx(1), ctx(2),
                  pl.BlockSpec((None, 2 * NAT_KH, LANE), lambda b, h: (h, 0, 0))] + w_in,
        out_specs=[
            pl.BlockSpec((SEQ, HEAD_DIM), lambda b, h: (b, h)),
            pl.BlockSpec((CTX_LEN, HEAD_DIM), lambda b, h: (b, h)),
        ] + w_out,
        out_shape=[
            jax.ShapeDtypeStruct((ROWS_LAT, D_MODEL), BF16),
            jax.ShapeDtypeStruct((ROWS_CTX, D_MODEL), BF16),
        ] + w_shapes,
        scratch_shapes=[
            pltpu.VMEM((NAT_TILES, GRID_W, LANE), F32),
            pltpu.VMEM((NAT_BUFFERS, NAT_Q, NAT_K), F32),
            pltpu.VMEM((NAT_BUFFERS, NAT_Q, CTX_LEN), F32),
            pltpu.VMEM((NAT_BUFFERS, NAT_Q, NAT_K), BF16),
            pltpu.VMEM((NAT_BUFFERS, NAT_Q, CTX_LEN), BF16),
            pltpu.VMEM((NAT_BUFFERS, NAT_Q, LANE), F32),
            pltpu.VMEM((SEQ + CTX_LEN, 2 * HEAD_DIM), BF16),
        ],
        compiler_params=_params("parallel", "parallel"),
        name="nat_attn",
    )(qkv, qkv, qkv, qkv, qkv, qkv, rpbw, *ffn_weights)


def _nat_rpb_pairs(rpb):
    padded = jnp.pad(rpb, ((0, 0), (0, 2), (0, GRID_W - (2 * NAT_KW - 1))))
    return jnp.concatenate([padded[:, :2 * NAT_KH], padded[:, 1:2 * NAT_KH + 1]], axis=-1)


def _rope_tables(tm):
    t = jnp.arange(SEQ)
    row = (t // GRID_W).astype(F32)
    col = (t % GRID_W).astype(F32)
    inv = ROPE_THETA ** (-jnp.arange(0, ROPE_AXIS_DIM, 2, dtype=F32) / ROPE_AXIS_DIM)
    ang = jnp.concatenate([row[:, None] * inv, col[:, None] * inv], axis=-1)
    cosf = jnp.repeat(jnp.cos(ang), 2, axis=-1)
    sign = jnp.tile(jnp.array([-1.0, 1.0], F32), HEAD_DIM // 2)
    sinf = jnp.repeat(jnp.sin(ang), 2, axis=-1) * sign
    cos_ext = jnp.concatenate([cosf, jnp.ones((tm, HEAD_DIM), F32)], axis=0)
    sin_ext = jnp.concatenate([sinf, jnp.zeros((tm, HEAD_DIM), F32)], axis=0)
    return cos_ext, sin_ext


def kernel(x, c, ctx, c_ctx, ada_w, ada_b, norm_g, ffn_w1, ffn_w3, ffn_w2, pool_w, pool_ls, gqa_wq, gqa_wk, gqa_wv, gqa_wo, gqa_qn, gqa_kn, nat_wq, nat_wk, nat_wv, nat_wo, nat_rpb, final_g):
    c8 =jnp.concatenate([c, c_ctx[None, :], jnp.zeros((8 - BATCH - 1, D_MODEL), F32)], axis=0)
    mods = _ada_all(c8, ada_w, ada_b)[:, :BATCH + 1].reshape(DEPTH, BATCH + 1, 6, D_MODEL)
    pool_w_all = pool_w.astype(BF16)
    ffn_f32 = (ffn_w1, ffn_w3, ffn_w2)
    ffn_bf16 = {0: (tuple(w[0:1].astype(BF16) for w in ffn_f32), 0)}
    gqa_cast_layers, nat_cast_layers = (1, 2), (3, 1)

    xa = None
    for i in range(DEPTH):
        kind, j = i % N_MIXERS, i // N_MIXERS
        last = i == DEPTH - 1
        n_rows = ROWS_LAT if last else ROWS_ALL
        g1, g2 = norm_g[i, 0], norm_g[i, 1]
        if kind == 0:
            x_lat, x_ctx = (x.reshape(ROWS_LAT, D_MODEL), ctx.reshape(ROWS_CTX, D_MODEL)) if i == 0 else (xa, xa)
            xa, h2 = _pool_layer(x_lat, x_ctx, mods[i], g1, g2, pool_w_all, j, pool_ls[j], n_rows)
        else:
            if kind == 1:
                w_qkv = jnp.concatenate([gqa_wq[j], gqa_wk[j], gqa_wv[j]], axis=1).astype(BF16)
                cos_ext, sin_ext = _rope_tables(GQA_PROJ_ROWS)
                q, k, v = _gqa_proj(xa, mods[i], g1, w_qkv, gqa_qn[j], gqa_kn[j], cos_ext, sin_ext)
                first, count = gqa_cast_layers
                o_lat, o_ctx, *cast = _gqa_attn(q, k, v, ffn_f32, first, count)
                wo = gqa_wo[j]
            else:
                w_qkv = jnp.stack([nat_wq[j], nat_wk[j], nat_wv[j]], axis=0).astype(BF16)
                qkv = _nat_proj(h1, w_qkv)
                first, count = nat_cast_layers
                o_lat, o_ctx, *cast = _nat_attn(qkv, _nat_rpb_pairs(nat_rpb[j]), ffn_f32, first, count)
                wo = nat_wo[j]
            for n in range(count):
                ffn_bf16[first + n] = (tuple(cast), n)
            xa, h2 = _attn_out_layer(o_lat, o_ctx, xa, mods[i], wo.astype(BF16), g2)
        (w1_all, w3_all, w2_all), idx = ffn_bf16[i]
        if not last and (i + 1) % N_MIXERS == 2:
            xa, h1 = _ffn_layer(xa, h2, mods[i], w1_all, w3_all, w2_all, idx, final_g, n_rows, last,
                                next_norm=(norm_g[i + 1, 0], mods[i + 1]))
        else:
            xa = _ffn_layer(xa, h2, mods[i], w1_all, w3_all, w2_all, idx, final_g, n_rows, last)
    return xa.reshape(BATCH, SEQ, D_MODEL)
```

```python
import functools

import jax
import jax.numpy as jnp
import numpy as np
from jax import lax
from jax.experimental import pallas as pl
from jax.experimental.pallas import tpu as pltpu

D_MODEL = 2048
BATCH = 2
SEQ = 4096
DEPTH = 4
GRID_W = 64
CTX_LEN = 256
N_MIXERS = 3
POOL_GROUPS = 4
POOL_WINDOWS = (2, 4, 8, 16)
POOL_GC = D_MODEL // POOL_GROUPS
HEAD_DIM = 128
GQA_HEADS = D_MODEL // HEAD_DIM
GQA_KV_HEADS = 4
GQA_GROUP = GQA_HEADS // GQA_KV_HEADS
ROPE_THETA = 10000.0
ROPE_AXIS_DIM = HEAD_DIM // 2
NAT_HEADS = D_MODEL // HEAD_DIM
NAT_KH = 8
NAT_KW = 16
FFN_HIDDEN = -(-8 * D_MODEL // (3 * 256)) * 256
NORM_EPS = 1e-6

ROWS_LAT = BATCH * SEQ
ROWS_CTX = BATCH * CTX_LEN
ROWS_ALL = ROWS_LAT + ROWS_CTX
GRID_H = SEQ // GRID_W
ATTN_SCALE = HEAD_DIM ** -0.5
LOG2E = 1.4426950408889634
QK_PRESCALE = ATTN_SCALE * LOG2E
MASK_VALUE = -1e30
LANE = 128
POOL_HALO = 8
SOFTMAX_ROWS = 32
GQA_PROJ_ROWS = 256

VMEM_LIMIT_BYTES = 56 * 1024 * 1024

F32 = jnp.float32
BF16 = jnp.bfloat16


def _params(*sem):
    return pltpu.CompilerParams(dimension_semantics=sem, vmem_limit_bytes=VMEM_LIMIT_BYTES)


def _dot(a, b):
    return jnp.dot(a, b, preferred_element_type=F32)


def _dot_nt(a, b):
    return lax.dot_general(a, b, (((1,), (1,)), ((), ())), preferred_element_type=F32)


def _silu(a):
    return a * (1.0 / (1.0 + jnp.exp(-a)))


def _rms(x, g):
    return x * lax.rsqrt(jnp.mean(x * x, axis=-1, keepdims=True) + NORM_EPS) * g


def _norm_mod(x, g, shift, scale):
    return x * lax.rsqrt(jnp.mean(x * x, axis=-1, keepdims=True) + NORM_EPS) * (g * (1.0 + scale)) + shift


def _mod_row(t, tm):
    return jnp.where(t < ROWS_LAT // tm, t // (SEQ // tm), BATCH)


def _ada_kernel(c_ref, w_ref, b_ref, o_ref):
    s = _silu(c_ref[...])
    o_ref[...] = _dot(s.astype(BF16), w_ref[...].astype(BF16)) + b_ref[...]


def _ada_all(c8, ada_w, ada_b):
    tn = 1024
    return pl.pallas_call(
        _ada_kernel,
        grid=(DEPTH, 6 * D_MODEL // tn),
        in_specs=[
            pl.BlockSpec((8, D_MODEL), lambda l, n: (0, 0)),
            pl.BlockSpec((None, D_MODEL, tn), lambda l, n: (l, 0, n)),
            pl.BlockSpec((None, 1, tn), lambda l, n: (l, 0, n)),
        ],
        out_specs=pl.BlockSpec((None, 8, tn), lambda l, n: (l, 0, n)),
        out_shape=jax.ShapeDtypeStruct((DEPTH, 8, 6 * D_MODEL), F32),
        compiler_params=_params("parallel", "parallel"),
        name="ada",
    )(c8, ada_w, ada_b.reshape(DEPTH, 1, 6 * D_MODEL))


def _pool_bands(tm):
    t = np.arange(tm)[:, None]
    j = np.arange(tm + 2 * POOL_HALO)[None, :]
    pos = np.where(j < tm, j, np.where(j < tm + POOL_HALO, j - tm - POOL_HALO, j - POOL_HALO))
    bands = [(pos >= t - w // 2) & (pos < t + w - w // 2) for w in POOL_WINDOWS]
    return jnp.asarray(np.stack(bands).astype(np.float32), BF16)


def _split_bf16(v):
    hi = v.astype(BF16)
    return hi, (v - hi.astype(F32)).astype(BF16)


def _pool_kernel(xl_ref, xc_ref, xp_ref, xn_ref, mod_ref, g1_ref, g2_ref, band_ref, w_ref, ls_ref,
                 o_ref, h2_ref, hi_ref, lo_ref, *, tm):
    t = pl.program_id(0)
    shift, scale, gate = mod_ref[0:1, :], mod_ref[1:2, :], mod_ref[2:3, :]
    g = g1_ref[...]

    n_lat_tiles = ROWS_LAT // tm
    is_lat = t < n_lat_tiles
    tiles_in_seq = jnp.where(is_lat, SEQ // tm, CTX_LEN // tm)
    pos_tile = jnp.where(is_lat, t % (SEQ // tm), (t - n_lat_tiles) % (CTX_LEN // tm))
    first = pos_tile == 0
    last = pos_tile == tiles_in_seq - 1

    x = jnp.where(is_lat, xl_ref[...], xc_ref[...])
    h = _norm_mod(x, g, shift, scale)
    halo = jnp.concatenate([jnp.where(first, 0.0, _norm_mod(xp_ref[...], g, shift, scale)),
                            jnp.where(last, 0.0, _norm_mod(xn_ref[...], g, shift, scale))], axis=0)
    hi_ref[0:tm, :], lo_ref[0:tm, :] = _split_bf16(h)
    hi_ref[tm:, :], lo_ref[tm:, :] = _split_bf16(halo)

    pos = pos_tile * tm + lax.broadcasted_iota(jnp.int32, (tm, 1), 0)
    n = tiles_in_seq * tm
    for k in range(POOL_GROUPS):
        w = POOL_WINDOWS[k]
        lanes = slice(k * POOL_GC, (k + 1) * POOL_GC)
        wsum = _dot(band_ref[k], hi_ref[:, lanes]) + _dot(band_ref[k], lo_ref[:, lanes])
        cnt = jnp.minimum(pos + (w - w // 2), n) - jnp.maximum(pos - w // 2, 0)
        pooled = wsum / cnt.astype(F32) - h[:, lanes]
        y = _dot(pooled.astype(BF16), w_ref[k]) * ls_ref[:, lanes]
        o_ref[:, lanes] = x[:, lanes] + gate[:, lanes] * y
    h2_ref[...] = _norm_mod(o_ref[...], g2_ref[...], mod_ref[3:4, :], mod_ref[4:5, :]).astype(BF16)


def _pool_layer(x_lat, x_ctx, mods_l, g1, g2, w_all, j, ls, n_rows):
    tm = 256
    n_lat = ROWS_LAT // tm
    nb8 = x_lat.shape[0] // POOL_HALO
    ctx_blk0 = (x_ctx.shape[0] - ROWS_CTX) // tm
    halo_blk = lambda b: jnp.clip(b, 0, nb8 - 1)
    row = lambda v: v.reshape(1, D_MODEL)
    return pl.pallas_call(
        functools.partial(_pool_kernel, tm=tm),
        grid=(n_rows // tm,),
        in_specs=[
            pl.BlockSpec((tm, D_MODEL), lambda t: (jnp.minimum(t, n_lat - 1), 0)),
            pl.BlockSpec((tm, D_MODEL), lambda t: (ctx_blk0 + jnp.maximum(t - n_lat, 0), 0)),
            pl.BlockSpec((POOL_HALO, D_MODEL), lambda t: (halo_blk(t * (tm // POOL_HALO) - 1), 0)),
            pl.BlockSpec((POOL_HALO, D_MODEL), lambda t: (halo_blk((t + 1) * (tm // POOL_HALO)), 0)),
            pl.BlockSpec((None, 6, D_MODEL), lambda t: (_mod_row(t, tm), 0, 0)),
            pl.BlockSpec((1, D_MODEL), lambda t: (0, 0)),
            pl.BlockSpec((1, D_MODEL), lambda t: (0, 0)),
            pl.BlockSpec((POOL_GROUPS, tm, tm + 2 * POOL_HALO), lambda t: (0, 0, 0)),
            pl.BlockSpec((None, POOL_GROUPS, POOL_GC, POOL_GC), lambda t: (j, 0, 0, 0)),
            pl.BlockSpec((1, D_MODEL), lambda t: (0, 0)),
        ],
        out_specs=[pl.BlockSpec((tm, D_MODEL), lambda t: (t, 0)), pl.BlockSpec((tm, D_MODEL), lambda t: (t, 0))],
        out_shape=[jax.ShapeDtypeStruct((n_rows, D_MODEL), F32), jax.ShapeDtypeStruct((n_rows, D_MODEL), BF16)],
        scratch_shapes=[pltpu.VMEM((tm + 2 * POOL_HALO, D_MODEL), BF16),
                        pltpu.VMEM((tm + 2 * POOL_HALO, D_MODEL), BF16)],
        compiler_params=_params("parallel"),
        name="pool_layer",
    )(x_lat, x_ctx, x_lat, x_lat, mods_l, row(g1), row(g2), _pool_bands(tm), w_all, row(ls))


ATTN_CAST_ROWS = 16


def _ffn_kernel(*refs, nf, final, emit_next_h, cast_plan):
    it = iter(refs)
    x_ref, h_ref, mod_ref, w1_ref, w3_ref, w2_ref, fg_ref = (next(it) for _ in range(7))
    if emit_next_h:
        gn_ref, modn_ref = next(it), next(it)
    src_refs = [[next(it) for _ in cols] for _, cols in cast_plan]
    o_ref = next(it)
    if emit_next_h:
        hn_ref = next(it)
    dst_refs = [next(it) for _ in cast_plan]
    acc_ref = next(it)
    f = pl.program_id(1)

    for (kind, cols), srcs, dst in zip(cast_plan, src_refs, dst_refs):
        off = 0
        for j, (c, src) in enumerate(zip(cols, srcs)):
            if kind == "stack":
                dst[j] = src[...].astype(BF16)
            else:
                dst[:, off:off + c] = src[...].astype(BF16)
                off += c

    @pl.when(f == 0)
    def _():
        acc_ref[...] = jnp.zeros_like(acc_ref)

    h = h_ref[...]
    a = _dot(h, w1_ref[...])
    b = _dot(h, w3_ref[...])
    acc_ref[...] += _dot((_silu(a) * b).astype(BF16), w2_ref[...])

    @pl.when(f == nf - 1)
    def _():
        y = x_ref[...] + mod_ref[5:6, :] * acc_ref[...]
        if final:
            y = _rms(y, fg_ref[...])
        o_ref[...] = y
        if emit_next_h:
            hn_ref[...] = _norm_mod(y, gn_ref[...], modn_ref[0:1, :], modn_ref[1:2, :]).astype(BF16)


def _ffn_layer(x, h, mods_l, w1_all, w3_all, w2_all, layer, final_g, n_rows, final, next_norm=None, side_casts=()):
    tm, tf = 512, 512
    nf = FFN_HIDDEN // tf
    emit_next_h = next_norm is not None
    n_slabs = D_MODEL // ATTN_CAST_ROWS
    assert not side_casts or (n_rows // tm) * nf >= n_slabs
    slab = lambda t, f: jnp.minimum(t * nf + f, n_slabs - 1)
    tile = pl.BlockSpec((tm, D_MODEL), lambda t, f: (t, 0))
    mod_spec = pl.BlockSpec((None, 6, D_MODEL), lambda t, f: (_mod_row(t, tm), 0, 0))
    row_spec = pl.BlockSpec((1, D_MODEL), lambda t, f: (0, 0))
    in_specs = [
        tile, tile, mod_spec,
        pl.BlockSpec((None, D_MODEL, tf), lambda t, f: (layer, 0, f)),
        pl.BlockSpec((None, D_MODEL, tf), lambda t, f: (layer, 0, f)),
        pl.BlockSpec((None, tf, D_MODEL), lambda t, f: (layer, f, 0)),
        row_spec,
    ]
    args = [x, h, mods_l, w1_all, w3_all, w2_all, final_g.reshape(1, D_MODEL)]
    out_specs, out_shape = [tile], [jax.ShapeDtypeStruct((n_rows, D_MODEL), F32)]
    if emit_next_h:
        in_specs += [row_spec, mod_spec]
        args += [next_norm[0].reshape(1, D_MODEL), next_norm[1]]
        out_specs.append(tile)
        out_shape.append(jax.ShapeDtypeStruct((n_rows, D_MODEL), BF16))
    cast_plan = []
    for kind, idx, weights in side_casts:
        cols = tuple(w.shape[-1] for w in weights)
        cast_plan.append((kind, cols))
        for w, c in zip(weights, cols):
            in_specs.append(pl.BlockSpec((None, ATTN_CAST_ROWS, c), lambda t, f, idx=idx: (idx, slab(t, f), 0)))
            args.append(w)
        if kind == "stack":
            out_specs.append(pl.BlockSpec((len(cols), ATTN_CAST_ROWS, cols[0]), lambda t, f: (0, slab(t, f), 0)))
            out_shape.append(jax.ShapeDtypeStruct((len(cols), D_MODEL, cols[0]), BF16))
        else:
            out_specs.append(pl.BlockSpec((ATTN_CAST_ROWS, sum(cols)), lambda t, f: (slab(t, f), 0)))
            out_shape.append(jax.ShapeDtypeStruct((D_MODEL, sum(cols)), BF16))
    outs = pl.pallas_call(
        functools.partial(_ffn_kernel, nf=nf, final=final, emit_next_h=emit_next_h, cast_plan=tuple(cast_plan)),
        grid=(n_rows // tm, nf),
        in_specs=in_specs,
        out_specs=out_specs,
        out_shape=out_shape,
        scratch_shapes=[pltpu.VMEM((tm, D_MODEL), F32)],
        compiler_params=_params("parallel", "arbitrary"),
        name="ffn_layer",
    )(*args)
    return outs[0] if len(outs) == 1 else outs


def _attn_out_kernel(ol_ref, oc_ref, x_ref, mod_ref, wo_ref, g2_ref, o_ref, h2_ref, *, tm):
    t = pl.program_id(0)
    o = jnp.where(t < ROWS_LAT // tm, ol_ref[...], oc_ref[...])
    y = x_ref[...] + mod_ref[2:3, :] * _dot(o, wo_ref[...])
    o_ref[...] = y
    h2_ref[...] = _norm_mod(y, g2_ref[...], mod_ref[3:4, :], mod_ref[4:5, :]).astype(BF16)


def _attn_out_layer(o_lat, o_ctx, x, mods_l, wo, g2):
    tm = 512
    n_lat = ROWS_LAT // tm
    return pl.pallas_call(
        functools.partial(_attn_out_kernel, tm=tm),
        grid=(ROWS_ALL // tm,),
        in_specs=[
            pl.BlockSpec((tm, D_MODEL), lambda t: (jnp.minimum(t, n_lat - 1), 0)),
            pl.BlockSpec((tm, D_MODEL), lambda t: (jnp.maximum(t - n_lat, 0), 0)),
            pl.BlockSpec((tm, D_MODEL), lambda t: (t, 0)),
            pl.BlockSpec((None, 6, D_MODEL), lambda t: (_mod_row(t, tm), 0, 0)),
            pl.BlockSpec((D_MODEL, D_MODEL), lambda t: (0, 0)),
            pl.BlockSpec((1, D_MODEL), lambda t: (0, 0)),
        ],
        out_specs=[pl.BlockSpec((tm, D_MODEL), lambda t: (t, 0)), pl.BlockSpec((tm, D_MODEL), lambda t: (t, 0))],
        out_shape=[jax.ShapeDtypeStruct((ROWS_ALL, D_MODEL), F32), jax.ShapeDtypeStruct((ROWS_ALL, D_MODEL), BF16)],
        compiler_params=_params("parallel"),
        name="attn_out",
    )(o_lat, o_ctx, x, mods_l, wo, g2.reshape(1, D_MODEL))


def _rope(xh, cosf, sinf, even):
    swapped = jnp.where(even, pltpu.roll(xh, HEAD_DIM - 1, 1), pltpu.roll(xh, 1, 1))
    return xh * cosf + swapped * sinf


def _gqa_proj_kernel(x_ref, mod_ref, g_ref, w_ref, qn_ref, kn_ref, cos_ref, sin_ref, q_ref, k_ref, v_ref, *, tm):
    h = _norm_mod(x_ref[...], g_ref[...], mod_ref[0:1, :], mod_ref[1:2, :]).astype(BF16)
    qkv = _dot(h, w_ref[...])
    cosf, sinf = cos_ref[...], sin_ref[...]
    even = lax.broadcasted_iota(jnp.int32, (tm, HEAD_DIM), 1) % 2 == 0
    for hd in range(GQA_HEADS):
        lanes = slice(hd * HEAD_DIM, (hd + 1) * HEAD_DIM)
        qh = _rope(_rms(qkv[:, lanes], qn_ref[...]), cosf, sinf, even)
        q_ref[:, lanes] = (qh * QK_PRESCALE).astype(BF16)
    k0 = GQA_HEADS * HEAD_DIM
    for hd in range(GQA_KV_HEADS):
        lanes = slice(hd * HEAD_DIM, (hd + 1) * HEAD_DIM)
        kh = qkv[:, k0 + hd * HEAD_DIM:k0 + (hd + 1) * HEAD_DIM]
        k_ref[:, lanes] = _rope(_rms(kh, kn_ref[...]), cosf, sinf, even).astype(BF16)
    v_ref[...] = qkv[:, k0 + GQA_KV_HEADS * HEAD_DIM:].astype(BF16)


def _gqa_proj(x, mods_l, g, w_qkv, qn, kn, cos_ext, sin_ext):
    tm = GQA_PROJ_ROWS
    dq, dkv = GQA_HEADS * HEAD_DIM, GQA_KV_HEADS * HEAD_DIM
    n_lat = ROWS_LAT // tm
    rope_blk = lambda t: (jnp.where(t < n_lat, t % (SEQ // tm), SEQ // tm), 0)
    return pl.pallas_call(
        functools.partial(_gqa_proj_kernel, tm=tm),
        grid=(ROWS_ALL // tm,),
        in_specs=[
            pl.BlockSpec((tm, D_MODEL), lambda t: (t, 0)),
            pl.BlockSpec((None, 6, D_MODEL), lambda t: (_mod_row(t, tm), 0, 0)),
            pl.BlockSpec((1, D_MODEL), lambda t: (0, 0)),
            pl.BlockSpec((D_MODEL, dq + 2 * dkv), lambda t: (0, 0)),
            pl.BlockSpec((1, HEAD_DIM), lambda t: (0, 0)),
            pl.BlockSpec((1, HEAD_DIM), lambda t: (0, 0)),
            pl.BlockSpec((tm, HEAD_DIM), rope_blk),
            pl.BlockSpec((tm, HEAD_DIM), rope_blk),
        ],
        out_specs=[
            pl.BlockSpec((tm, dq), lambda t: (t, 0)),
            pl.BlockSpec((tm, dkv), lambda t: (t, 0)),
            pl.BlockSpec((tm, dkv), lambda t: (t, 0)),
        ],
        out_shape=[
            jax.ShapeDtypeStruct((ROWS_ALL, dq), BF16),
            jax.ShapeDtypeStruct((ROWS_ALL, dkv), BF16),
            jax.ShapeDtypeStruct((ROWS_ALL, dkv), BF16),
        ],
        compiler_params=_params("parallel"),
        name="gqa_proj",
    )(x, mods_l, g.reshape(1, D_MODEL), w_qkv, qn.reshape(1, HEAD_DIM), kn.reshape(1, HEAD_DIM), cos_ext, sin_ext)


def _softmax_rows(s):
    m = jnp.max(s, axis=-1, keepdims=True)
    p = jnp.exp2(s - m)
    return p, jnp.sum(p, axis=-1, keepdims=True)


def _stack_heads(ref, n):
    return jnp.concatenate([ref[:, g * HEAD_DIM:(g + 1) * HEAD_DIM] for g in range(n)], axis=0)


def _online_softmax_chunk(s_ref, p_ref, m_ref, a_ref, n_rows, width):
    nslab = width // LANE
    blocks = [slice(i * SOFTMAX_ROWS, (i + 1) * SOFTMAX_ROWS) for i in range(n_rows // SOFTMAX_ROWS)]
    for rows in blocks:
        mc = functools.reduce(jnp.maximum, [s_ref[rows, c * LANE:(c + 1) * LANE] for c in range(nslab)])
        m_prev = m_ref[rows, :]
        m_new = jnp.maximum(m_prev, jnp.max(mc, axis=-1, keepdims=True))
        a_ref[rows, :] = jnp.exp2(m_prev - m_new)
        m_ref[rows, :] = m_new
    for rows in blocks:
        m_new = m_ref[rows, :]
        for c in range(nslab):
            p_ref[rows, c * LANE:(c + 1) * LANE] = jnp.exp2(s_ref[rows, c * LANE:(c + 1) * LANE] - m_new).astype(BF16)


FFN_WEIGHT_SHAPES = ((D_MODEL, FFN_HIDDEN), (D_MODEL, FFN_HIDDEN), (FFN_HIDDEN, D_MODEL))


def _weight_cast_specs(first_layer, n_layers, n_steps, step_id):
    per_layer = n_steps // n_layers
    in_specs, out_specs, out_shapes = [], [], []
    for rows, cols in FFN_WEIGHT_SHAPES:
        blk = (None, rows // per_layer, cols)
        in_specs.append(pl.BlockSpec(
            blk, lambda *g: (first_layer + step_id(*g) // per_layer, step_id(*g) % per_layer, 0)))
        out_specs.append(pl.BlockSpec(blk, lambda *g: (step_id(*g) // per_layer, step_id(*g) % per_layer, 0)))
        out_shapes.append(jax.ShapeDtypeStruct((n_layers, rows, cols), BF16))
    return in_specs, out_specs, out_shapes


def _cast_weight_slabs(src_refs, dst_refs):
    for src, dst in zip(src_refs, dst_refs):
        dst[...] = src[...].astype(BF16)


def _gqa_attn_kernel(q_ref, qc_ref, k_ref, v_ref, kc_ref, vc_ref, w1f_ref, w3f_ref, w2f_ref,
                     ol_ref, oc_ref, w1b_ref, w3b_ref, w2b_ref,
                     s_ref, p_ref, m_ref, a_ref, acc_ref, vaug_ref, *, tq, tk, n_sub):
    _cast_weight_slabs((w1f_ref, w3f_ref, w2f_ref), (w1b_ref, w3b_ref, w2b_ref))
    kc, vc = kc_ref[...], vc_ref[...]
    sub_q = tq // n_sub
    n_rows = GQA_GROUP * sub_q

    @pl.when(pl.program_id(2) == 0)
    def _():
        p, l = _softmax_rows(_dot_nt(_stack_heads(qc_ref, GQA_GROUP), kc))
        oc = _dot(p.astype(BF16), vc) / l
        for g in range(GQA_GROUP):
            oc_ref[:, g * HEAD_DIM:(g + 1) * HEAD_DIM] = oc[g * CTX_LEN:(g + 1) * CTX_LEN].astype(BF16)
        vaug_ref[0:SEQ, 0:HEAD_DIM] = v_ref[...]
        vaug_ref[SEQ:, 0:HEAD_DIM] = vc
        vaug_ref[:, HEAD_DIM:] = jnp.ones((SEQ + CTX_LEN, HEAD_DIM), BF16)

    qs = [_stack_heads(q_ref.at[u * sub_q:(u + 1) * sub_q, :], GQA_GROUP) for u in range(n_sub)]
    m_ref[...] = jnp.full_like(m_ref, MASK_VALUE)
    acc_ref[...] = jnp.zeros_like(acc_ref)

    chunks = [(k_ref, j * tk, j * tk, tk) for j in range(SEQ // tk)] + [(kc_ref, 0, SEQ, CTX_LEN)]
    for n, (kr, off, voff, width) in enumerate(chunks):
        for u in range(n_sub):
            sb, pb = s_ref.at[u, n % 2], p_ref.at[u, n % 2]
            sb[:, :width] = _dot_nt(qs[u], kr[off:off + width, :])
            _online_softmax_chunk(sb, pb, m_ref.at[u], a_ref.at[u], n_rows, width)
            pv = _dot(pb[:, :width], vaug_ref[voff:voff + width, :])
            for half in (slice(0, HEAD_DIM), slice(HEAD_DIM, 2 * HEAD_DIM)):
                acc_ref[u, :, half] = a_ref[u] * acc_ref[u, :, half] + pv[:, half]

    for u in range(n_sub):
        o = acc_ref[u, :, 0:HEAD_DIM] / acc_ref[u, :, HEAD_DIM:]
        for g in range(GQA_GROUP):
            ol_ref[u * sub_q:(u + 1) * sub_q, g * HEAD_DIM:(g + 1) * HEAD_DIM] = (
                o[g * sub_q:(g + 1) * sub_q].astype(BF16))


def _gqa_attn(q, k, v, ffn_weights, first_layer, n_layers):
    tq, tk, n_sub = 512, 1024, 2
    sub_rows = GQA_GROUP * tq // n_sub
    gw = GQA_GROUP * HEAD_DIM
    nq = SEQ // tq
    ctx_blk = ROWS_LAT // CTX_LEN
    w_in, w_out, w_shapes = _weight_cast_specs(first_layer, n_layers, BATCH * GQA_KV_HEADS * nq,
                                               lambda b, h, i: (b * GQA_KV_HEADS + h) * nq + i)
    return pl.pallas_call(
        functools.partial(_gqa_attn_kernel, tq=tq, tk=tk, n_sub=n_sub),
        grid=(BATCH, GQA_KV_HEADS, nq),
        in_specs=[
            pl.BlockSpec((tq, gw), lambda b, h, i: (b * nq + i, h)),
            pl.BlockSpec((CTX_LEN, gw), lambda b, h, i: (ctx_blk + b, h)),
            pl.BlockSpec((SEQ, HEAD_DIM), lambda b, h, i: (b, h)),
            pl.BlockSpec((SEQ, HEAD_DIM), lambda b, h, i: (b, h)),
            pl.BlockSpec((CTX_LEN, HEAD_DIM), lambda b, h, i: (ctx_blk + b, h)),
            pl.BlockSpec((CTX_LEN, HEAD_DIM), lambda b, h, i: (ctx_blk + b, h)),
        ] + w_in,
        out_specs=[
            pl.BlockSpec((tq, gw), lambda b, h, i: (b * nq + i, h)),
            pl.BlockSpec((CTX_LEN, gw), lambda b, h, i: (b, h)),
        ] + w_out,
        out_shape=[
            jax.ShapeDtypeStruct((ROWS_LAT, D_MODEL), BF16),
            jax.ShapeDtypeStruct((ROWS_CTX, D_MODEL), BF16),
        ] + w_shapes,
        scratch_shapes=[
            pltpu.VMEM((n_sub, 2, sub_rows, tk), F32),
            pltpu.VMEM((n_sub, 2, sub_rows, tk), BF16),
            pltpu.VMEM((n_sub, sub_rows, LANE), F32),
            pltpu.VMEM((n_sub, sub_rows, LANE), F32),
            pltpu.VMEM((n_sub, sub_rows, 2 * HEAD_DIM), F32),
            pltpu.VMEM((SEQ + CTX_LEN, 2 * HEAD_DIM), BF16),
        ],
        compiler_params=_params("parallel", "parallel", "arbitrary"),
        name="gqa_attn",
    )(q, q, k, v, k, v, *ffn_weights)


def _nat_proj_kernel(h_ref, w_ref, o_ref):
    prescale = jnp.where(pl.program_id(0) == 0, QK_PRESCALE, 1.0)
    o_ref[...] = (_dot(h_ref[...], w_ref[...]) * prescale).astype(BF16)


def _nat_proj(h, w_qkv):
    tm = 512
    return pl.pallas_call(
        _nat_proj_kernel,
        grid=(3, ROWS_ALL // tm),
        in_specs=[
            pl.BlockSpec((tm, D_MODEL), lambda n, t: (t, 0)),
            pl.BlockSpec((None, D_MODEL, D_MODEL), lambda n, t: (n, 0, 0)),
        ],
        out_specs=pl.BlockSpec((None, tm, D_MODEL), lambda n, t: (n, t, 0)),
        out_shape=jax.ShapeDtypeStruct((3, ROWS_ALL, D_MODEL), BF16),
        compiler_params=_params("parallel", "parallel"),
        name="nat_proj",
    )(h, w_qkv)


NAT_BLOCK_ROWS = 8
NAT_WIN_ROWS = NAT_BLOCK_ROWS + NAT_KH
NAT_Q = NAT_BLOCK_ROWS * GRID_W
NAT_K = NAT_WIN_ROWS * GRID_W
NAT_SLABS = NAT_K // LANE
NAT_PAIR_TILES = 2 * NAT_KH - 2
NAT_TILE_LEFT_MASKED = NAT_PAIR_TILES
NAT_TILE_RIGHT_MASKED = NAT_PAIR_TILES + 1
NAT_TILES = NAT_PAIR_TILES + 2
NAT_BUFFERS = 3


def _nat_row_plan(cls, i):
    half = NAT_KH // 2
    if cls == "first":
        r, ws = i, 0
    elif cls == "mid":
        r, ws = NAT_BLOCK_ROWS + i, NAT_BLOCK_ROWS - half
    else:
        r, ws = GRID_H - NAT_BLOCK_ROWS + i, GRID_H - NAT_WIN_ROWS
    rstart = min(max(r - half, 0), GRID_H - NAT_KH)
    p, delta = r - rstart, rstart - ws
    assert 0 <= delta and delta + NAT_KH <= NAT_WIN_ROWS
    if delta % 2 == 0:
        slabs = [delta // 2 + t for t in range(half)]
        tiles = [2 * t - p + NAT_KH - 1 for t in range(half)]
    else:
        assert p == half
        slabs = [(delta - 1) // 2 + t for t in range(half + 1)]
        tiles = ([NAT_TILE_LEFT_MASKED] + [2 * t - 1 - p + NAT_KH - 1 for t in range(1, half)]
                 + [NAT_TILE_RIGHT_MASKED])
    return slabs, tiles


def _nat_build_tiles(rpbw_ref, t2_ref):
    lane = lax.broadcasted_iota(jnp.int32, (GRID_W, LANE), 1)
    cq = lax.broadcasted_iota(jnp.int32, (GRID_W, LANE), 0)
    kc = lane % GRID_W
    cstart = jnp.clip(cq - NAT_KW // 2, 0, GRID_W - NAT_KW)
    inside = (kc >= cstart) & (kc < cstart + NAT_KW)
    for d in range(NAT_PAIR_TILES):
        w = jnp.broadcast_to(rpbw_ref[d:d + 1, :] * LOG2E, (GRID_W, LANE))
        t = pltpu.roll(w, LANE - (NAT_KW - 1), 1, stride=1, stride_axis=0)
        t2_ref[d] = jnp.where(inside, t, MASK_VALUE)
    half = NAT_KH // 2
    t2_ref[NAT_TILE_LEFT_MASKED] = jnp.where(lane < GRID_W, MASK_VALUE, t2_ref[half - 2])
    t2_ref[NAT_TILE_RIGHT_MASKED] = jnp.where(lane < GRID_W, t2_ref[NAT_KH + half - 2], MASK_VALUE)


def _nat_softmax_block(cls, s_ref, sc_ref, t2_ref, p_ref, pc_ref, m_ref):
    n_ctx_slabs = CTX_LEN // LANE
    subs = []
    for i in range(NAT_BLOCK_ROWS):
        slabs, tiles = _nat_row_plan(cls, i)
        for a in range(NAT_SLABS):
            if a not in slabs:
                p_ref[i * GRID_W:(i + 1) * GRID_W, a * LANE:(a + 1) * LANE] = jnp.zeros((GRID_W, LANE), BF16)
        for sub in range(GRID_W // SOFTMAX_ROWS):
            trow = slice(sub * SOFTMAX_ROWS, (sub + 1) * SOFTMAX_ROWS)
            rows = slice(i * GRID_W + sub * SOFTMAX_ROWS, i * GRID_W + (sub + 1) * SOFTMAX_ROWS)
            subs.append((rows, trow, slabs, tiles))

    def scores(rows, trow, slabs, tiles):
        sv = [s_ref[rows, a * LANE:(a + 1) * LANE] + t2_ref[t, trow, :] for a, t in zip(slabs, tiles)]
        return sv + [sc_ref[rows, c * LANE:(c + 1) * LANE] for c in range(n_ctx_slabs)]

    for rows, trow, slabs, tiles in subs:
        m = jnp.max(functools.reduce(jnp.maximum, scores(rows, trow, slabs, tiles)), axis=-1, keepdims=True)
        m_ref[rows, :] = jnp.broadcast_to(m, (SOFTMAX_ROWS, LANE))
    for rows, trow, slabs, tiles in subs:
        m = m_ref[rows, :]
        pv = [jnp.exp2(s - m) for s in scores(rows, trow, slabs, tiles)]
        for a, pa in zip(slabs, pv):
            p_ref[rows, a * LANE:(a + 1) * LANE] = pa.astype(BF16)
        for c in range(n_ctx_slabs):
            pc_ref[rows, c * LANE:(c + 1) * LANE] = pv[len(slabs) + c].astype(BF16)


def _nat_attn_kernel(q_ref, k_ref, v_ref, qc_ref, kc_ref, vc_ref, rpbw_ref, w1f_ref, w3f_ref, w2f_ref,
                     ol_ref, oc_ref, w1b_ref, w3b_ref, w2b_ref,
                     t2_ref, s_ref, sc_ref, p_ref, pc_ref, m_ref, vaug_ref):
    _cast_weight_slabs((w1f_ref, w3f_ref, w2f_ref), (w1b_ref, w3b_ref, w2b_ref))
    kc, vc = kc_ref[...], vc_ref[...]
    p, l = _softmax_rows(_dot_nt(qc_ref[...], kc))
    oc_ref[...] = (_dot(p.astype(BF16), vc) / l).astype(BF16)

    _nat_build_tiles(rpbw_ref, t2_ref)
    vaug_ref[0:SEQ, 0:HEAD_DIM] = v_ref[...]
    vaug_ref[SEQ:, 0:HEAD_DIM] = vc
    vaug_ref[:, HEAD_DIM:] = jnp.ones((SEQ + CTX_LEN, HEAD_DIM), BF16)

    n_blocks = GRID_H // NAT_BLOCK_ROWS
    half_win = (NAT_KH // 2) * GRID_W
    specs = [("first", 0, 0)]
    specs += [("mid", j * NAT_Q, j * NAT_Q - half_win) for j in range(1, n_blocks - 1)]
    specs += [("last", SEQ - NAT_Q, SEQ - NAT_K)]

    def scores(n):
        _, q_tok, k_tok = specs[n]
        qb = q_ref[q_tok:q_tok + NAT_Q, :]
        s_ref[n % NAT_BUFFERS] = _dot_nt(qb, k_ref[k_tok:k_tok + NAT_K, :])
        sc_ref[n % NAT_BUFFERS] = _dot_nt(qb, kc)

    def finish(n):
        cls, q_tok, k_tok = specs[n]
        b = n % NAT_BUFFERS
        _nat_softmax_block(cls, s_ref.at[b], sc_ref.at[b], t2_ref, p_ref.at[b], pc_ref.at[b], m_ref.at[b])
        o = _dot(p_ref[b], vaug_ref[k_tok:k_tok + NAT_K, :]) + _dot(pc_ref[b], vaug_ref[SEQ:, :])
        ol_ref[q_tok:q_tok + NAT_Q, :] = (o[:, 0:HEAD_DIM] / o[:, HEAD_DIM:]).astype(BF16)

    scores(0)
    for n in range(n_blocks):
        if n + 1 < n_blocks:
            scores(n + 1)
        finish(n)


def _nat_attn(qkv, rpbw, ffn_weights, first_layer, n_layers):
    ctx_blk = ROWS_LAT // CTX_LEN
    lat = lambda n: pl.BlockSpec((None, SEQ, HEAD_DIM), lambda b, h: (n, b, h))
    ctx = lambda n: pl.BlockSpec((None, CTX_LEN, HEAD_DIM), lambda b, h: (n, ctx_blk + b, h))
    w_in, w_out, w_shapes = _weight_cast_specs(first_layer, n_layers, BATCH * NAT_HEADS,
                                               lambda b, h: b * NAT_HEADS + h)
    return pl.pallas_call(
        _nat_attn_kernel,
        grid=(BATCH, NAT_HEADS),
        in_specs=[lat(0), lat(1), lat(2), ctx(0), ctx(1), ctx(2),
                  pl.BlockSpec((None, 2 * NAT_KH, LANE), lambda b, h: (h, 0, 0))] + w_in,
        out_specs=[
            pl.BlockSpec((SEQ, HEAD_DIM), lambda b, h: (b, h)),
            pl.BlockSpec((CTX_LEN, HEAD_DIM), lambda b, h: (b, h)),
        ] + w_out,
        out_shape=[
            jax.ShapeDtypeStruct((ROWS_LAT, D_MODEL), BF16),
            jax.ShapeDtypeStruct((ROWS_CTX, D_MODEL), BF16),
        ] + w_shapes,
        scratch_shapes=[
            pltpu.VMEM((NAT_TILES, GRID_W, LANE), F32),
            pltpu.VMEM((NAT_BUFFERS, NAT_Q, NAT_K), F32),
            pltpu.VMEM((NAT_BUFFERS, NAT_Q, CTX_LEN), F32),
            pltpu.VMEM((NAT_BUFFERS, NAT_Q, NAT_K), BF16),
            pltpu.VMEM((NAT_BUFFERS, NAT_Q, CTX_LEN), BF16),
            pltpu.VMEM((NAT_BUFFERS, NAT_Q, LANE), F32),
            pltpu.VMEM((SEQ + CTX_LEN, 2 * HEAD_DIM), BF16),
        ],
        compiler_params=_params("parallel", "parallel"),
        name="nat_attn",
    )(qkv, qkv, qkv, qkv, qkv, qkv, rpbw, *ffn_weights)


def _nat_rpb_pairs(rpb):
    padded = jnp.pad(rpb, ((0, 0), (0, 2), (0, GRID_W - (2 * NAT_KW - 1))))
    return jnp.concatenate([padded[:, :2 * NAT_KH], padded[:, 1:2 * NAT_KH + 1]], axis=-1)


def _rope_tables(tm):
    t = jnp.arange(SEQ)
    row = (t // GRID_W).astype(F32)
    col = (t % GRID_W).astype(F32)
    inv = ROPE_THETA ** (-jnp.arange(0, ROPE_AXIS_DIM, 2, dtype=F32) / ROPE_AXIS_DIM)
    ang = jnp.concatenate([row[:, None] * inv, col[:, None] * inv], axis=-1)
    cosf = jnp.repeat(jnp.cos(ang), 2, axis=-1)
    sign = jnp.tile(jnp.array([-1.0, 1.0], F32), HEAD_DIM // 2)
    sinf = jnp.repeat(jnp.sin(ang), 2, axis=-1) * sign
    cos_ext = jnp.concatenate([cosf, jnp.ones((tm, HEAD_DIM), F32)], axis=0)
    sin_ext = jnp.concatenate([sinf, jnp.zeros((tm, HEAD_DIM), F32)], axis=0)
    return cos_ext, sin_ext


def kernel(x, c, ctx, c_ctx, ada_w, ada_b, norm_g, ffn_w1, ffn_w3, ffn_w2, pool_w, pool_ls, gqa_wq, gqa_wk, gqa_wv, gqa_wo, gqa_qn, gqa_kn, nat_wq, nat_wk, nat_wv, nat_wo, nat_rpb, final_g):
    c8 =jnp.concatenate([c, c_ctx[None, :], jnp.zeros((8 - BATCH - 1, D_MODEL), F32)], axis=0)
    mods = _ada_all(c8, ada_w, ada_b)[:, :BATCH + 1].reshape(DEPTH, BATCH + 1, 6, D_MODEL)
    pool_w_all = pool_w.astype(BF16)
    ffn_f32 = (ffn_w1, ffn_w3, ffn_w2)
    ffn_bf16 = {0: (tuple(w[0:1].astype(BF16) for w in ffn_f32), 0)}
    gqa_cast_layers, nat_cast_layers = (1, 2), (3, 1)
    attn_layers = [i for i in range(1, DEPTH) if i % N_MIXERS != 0]
    attn_bf16 = {}

    xa = None
    for i in range(DEPTH):
        kind, j = i % N_MIXERS, i // N_MIXERS
        last = i == DEPTH - 1
        n_rows = ROWS_LAT if last else ROWS_ALL
        g1, g2 = norm_g[i, 0], norm_g[i, 1]
        if kind == 0:
            x_lat, x_ctx = (x.reshape(ROWS_LAT, D_MODEL), ctx.reshape(ROWS_CTX, D_MODEL)) if i == 0 else (xa, xa)
            xa, h2 = _pool_layer(x_lat, x_ctx, mods[i], g1, g2, pool_w_all, j, pool_ls[j], n_rows)
        else:
            w_qkv, wo = attn_bf16[i]
            if kind == 1:
                cos_ext, sin_ext = _rope_tables(GQA_PROJ_ROWS)
                q, k, v = _gqa_proj(xa, mods[i], g1, w_qkv, gqa_qn[j], gqa_kn[j], cos_ext, sin_ext)
                first, count = gqa_cast_layers
                o_lat, o_ctx, *cast = _gqa_attn(q, k, v, ffn_f32, first, count)
            else:
                qkv = _nat_proj(h1, w_qkv)
                first, count = nat_cast_layers
                o_lat, o_ctx, *cast = _nat_attn(qkv, _nat_rpb_pairs(nat_rpb[j]), ffn_f32, first, count)
            for n in range(count):
                ffn_bf16[first + n] = (tuple(cast), n)
            xa, h2 = _attn_out_layer(o_lat, o_ctx, xa, mods[i], wo, g2)
        (w1_all, w3_all, w2_all), idx = ffn_bf16[i]
        next_norm = (norm_g[i + 1, 0], mods[i + 1]) if not last and (i + 1) % N_MIXERS == 2 else None
        side_casts = []
        if i == 0:
            for a in attn_layers:
                ja = a // N_MIXERS
                if a % N_MIXERS == 1:
                    side_casts += [("concat", ja, [gqa_wq, gqa_wk, gqa_wv]), ("concat", ja, [gqa_wo])]
                else:
                    side_casts += [("stack", ja, [nat_wq, nat_wk, nat_wv]), ("concat", ja, [nat_wo])]
        outs = _ffn_layer(xa, h2, mods[i], w1_all, w3_all, w2_all, idx, final_g, n_rows, last,
                          next_norm=next_norm, side_casts=side_casts)
        if next_norm is None and not side_casts:
            xa = outs
        else:
            xa, rest = outs[0], list(outs[1:])
            if next_norm is not None:
                h1 = rest.pop(0)
            for n, a in enumerate(attn_layers if side_casts else ()):
                attn_bf16[a] = (rest[2 * n], rest[2 * n + 1])
    return xa.reshape(BATCH, SEQ, D_MODEL)
```

```python
import functools

import jax
import jax.numpy as jnp
import numpy as np
from jax import lax
from jax.experimental import pallas as pl
from jax.experimental.pallas import tpu as pltpu

D_MODEL = 2048
BATCH = 2
SEQ = 4096
DEPTH = 4
GRID_W = 64
CTX_LEN = 256
N_MIXERS = 3
POOL_GROUPS = 4
POOL_WINDOWS = (2, 4, 8, 16)
POOL_GC = D_MODEL // POOL_GROUPS
HEAD_DIM = 128
GQA_HEADS = D_MODEL // HEAD_DIM
GQA_KV_HEADS = 4
GQA_GROUP = GQA_HEADS // GQA_KV_HEADS
ROPE_THETA = 10000.0
ROPE_AXIS_DIM = HEAD_DIM // 2
NAT_HEADS = D_MODEL // HEAD_DIM
NAT_KH = 8
NAT_KW = 16
FFN_HIDDEN = -(-8 * D_MODEL // (3 * 256)) * 256
NORM_EPS = 1e-6

ROWS_LAT = BATCH * SEQ
ROWS_CTX = BATCH * CTX_LEN
ROWS_ALL = ROWS_LAT + ROWS_CTX
GRID_H = SEQ // GRID_W
ATTN_SCALE = HEAD_DIM ** -0.5
LOG2E = 1.4426950408889634
QK_PRESCALE = ATTN_SCALE * LOG2E
MASK_VALUE = -1e30
LANE = 128
POOL_HALO = 8
SOFTMAX_ROWS = 32
GQA_PROJ_ROWS = 256

VMEM_LIMIT_BYTES = 56 * 1024 * 1024

F32 = jnp.float32
BF16 = jnp.bfloat16


def _params(*sem):
    return pltpu.CompilerParams(dimension_semantics=sem, vmem_limit_bytes=VMEM_LIMIT_BYTES)


def _dot(a, b):
    return jnp.dot(a, b, preferred_element_type=F32)


def _dot_nt(a, b):
    return lax.dot_general(a, b, (((1,), (1,)), ((), ())), preferred_element_type=F32)


def _silu(a):
    return a * (1.0 / (1.0 + jnp.exp(-a)))


def _rms(x, g):
    return x * lax.rsqrt(jnp.mean(x * x, axis=-1, keepdims=True) + NORM_EPS) * g


def _norm_mod(x, g, shift, scale):
    return x * lax.rsqrt(jnp.mean(x * x, axis=-1, keepdims=True) + NORM_EPS) * (g * (1.0 + scale)) + shift


def _mod_row(t, tm):
    return jnp.where(t < ROWS_LAT // tm, t // (SEQ // tm), BATCH)


def _ada_kernel(c_ref, w_ref, b_ref, o_ref):
    s = _silu(c_ref[...])
    o_ref[...] = _dot(s.astype(BF16), w_ref[...].astype(BF16)) + b_ref[...]


def _ada_all(c8, ada_w, ada_b):
    tn = 1024
    return pl.pallas_call(
        _ada_kernel,
        grid=(DEPTH, 6 * D_MODEL // tn),
        in_specs=[
            pl.BlockSpec((8, D_MODEL), lambda l, n: (0, 0)),
            pl.BlockSpec((None, D_MODEL, tn), lambda l, n: (l, 0, n)),
            pl.BlockSpec((None, 1, tn), lambda l, n: (l, 0, n)),
        ],
        out_specs=pl.BlockSpec((None, 8, tn), lambda l, n: (l, 0, n)),
        out_shape=jax.ShapeDtypeStruct((DEPTH, 8, 6 * D_MODEL), F32),
        compiler_params=_params("parallel", "parallel"),
        name="ada",
    )(c8, ada_w, ada_b.reshape(DEPTH, 1, 6 * D_MODEL))


def _pool_bands(tm):
    t = np.arange(tm)[:, None]
    j = np.arange(tm + 2 * POOL_HALO)[None, :]
    pos = np.where(j < tm, j, np.where(j < tm + POOL_HALO, j - tm - POOL_HALO, j - POOL_HALO))
    bands = [(pos >= t - w // 2) & (pos < t + w - w // 2) for w in POOL_WINDOWS]
    return jnp.asarray(np.stack(bands).astype(np.float32), BF16)


def _split_bf16(v):
    hi = v.astype(BF16)
    return hi, (v - hi.astype(F32)).astype(BF16)


def _pool_kernel(xl_ref, xc_ref, xp_ref, xn_ref, mod_ref, g1_ref, g2_ref, band_ref, w_ref, ls_ref, *rest,
                 tm, cast_ffn):
    if cast_ffn:
        w1f_ref, w3f_ref, w2f_ref, o_ref, h2_ref, w1b_ref, w3b_ref, w2b_ref, hi_ref, lo_ref = rest
        _cast_weight_slabs((w1f_ref, w3f_ref, w2f_ref), (w1b_ref, w3b_ref, w2b_ref))
    else:
        o_ref, h2_ref, hi_ref, lo_ref = rest
    t = pl.program_id(0)
    shift, scale, gate = mod_ref[0:1, :], mod_ref[1:2, :], mod_ref[2:3, :]
    g = g1_ref[...]

    n_lat_tiles = ROWS_LAT // tm
    is_lat = t < n_lat_tiles
    tiles_in_seq = jnp.where(is_lat, SEQ // tm, CTX_LEN // tm)
    pos_tile = jnp.where(is_lat, t % (SEQ // tm), (t - n_lat_tiles) % (CTX_LEN // tm))
    first = pos_tile == 0
    last = pos_tile == tiles_in_seq - 1

    x = jnp.where(is_lat, xl_ref[...], xc_ref[...])
    h = _norm_mod(x, g, shift, scale)
    halo = jnp.concatenate([jnp.where(first, 0.0, _norm_mod(xp_ref[...], g, shift, scale)),
                            jnp.where(last, 0.0, _norm_mod(xn_ref[...], g, shift, scale))], axis=0)
    hi_ref[0:tm, :], lo_ref[0:tm, :] = _split_bf16(h)
    hi_ref[tm:, :], lo_ref[tm:, :] = _split_bf16(halo)

    pos = pos_tile * tm + lax.broadcasted_iota(jnp.int32, (tm, 1), 0)
    n = tiles_in_seq * tm
    for k in range(POOL_GROUPS):
        w = POOL_WINDOWS[k]
        lanes = slice(k * POOL_GC, (k + 1) * POOL_GC)
        wsum = _dot(band_ref[k], hi_ref[:, lanes]) + _dot(band_ref[k], lo_ref[:, lanes])
        cnt = jnp.minimum(pos + (w - w // 2), n) - jnp.maximum(pos - w // 2, 0)
        pooled = wsum / cnt.astype(F32) - h[:, lanes]
        y = _dot(pooled.astype(BF16), w_ref[k]) * ls_ref[:, lanes]
        o_ref[:, lanes] = x[:, lanes] + gate[:, lanes] * y
    h2_ref[...] = _norm_mod(o_ref[...], g2_ref[...], mod_ref[3:4, :], mod_ref[4:5, :]).astype(BF16)


def _pool_layer(x_lat, x_ctx, mods_l, g1, g2, w_all, j, ls, n_rows, cast_ffn_layer=None, ffn_weights=()):
    tm = 256
    n_lat = ROWS_LAT // tm
    nb8 = x_lat.shape[0] // POOL_HALO
    ctx_blk0 = (x_ctx.shape[0] - ROWS_CTX) // tm
    halo_blk = lambda b: jnp.clip(b, 0, nb8 - 1)
    row = lambda v: v.reshape(1, D_MODEL)
    cast_ffn = cast_ffn_layer is not None
    w_in, w_out, w_shapes = (_weight_cast_specs(cast_ffn_layer, 1, n_lat, lambda t: jnp.minimum(t, n_lat - 1))
                             if cast_ffn else ([], [], []))
    return pl.pallas_call(
        functools.partial(_pool_kernel, tm=tm, cast_ffn=cast_ffn),
        grid=(n_rows // tm,),
        in_specs=[
            pl.BlockSpec((tm, D_MODEL), lambda t: (jnp.minimum(t, n_lat - 1), 0)),
            pl.BlockSpec((tm, D_MODEL), lambda t: (ctx_blk0 + jnp.maximum(t - n_lat, 0), 0)),
            pl.BlockSpec((POOL_HALO, D_MODEL), lambda t: (halo_blk(t * (tm // POOL_HALO) - 1), 0)),
            pl.BlockSpec((POOL_HALO, D_MODEL), lambda t: (halo_blk((t + 1) * (tm // POOL_HALO)), 0)),
            pl.BlockSpec((None, 6, D_MODEL), lambda t: (_mod_row(t, tm), 0, 0)),
            pl.BlockSpec((1, D_MODEL), lambda t: (0, 0)),
            pl.BlockSpec((1, D_MODEL), lambda t: (0, 0)),
            pl.BlockSpec((POOL_GROUPS, tm, tm + 2 * POOL_HALO), lambda t: (0, 0, 0)),
            pl.BlockSpec((None, POOL_GROUPS, POOL_GC, POOL_GC), lambda t: (j, 0, 0, 0)),
            pl.BlockSpec((1, D_MODEL), lambda t: (0, 0)),
        ] + w_in,
        out_specs=[pl.BlockSpec((tm, D_MODEL), lambda t: (t, 0)),
                   pl.BlockSpec((tm, D_MODEL), lambda t: (t, 0))] + w_out,
        out_shape=[jax.ShapeDtypeStruct((n_rows, D_MODEL), F32),
                   jax.ShapeDtypeStruct((n_rows, D_MODEL), BF16)] + w_shapes,
        scratch_shapes=[pltpu.VMEM((tm + 2 * POOL_HALO, D_MODEL), BF16),
                        pltpu.VMEM((tm + 2 * POOL_HALO, D_MODEL), BF16)],
        compiler_params=_params("parallel"),
        name="pool_layer",
    )(x_lat, x_ctx, x_lat, x_lat, mods_l, row(g1), row(g2), _pool_bands(tm), w_all, row(ls), *ffn_weights)


ATTN_CAST_ROWS = 16


def _ffn_kernel(*refs, nf, final, emit_next_h, cast_plan):
    it = iter(refs)
    x_ref, h_ref, mod_ref, w1_ref, w3_ref, w2_ref, fg_ref = (next(it) for _ in range(7))
    if emit_next_h:
        gn_ref, modn_ref = next(it), next(it)
    src_refs = [[next(it) for _ in cols] for _, cols in cast_plan]
    o_ref = next(it)
    if emit_next_h:
        hn_ref = next(it)
    dst_refs = [next(it) for _ in cast_plan]
    acc_ref = next(it)
    f = pl.program_id(1)

    for (kind, cols), srcs, dst in zip(cast_plan, src_refs, dst_refs):
        off = 0
        for j, (c, src) in enumerate(zip(cols, srcs)):
            if kind == "stack":
                dst[j] = src[...].astype(BF16)
            else:
                dst[:, off:off + c] = src[...].astype(BF16)
                off += c

    @pl.when(f == 0)
    def _():
        acc_ref[...] = jnp.zeros_like(acc_ref)

    h = h_ref[...]
    a = _dot(h, w1_ref[...])
    b = _dot(h, w3_ref[...])
    acc_ref[...] += _dot((_silu(a) * b).astype(BF16), w2_ref[...])

    @pl.when(f == nf - 1)
    def _():
        y = x_ref[...] + mod_ref[5:6, :] * acc_ref[...]
        if final:
            y = _rms(y, fg_ref[...])
        o_ref[...] = y
        if emit_next_h:
            hn_ref[...] = _norm_mod(y, gn_ref[...], modn_ref[0:1, :], modn_ref[1:2, :]).astype(BF16)


def _ffn_layer(x, h, mods_l, w1_all, w3_all, w2_all, layer, final_g, n_rows, final, next_norm=None, side_casts=()):
    tm, tf = 512, 512
    nf = FFN_HIDDEN // tf
    emit_next_h = next_norm is not None
    n_slabs = D_MODEL // ATTN_CAST_ROWS
    assert not side_casts or (n_rows // tm) * nf >= n_slabs
    slab = lambda t, f: jnp.minimum(t * nf + f, n_slabs - 1)
    tile = pl.BlockSpec((tm, D_MODEL), lambda t, f: (t, 0))
    mod_spec = pl.BlockSpec((None, 6, D_MODEL), lambda t, f: (_mod_row(t, tm), 0, 0))
    row_spec = pl.BlockSpec((1, D_MODEL), lambda t, f: (0, 0))
    in_specs = [
        tile, tile, mod_spec,
        pl.BlockSpec((None, D_MODEL, tf), lambda t, f: (layer, 0, f)),
        pl.BlockSpec((None, D_MODEL, tf), lambda t, f: (layer, 0, f)),
        pl.BlockSpec((None, tf, D_MODEL), lambda t, f: (layer, f, 0)),
        row_spec,
    ]
    args = [x, h, mods_l, w1_all, w3_all, w2_all, final_g.reshape(1, D_MODEL)]
    out_specs, out_shape = [tile], [jax.ShapeDtypeStruct((n_rows, D_MODEL), F32)]
    if emit_next_h:
        in_specs += [row_spec, mod_spec]
        args += [next_norm[0].reshape(1, D_MODEL), next_norm[1]]
        out_specs.append(tile)
        out_shape.append(jax.ShapeDtypeStruct((n_rows, D_MODEL), BF16))
    cast_plan = []
    for kind, idx, weights in side_casts:
        cols = tuple(w.shape[-1] for w in weights)
        cast_plan.append((kind, cols))
        for w, c in zip(weights, cols):
            in_specs.append(pl.BlockSpec((None, ATTN_CAST_ROWS, c), lambda t, f, idx=idx: (idx, slab(t, f), 0)))
            args.append(w)
        if kind == "stack":
            out_specs.append(pl.BlockSpec((len(cols), ATTN_CAST_ROWS, cols[0]), lambda t, f: (0, slab(t, f), 0)))
            out_shape.append(jax.ShapeDtypeStruct((len(cols), D_MODEL, cols[0]), BF16))
        else:
            out_specs.append(pl.BlockSpec((ATTN_CAST_ROWS, sum(cols)), lambda t, f: (slab(t, f), 0)))
            out_shape.append(jax.ShapeDtypeStruct((D_MODEL, sum(cols)), BF16))
    outs = pl.pallas_call(
        functools.partial(_ffn_kernel, nf=nf, final=final, emit_next_h=emit_next_h, cast_plan=tuple(cast_plan)),
        grid=(n_rows // tm, nf),
        in_specs=in_specs,
        out_specs=out_specs,
        out_shape=out_shape,
        scratch_shapes=[pltpu.VMEM((tm, D_MODEL), F32)],
        compiler_params=_params("parallel", "arbitrary"),
        name="ffn_layer",
    )(*args)
    return outs[0] if len(outs) == 1 else outs


def _attn_out_kernel(ol_ref, oc_ref, x_ref, mod_ref, wo_ref, g2_ref, o_ref, h2_ref, *, tm):
    t = pl.program_id(0)
    o = jnp.where(t < ROWS_LAT // tm, ol_ref[...], oc_ref[...])
    y = x_ref[...] + mod_ref[2:3, :] * _dot(o, wo_ref[...])
    o_ref[...] = y
    h2_ref[...] = _norm_mod(y, g2_ref[...], mod_ref[3:4, :], mod_ref[4:5, :]).astype(BF16)


def _attn_out_layer(o_lat, o_ctx, x, mods_l, wo, g2):
    tm = 512
    n_lat = ROWS_LAT // tm
    return pl.pallas_call(
        functools.partial(_attn_out_kernel, tm=tm),
        grid=(ROWS_ALL // tm,),
        in_specs=[
            pl.BlockSpec((tm, D_MODEL), lambda t: (jnp.minimum(t, n_lat - 1), 0)),
            pl.BlockSpec((tm, D_MODEL), lambda t: (jnp.maximum(t - n_lat, 0), 0)),
            pl.BlockSpec((tm, D_MODEL), lambda t: (t, 0)),
            pl.BlockSpec((None, 6, D_MODEL), lambda t: (_mod_row(t, tm), 0, 0)),
            pl.BlockSpec((D_MODEL, D_MODEL), lambda t: (0, 0)),
            pl.BlockSpec((1, D_MODEL), lambda t: (0, 0)),
        ],
        out_specs=[pl.BlockSpec((tm, D_MODEL), lambda t: (t, 0)), pl.BlockSpec((tm, D_MODEL), lambda t: (t, 0))],
        out_shape=[jax.ShapeDtypeStruct((ROWS_ALL, D_MODEL), F32), jax.ShapeDtypeStruct((ROWS_ALL, D_MODEL), BF16)],
        compiler_params=_params("parallel"),
        name="attn_out",
    )(o_lat, o_ctx, x, mods_l, wo, g2.reshape(1, D_MODEL))


def _rope(xh, cosf, sinf, even):
    swapped = jnp.where(even, pltpu.roll(xh, HEAD_DIM - 1, 1), pltpu.roll(xh, 1, 1))
    return xh * cosf + swapped * sinf


def _gqa_proj_kernel(x_ref, mod_ref, g_ref, w_ref, qn_ref, kn_ref, cos_ref, sin_ref, q_ref, k_ref, v_ref, *, tm):
    h = _norm_mod(x_ref[...], g_ref[...], mod_ref[0:1, :], mod_ref[1:2, :]).astype(BF16)
    qkv = _dot(h, w_ref[...])
    cosf, sinf = cos_ref[...], sin_ref[...]
    even = lax.broadcasted_iota(jnp.int32, (tm, HEAD_DIM), 1) % 2 == 0
    for hd in range(GQA_HEADS):
        lanes = slice(hd * HEAD_DIM, (hd + 1) * HEAD_DIM)
        qh = _rope(_rms(qkv[:, lanes], qn_ref[...]), cosf, sinf, even)
        q_ref[:, lanes] = (qh * QK_PRESCALE).astype(BF16)
    k0 = GQA_HEADS * HEAD_DIM
    for hd in range(GQA_KV_HEADS):
        lanes = slice(hd * HEAD_DIM, (hd + 1) * HEAD_DIM)
        kh = qkv[:, k0 + hd * HEAD_DIM:k0 + (hd + 1) * HEAD_DIM]
        k_ref[:, lanes] = _rope(_rms(kh, kn_ref[...]), cosf, sinf, even).astype(BF16)
    v_ref[...] = qkv[:, k0 + GQA_KV_HEADS * HEAD_DIM:].astype(BF16)


def _gqa_proj(x, mods_l, g, w_qkv, qn, kn, cos_ext, sin_ext):
    tm = GQA_PROJ_ROWS
    dq, dkv = GQA_HEADS * HEAD_DIM, GQA_KV_HEADS * HEAD_DIM
    n_lat = ROWS_LAT // tm
    rope_blk = lambda t: (jnp.where(t < n_lat, t % (SEQ // tm), SEQ // tm), 0)
    return pl.pallas_call(
        functools.partial(_gqa_proj_kernel, tm=tm),
        grid=(ROWS_ALL // tm,),
        in_specs=[
            pl.BlockSpec((tm, D_MODEL), lambda t: (t, 0)),
            pl.BlockSpec((None, 6, D_MODEL), lambda t: (_mod_row(t, tm), 0, 0)),
            pl.BlockSpec((1, D_MODEL), lambda t: (0, 0)),
            pl.BlockSpec((D_MODEL, dq + 2 * dkv), lambda t: (0, 0)),
            pl.BlockSpec((1, HEAD_DIM), lambda t: (0, 0)),
            pl.BlockSpec((1, HEAD_DIM), lambda t: (0, 0)),
            pl.BlockSpec((tm, HEAD_DIM), rope_blk),
            pl.BlockSpec((tm, HEAD_DIM), rope_blk),
        ],
        out_specs=[
            pl.BlockSpec((tm, dq), lambda t: (t, 0)),
            pl.BlockSpec((tm, dkv), lambda t: (t, 0)),
            pl.BlockSpec((tm, dkv), lambda t: (t, 0)),
        ],
        out_shape=[
            jax.ShapeDtypeStruct((ROWS_ALL, dq), BF16),
            jax.ShapeDtypeStruct((ROWS_ALL, dkv), BF16),
            jax.ShapeDtypeStruct((ROWS_ALL, dkv), BF16),
        ],
        compiler_params=_params("parallel"),
        name="gqa_proj",
    )(x, mods_l, g.reshape(1, D_MODEL), w_qkv, qn.reshape(1, HEAD_DIM), kn.reshape(1, HEAD_DIM), cos_ext, sin_ext)


def _softmax_rows(s):
    m = jnp.max(s, axis=-1, keepdims=True)
    p = jnp.exp2(s - m)
    return p, jnp.sum(p, axis=-1, keepdims=True)


def _stack_heads(ref, n):
    return jnp.concatenate([ref[:, g * HEAD_DIM:(g + 1) * HEAD_DIM] for g in range(n)], axis=0)


def _online_softmax_chunk(s_ref, p_ref, m_ref, a_ref, n_rows, width):
    nslab = width // LANE
    blocks = [slice(i * SOFTMAX_ROWS, (i + 1) * SOFTMAX_ROWS) for i in range(n_rows // SOFTMAX_ROWS)]
    for rows in blocks:
        mc = functools.reduce(jnp.maximum, [s_ref[rows, c * LANE:(c + 1) * LANE] for c in range(nslab)])
        m_prev = m_ref[rows, :]
        m_new = jnp.maximum(m_prev, jnp.max(mc, axis=-1, keepdims=True))
        a_ref[rows, :] = jnp.exp2(m_prev - m_new)
        m_ref[rows, :] = m_new
    for rows in blocks:
        m_new = m_ref[rows, :]
        for c in range(nslab):
            p_ref[rows, c * LANE:(c + 1) * LANE] = jnp.exp2(s_ref[rows, c * LANE:(c + 1) * LANE] - m_new).astype(BF16)


FFN_WEIGHT_SHAPES = ((D_MODEL, FFN_HIDDEN), (D_MODEL, FFN_HIDDEN), (FFN_HIDDEN, D_MODEL))


def _weight_cast_specs(first_layer, n_layers, n_steps, step_id):
    per_layer = n_steps // n_layers
    in_specs, out_specs, out_shapes = [], [], []
    for rows, cols in FFN_WEIGHT_SHAPES:
        blk = (None, rows // per_layer, cols)
        in_specs.append(pl.BlockSpec(
            blk, lambda *g: (first_layer + step_id(*g) // per_layer, step_id(*g) % per_layer, 0)))
        out_specs.append(pl.BlockSpec(blk, lambda *g: (step_id(*g) // per_layer, step_id(*g) % per_layer, 0)))
        out_shapes.append(jax.ShapeDtypeStruct((n_layers, rows, cols), BF16))
    return in_specs, out_specs, out_shapes


def _cast_weight_slabs(src_refs, dst_refs):
    for src, dst in zip(src_refs, dst_refs):
        dst[...] = src[...].astype(BF16)


def _gqa_attn_kernel(q_ref, qc_ref, k_ref, v_ref, kc_ref, vc_ref, w1f_ref, w3f_ref, w2f_ref,
                     ol_ref, oc_ref, w1b_ref, w3b_ref, w2b_ref,
                     s_ref, p_ref, m_ref, a_ref, acc_ref, vaug_ref, *, tq, tk, n_sub):
    _cast_weight_slabs((w1f_ref, w3f_ref, w2f_ref), (w1b_ref, w3b_ref, w2b_ref))
    kc, vc = kc_ref[...], vc_ref[...]
    sub_q = tq // n_sub
    n_rows = GQA_GROUP * sub_q

    @pl.when(pl.program_id(2) == 0)
    def _():
        p, l = _softmax_rows(_dot_nt(_stack_heads(qc_ref, GQA_GROUP), kc))
        oc = _dot(p.astype(BF16), vc) / l
        for g in range(GQA_GROUP):
            oc_ref[:, g * HEAD_DIM:(g + 1) * HEAD_DIM] = oc[g * CTX_LEN:(g + 1) * CTX_LEN].astype(BF16)
        vaug_ref[0:SEQ, 0:HEAD_DIM] = v_ref[...]
        vaug_ref[SEQ:, 0:HEAD_DIM] = vc
        vaug_ref[:, HEAD_DIM:] = jnp.ones((SEQ + CTX_LEN, HEAD_DIM), BF16)

    qs = [_stack_heads(q_ref.at[u * sub_q:(u + 1) * sub_q, :], GQA_GROUP) for u in range(n_sub)]
    m_ref[...] = jnp.full_like(m_ref, MASK_VALUE)
    acc_ref[...] = jnp.zeros_like(acc_ref)

    chunks = [(k_ref, j * tk, j * tk, tk) for j in range(SEQ // tk)] + [(kc_ref, 0, SEQ, CTX_LEN)]
    for n, (kr, off, voff, width) in enumerate(chunks):
        for u in range(n_sub):
            sb, pb = s_ref.at[u, n % 2], p_ref.at[u, n % 2]
            sb[:, :width] = _dot_nt(qs[u], kr[off:off + width, :])
            _online_softmax_chunk(sb, pb, m_ref.at[u], a_ref.at[u], n_rows, width)
            pv = _dot(pb[:, :width], vaug_ref[voff:voff + width, :])
            for half in (slice(0, HEAD_DIM), slice(HEAD_DIM, 2 * HEAD_DIM)):
                acc_ref[u, :, half] = a_ref[u] * acc_ref[u, :, half] + pv[:, half]

    for u in range(n_sub):
        o = acc_ref[u, :, 0:HEAD_DIM] / acc_ref[u, :, HEAD_DIM:]
        for g in range(GQA_GROUP):
            ol_ref[u * sub_q:(u + 1) * sub_q, g * HEAD_DIM:(g + 1) * HEAD_DIM] = (
                o[g * sub_q:(g + 1) * sub_q].astype(BF16))


def _gqa_attn(q, k, v, ffn_weights, first_layer, n_layers):
    tq, tk, n_sub = 512, 1024, 2
    sub_rows = GQA_GROUP * tq // n_sub
    gw = GQA_GROUP * HEAD_DIM
    nq = SEQ // tq
    ctx_blk = ROWS_LAT // CTX_LEN
    w_in, w_out, w_shapes = _weight_cast_specs(first_layer, n_layers, BATCH * GQA_KV_HEADS * nq,
                                               lambda b, h, i: (b * GQA_KV_HEADS + h) * nq + i)
    return pl.pallas_call(
        functools.partial(_gqa_attn_kernel, tq=tq, tk=tk, n_sub=n_sub),
        grid=(BATCH, GQA_KV_HEADS, nq),
        in_specs=[
            pl.BlockSpec((tq, gw), lambda b, h, i: (b * nq + i, h)),
            pl.BlockSpec((CTX_LEN, gw), lambda b, h, i: (ctx_blk + b, h)),
            pl.BlockSpec((SEQ, HEAD_DIM), lambda b, h, i: (b, h)),
            pl.BlockSpec((SEQ, HEAD_DIM), lambda b, h, i: (b, h)),
            pl.BlockSpec((CTX_LEN, HEAD_DIM), lambda b, h, i: (ctx_blk + b, h)),
            pl.BlockSpec((CTX_LEN, HEAD_DIM), lambda b, h, i: (ctx_blk + b, h)),
        ] + w_in,
        out_specs=[
            pl.BlockSpec((tq, gw), lambda b, h, i: (b * nq + i, h)),
            pl.BlockSpec((CTX_LEN, gw), lambda b, h, i: (b, h)),
        ] + w_out,
        out_shape=[
            jax.ShapeDtypeStruct((ROWS_LAT, D_MODEL), BF16),
            jax.ShapeDtypeStruct((ROWS_CTX, D_MODEL), BF16),
        ] + w_shapes,
        scratch_shapes=[
            pltpu.VMEM((n_sub, 2, sub_rows, tk), F32),
            pltpu.VMEM((n_sub, 2, sub_rows, tk), BF16),
            pltpu.VMEM((n_sub, sub_rows, LANE), F32),
            pltpu.VMEM((n_sub, sub_rows, LANE), F32),
            pltpu.VMEM((n_sub, sub_rows, 2 * HEAD_DIM), F32),
            pltpu.VMEM((SEQ + CTX_LEN, 2 * HEAD_DIM), BF16),
        ],
        compiler_params=_params("parallel", "parallel", "arbitrary"),
        name="gqa_attn",
    )(q, q, k, v, k, v, *ffn_weights)


def _nat_proj_kernel(h_ref, w_ref, o_ref):
    prescale = jnp.where(pl.program_id(0) == 0, QK_PRESCALE, 1.0)
    o_ref[...] = (_dot(h_ref[...], w_ref[...]) * prescale).astype(BF16)


def _nat_proj(h, w_qkv):
    tm = 512
    return pl.pallas_call(
        _nat_proj_kernel,
        grid=(3, ROWS_ALL // tm),
        in_specs=[
            pl.BlockSpec((tm, D_MODEL), lambda n, t: (t, 0)),
            pl.BlockSpec((None, D_MODEL, D_MODEL), lambda n, t: (n, 0, 0)),
        ],
        out_specs=pl.BlockSpec((None, tm, D_MODEL), lambda n, t: (n, t, 0)),
        out_shape=jax.ShapeDtypeStruct((3, ROWS_ALL, D_MODEL), BF16),
        compiler_params=_params("parallel", "parallel"),
        name="nat_proj",
    )(h, w_qkv)


NAT_BLOCK_ROWS = 8
NAT_WIN_ROWS = NAT_BLOCK_ROWS + NAT_KH
NAT_Q = NAT_BLOCK_ROWS * GRID_W
NAT_K = NAT_WIN_ROWS * GRID_W
NAT_SLABS = NAT_K // LANE
NAT_PAIR_TILES = 2 * NAT_KH - 2
NAT_TILE_LEFT_MASKED = NAT_PAIR_TILES
NAT_TILE_RIGHT_MASKED = NAT_PAIR_TILES + 1
NAT_TILES = NAT_PAIR_TILES + 2
NAT_BUFFERS = 3


def _nat_row_plan(cls, i):
    half = NAT_KH // 2
    if cls == "first":
        r, ws = i, 0
    elif cls == "mid":
        r, ws = NAT_BLOCK_ROWS + i, NAT_BLOCK_ROWS - half
    else:
        r, ws = GRID_H - NAT_BLOCK_ROWS + i, GRID_H - NAT_WIN_ROWS
    rstart = min(max(r - half, 0), GRID_H - NAT_KH)
    p, delta = r - rstart, rstart - ws
    assert 0 <= delta and delta + NAT_KH <= NAT_WIN_ROWS
    if delta % 2 == 0:
        slabs = [delta // 2 + t for t in range(half)]
        tiles = [2 * t - p + NAT_KH - 1 for t in range(half)]
    else:
        assert p == half
        slabs = [(delta - 1) // 2 + t for t in range(half + 1)]
        tiles = ([NAT_TILE_LEFT_MASKED] + [2 * t - 1 - p + NAT_KH - 1 for t in range(1, half)]
                 + [NAT_TILE_RIGHT_MASKED])
    return slabs, tiles


def _nat_build_tiles(rpbw_ref, t2_ref):
    lane = lax.broadcasted_iota(jnp.int32, (GRID_W, LANE), 1)
    cq = lax.broadcasted_iota(jnp.int32, (GRID_W, LANE), 0)
    kc = lane % GRID_W
    cstart = jnp.clip(cq - NAT_KW // 2, 0, GRID_W - NAT_KW)
    inside = (kc >= cstart) & (kc < cstart + NAT_KW)
    for d in range(NAT_PAIR_TILES):
        w = jnp.broadcast_to(rpbw_ref[d:d + 1, :] * LOG2E, (GRID_W, LANE))
        t = pltpu.roll(w, LANE - (NAT_KW - 1), 1, stride=1, stride_axis=0)
        t2_ref[d] = jnp.where(inside, t, MASK_VALUE)
    half = NAT_KH // 2
    t2_ref[NAT_TILE_LEFT_MASKED] = jnp.where(lane < GRID_W, MASK_VALUE, t2_ref[half - 2])
    t2_ref[NAT_TILE_RIGHT_MASKED] = jnp.where(lane < GRID_W, t2_ref[NAT_KH + half - 2], MASK_VALUE)


def _nat_softmax_block(cls, s_ref, sc_ref, t2_ref, p_ref, pc_ref, m_ref):
    n_ctx_slabs = CTX_LEN // LANE
    subs = []
    for i in range(NAT_BLOCK_ROWS):
        slabs, tiles = _nat_row_plan(cls, i)
        for a in range(NAT_SLABS):
            if a not in slabs:
                p_ref[i * GRID_W:(i + 1) * GRID_W, a * LANE:(a + 1) * LANE] = jnp.zeros((GRID_W, LANE), BF16)
        for sub in range(GRID_W // SOFTMAX_ROWS):
            trow = slice(sub * SOFTMAX_ROWS, (sub + 1) * SOFTMAX_ROWS)
            rows = slice(i * GRID_W + sub * SOFTMAX_ROWS, i * GRID_W + (sub + 1) * SOFTMAX_ROWS)
            subs.append((rows, trow, slabs, tiles))

    def scores(rows, trow, slabs, tiles):
        sv = [s_ref[rows, a * LANE:(a + 1) * LANE] + t2_ref[t, trow, :] for a, t in zip(slabs, tiles)]
        return sv + [sc_ref[rows, c * LANE:(c + 1) * LANE] for c in range(n_ctx_slabs)]

    for rows, trow, slabs, tiles in subs:
        m = jnp.max(functools.reduce(jnp.maximum, scores(rows, trow, slabs, tiles)), axis=-1, keepdims=True)
        m_ref[rows, :] = jnp.broadcast_to(m, (SOFTMAX_ROWS, LANE))
    for rows, trow, slabs, tiles in subs:
        m = m_ref[rows, :]
        pv = [jnp.exp2(s - m) for s in scores(rows, trow, slabs, tiles)]
        for a, pa in zip(slabs, pv):
            p_ref[rows, a * LANE:(a + 1) * LANE] = pa.astype(BF16)
        for c in range(n_ctx_slabs):
            pc_ref[rows, c * LANE:(c + 1) * LANE] = pv[len(slabs) + c].astype(BF16)


def _nat_attn_kernel(q_ref, k_ref, v_ref, qc_ref, kc_ref, vc_ref, rpbw_ref, w1f_ref, w3f_ref, w2f_ref,
                     ol_ref, oc_ref, w1b_ref, w3b_ref, w2b_ref,
                     t2_ref, s_ref, sc_ref, p_ref, pc_ref, m_ref, vaug_ref):
    _cast_weight_slabs((w1f_ref, w3f_ref, w2f_ref), (w1b_ref, w3b_ref, w2b_ref))
    kc, vc = kc_ref[...], vc_ref[...]
    p, l = _softmax_rows(_dot_nt(qc_ref[...], kc))
    oc_ref[...] = (_dot(p.astype(BF16), vc) / l).astype(BF16)

    _nat_build_tiles(rpbw_ref, t2_ref)
    vaug_ref[0:SEQ, 0:HEAD_DIM] = v_ref[...]
    vaug_ref[SEQ:, 0:HEAD_DIM] = vc
    vaug_ref[:, HEAD_DIM:] = jnp.ones((SEQ + CTX_LEN, HEAD_DIM), BF16)

    n_blocks = GRID_H // NAT_BLOCK_ROWS
    half_win = (NAT_KH // 2) * GRID_W
    specs = [("first", 0, 0)]
    specs += [("mid", j * NAT_Q, j * NAT_Q - half_win) for j in range(1, n_blocks - 1)]
    specs += [("last", SEQ - NAT_Q, SEQ - NAT_K)]

    def scores(n):
        _, q_tok, k_tok = specs[n]
        qb = q_ref[q_tok:q_tok + NAT_Q, :]
        s_ref[n % NAT_BUFFERS] = _dot_nt(qb, k_ref[k_tok:k_tok + NAT_K, :])
        sc_ref[n % NAT_BUFFERS] = _dot_nt(qb, kc)

    def finish(n):
        cls, q_tok, k_tok = specs[n]
        b = n % NAT_BUFFERS
        _nat_softmax_block(cls, s_ref.at[b], sc_ref.at[b], t2_ref, p_ref.at[b], pc_ref.at[b], m_ref.at[b])
        o = _dot(p_ref[b], vaug_ref[k_tok:k_tok + NAT_K, :]) + _dot(pc_ref[b], vaug_ref[SEQ:, :])
        ol_ref[q_tok:q_tok + NAT_Q, :] = (o[:, 0:HEAD_DIM] / o[:, HEAD_DIM:]).astype(BF16)

    scores(0)
    for n in range(n_blocks):
        if n + 1 < n_blocks:
            scores(n + 1)
        finish(n)


def _nat_attn(qkv, rpbw, ffn_weights, first_layer, n_layers):
    ctx_blk = ROWS_LAT // CTX_LEN
    lat = lambda n: pl.BlockSpec((None, SEQ, HEAD_DIM), lambda b, h: (n, b, h))
    ctx = lambda n: pl.BlockSpec((None, CTX_LEN, HEAD_DIM), lambda b, h: (n, ctx_blk + b, h))
    w_in, w_out, w_shapes = _weight_cast_specs(first_layer, n_layers, BATCH * NAT_HEADS,
                                               lambda b, h: b * NAT_HEADS + h)
    return pl.pallas_call(
        _nat_attn_kernel,
        grid=(BATCH, NAT_HEADS),
        in_specs=[lat(0), lat(1), lat(2), ctx(0), ctx(1), ctx(2),
                  pl.BlockSpec((None, 2 * NAT_KH, LANE), lambda b, h: (h, 0, 0))] + w_in,
        out_specs=[
            pl.BlockSpec((SEQ, HEAD_DIM), lambda b, h: (b, h)),
            pl.BlockSpec((CTX_LEN, HEAD_DIM), lambda b, h: (b, h)),
        ] + w_out,
        out_shape=[
            jax.ShapeDtypeStruct((ROWS_LAT, D_MODEL), BF16),
            jax.ShapeDtypeStruct((ROWS_CTX, D_MODEL), BF16),
        ] + w_shapes,
        scratch_shapes=[
            pltpu.VMEM((NAT_TILES, GRID_W, LANE), F32),
            pltpu.VMEM((NAT_BUFFERS, NAT_Q, NAT_K), F32),
            pltpu.VMEM((NAT_BUFFERS, NAT_Q, CTX_LEN), F32),
            pltpu.VMEM((NAT_BUFFERS, NAT_Q, NAT_K), BF16),
            pltpu.VMEM((NAT_BUFFERS, NAT_Q, CTX_LEN), BF16),
            pltpu.VMEM((NAT_BUFFERS, NAT_Q, LANE), F32),
            pltpu.VMEM((SEQ + CTX_LEN, 2 * HEAD_DIM), BF16),
        ],
        compiler_params=_params("parallel", "parallel"),
        name="nat_attn",
    )(qkv, qkv, qkv, qkv, qkv, qkv, rpbw, *ffn_weights)


def _nat_rpb_pairs(rpb):
    padded = jnp.pad(rpb, ((0, 0), (0, 2), (0, GRID_W - (2 * NAT_KW - 1))))
    return jnp.concatenate([padded[:, :2 * NAT_KH], padded[:, 1:2 * NAT_KH + 1]], axis=-1)


def _rope_tables(tm):
    t = jnp.arange(SEQ)
    row = (t // GRID_W).astype(F32)
    col = (t % GRID_W).astype(F32)
    inv = ROPE_THETA ** (-jnp.arange(0, ROPE_AXIS_DIM, 2, dtype=F32) / ROPE_AXIS_DIM)
    ang = jnp.concatenate([row[:, None] * inv, col[:, None] * inv], axis=-1)
    cosf = jnp.repeat(jnp.cos(ang), 2, axis=-1)
    sign = jnp.tile(jnp.array([-1.0, 1.0], F32), HEAD_DIM // 2)
    sinf = jnp.repeat(jnp.sin(ang), 2, axis=-1) * sign
    cos_ext = jnp.concatenate([cosf, jnp.ones((tm, HEAD_DIM), F32)], axis=0)
    sin_ext = jnp.concatenate([sinf, jnp.zeros((tm, HEAD_DIM), F32)], axis=0)
    return cos_ext, sin_ext


def kernel(x, c, ctx, c_ctx, ada_w, ada_b, norm_g, ffn_w1, ffn_w3, ffn_w2, pool_w, pool_ls, gqa_wq, gqa_wk, gqa_wv, gqa_wo, gqa_qn, gqa_kn, nat_wq, nat_wk, nat_wv, nat_wo, nat_rpb, final_g):
    c8 =jnp.concatenate([c, c_ctx[None, :], jnp.zeros((8 - BATCH - 1, D_MODEL), F32)], axis=0)
    mods = _ada_all(c8, ada_w, ada_b)[:, :BATCH + 1].reshape(DEPTH, BATCH + 1, 6, D_MODEL)
    pool_w_all = pool_w.astype(BF16)
    ffn_f32 = (ffn_w1, ffn_w3, ffn_w2)
    ffn_bf16 = {}
    gqa_cast_layers, nat_cast_layers = (1, 2), (3, 1)
    attn_layers = [i for i in range(1, DEPTH) if i % N_MIXERS != 0]
    attn_bf16 = {}

    xa = None
    for i in range(DEPTH):
        kind, j = i % N_MIXERS, i // N_MIXERS
        last = i == DEPTH - 1
        n_rows = ROWS_LAT if last else ROWS_ALL
        g1, g2 = norm_g[i, 0], norm_g[i, 1]
        if kind == 0:
            x_lat, x_ctx = (x.reshape(ROWS_LAT, D_MODEL), ctx.reshape(ROWS_CTX, D_MODEL)) if i == 0 else (xa, xa)
            if i in ffn_bf16:
                xa, h2 = _pool_layer(x_lat, x_ctx, mods[i], g1, g2, pool_w_all, j, pool_ls[j], n_rows)
            else:
                xa, h2, *cast = _pool_layer(x_lat, x_ctx, mods[i], g1, g2, pool_w_all, j, pool_ls[j], n_rows,
                                            cast_ffn_layer=i, ffn_weights=ffn_f32)
                ffn_bf16[i] = (tuple(cast), 0)
        else:
            w_qkv, wo = attn_bf16[i]
            if kind == 1:
                cos_ext, sin_ext = _rope_tables(GQA_PROJ_ROWS)
                q, k, v = _gqa_proj(xa, mods[i], g1, w_qkv, gqa_qn[j], gqa_kn[j], cos_ext, sin_ext)
                first, count = gqa_cast_layers
                o_lat, o_ctx, *cast = _gqa_attn(q, k, v, ffn_f32, first, count)
            else:
                qkv = _nat_proj(h1, w_qkv)
                first, count = nat_cast_layers
                o_lat, o_ctx, *cast = _nat_attn(qkv, _nat_rpb_pairs(nat_rpb[j]), ffn_f32, first, count)
            for n in range(count):
                ffn_bf16[first + n] = (tuple(cast), n)
            xa, h2 = _attn_out_layer(o_lat, o_ctx, xa, mods[i], wo, g2)
        (w1_all, w3_all, w2_all), idx = ffn_bf16[i]
        next_norm = (norm_g[i + 1, 0], mods[i + 1]) if not last and (i + 1) % N_MIXERS == 2 else None
        side_casts = []
        if i == 0:
            for a in attn_layers:
                ja = a // N_MIXERS
                if a % N_MIXERS == 1:
                    side_casts += [("concat", ja, [gqa_wq, gqa_wk, gqa_wv]), ("concat", ja, [gqa_wo])]
                else:
                    side_casts += [("stack", ja, [nat_wq, nat_wk, nat_wv]), ("concat", ja, [nat_wo])]
        outs = _ffn_layer(xa, h2, mods[i], w1_all, w3_all, w2_all, idx, final_g, n_rows, last,
                          next_norm=next_norm, side_casts=side_casts)
        if next_norm is None and not side_casts:
            xa = outs
        else:
            xa, rest = outs[0], list(outs[1:])
            if next_norm is not None:
                h1 = rest.pop(0)
            for n, a in enumerate(attn_layers if side_casts else ()):
                attn_bf16[a] = (rest[2 * n], rest[2 * n + 1])
    return xa.reshape(BATCH, SEQ, D_MODEL)
```

```python
import functools

import jax
import jax.numpy as jnp
import numpy as np
from jax import lax
from jax.experimental import pallas as pl
from jax.experimental.pallas import tpu as pltpu

D_MODEL = 2048
BATCH = 2
SEQ = 4096
DEPTH = 4
GRID_W = 64
CTX_LEN = 256
N_MIXERS = 3
POOL_GROUPS = 4
POOL_WINDOWS = (2, 4, 8, 16)
POOL_GC = D_MODEL // POOL_GROUPS
HEAD_DIM = 128
GQA_HEADS = D_MODEL // HEAD_DIM
GQA_KV_HEADS = 4
GQA_GROUP = GQA_HEADS // GQA_KV_HEADS
ROPE_THETA = 10000.0
ROPE_AXIS_DIM = HEAD_DIM // 2
NAT_HEADS = D_MODEL // HEAD_DIM
NAT_KH = 8
NAT_KW = 16
FFN_HIDDEN = -(-8 * D_MODEL // (3 * 256)) * 256
NORM_EPS = 1e-6

ROWS_LAT = BATCH * SEQ
ROWS_CTX = BATCH * CTX_LEN
ROWS_ALL = ROWS_LAT + ROWS_CTX
GRID_H = SEQ // GRID_W
ATTN_SCALE = HEAD_DIM ** -0.5
LOG2E = 1.4426950408889634
QK_PRESCALE = ATTN_SCALE * LOG2E
MASK_VALUE = -1e30
LANE = 128
POOL_HALO = 8
SOFTMAX_ROWS = 32
GQA_PROJ_ROWS = 256

VMEM_LIMIT_BYTES = 56 * 1024 * 1024

F32 = jnp.float32
BF16 = jnp.bfloat16


def _params(*sem):
    return pltpu.CompilerParams(dimension_semantics=sem, vmem_limit_bytes=VMEM_LIMIT_BYTES)


def _dot(a, b):
    return jnp.dot(a, b, preferred_element_type=F32)


def _dot_nt(a, b):
    return lax.dot_general(a, b, (((1,), (1,)), ((), ())), preferred_element_type=F32)


def _silu(a):
    return a * (1.0 / (1.0 + jnp.exp(-a)))


def _rms(x, g):
    return x * lax.rsqrt(jnp.mean(x * x, axis=-1, keepdims=True) + NORM_EPS) * g


def _norm_mod(x, g, shift, scale):
    return x * lax.rsqrt(jnp.mean(x * x, axis=-1, keepdims=True) + NORM_EPS) * (g * (1.0 + scale)) + shift


def _mod_row(t, tm):
    return jnp.where(t < ROWS_LAT // tm, t // (SEQ // tm), BATCH)


def _ada_kernel(c_ref, w_ref, b_ref, o_ref):
    s = _silu(c_ref[...])
    o_ref[...] = _dot(s.astype(BF16), w_ref[...].astype(BF16)) + b_ref[...]


def _ada_all(c8, ada_w, ada_b):
    tn = 1024
    return pl.pallas_call(
        _ada_kernel,
        grid=(DEPTH, 6 * D_MODEL // tn),
        in_specs=[
            pl.BlockSpec((8, D_MODEL), lambda l, n: (0, 0)),
            pl.BlockSpec((None, D_MODEL, tn), lambda l, n: (l, 0, n)),
            pl.BlockSpec((None, 1, tn), lambda l, n: (l, 0, n)),
        ],
        out_specs=pl.BlockSpec((None, 8, tn), lambda l, n: (l, 0, n)),
        out_shape=jax.ShapeDtypeStruct((DEPTH, 8, 6 * D_MODEL), F32),
        compiler_params=_params("parallel", "parallel"),
        name="ada",
    )(c8, ada_w, ada_b.reshape(DEPTH, 1, 6 * D_MODEL))


def _pool_bands(tm):
    t = np.arange(tm)[:, None]
    j = np.arange(tm + 2 * POOL_HALO)[None, :]
    pos = np.where(j < tm, j, np.where(j < tm + POOL_HALO, j - tm - POOL_HALO, j - POOL_HALO))
    bands = [(pos >= t - w // 2) & (pos < t + w - w // 2) for w in POOL_WINDOWS]
    return jnp.asarray(np.stack(bands).astype(np.float32), BF16)


def _split_bf16(v):
    hi = v.astype(BF16)
    return hi, (v - hi.astype(F32)).astype(BF16)


def _pool_kernel(xl_ref, xc_ref, xp_ref, xn_ref, mod_ref, g1_ref, g2_ref, band_ref, w_ref, ls_ref, *rest,
                 tm, cast_ffn):
    if cast_ffn:
        w1f_ref, w3f_ref, w2f_ref, o_ref, h2_ref, w1b_ref, w3b_ref, w2b_ref, hi_ref, lo_ref = rest
        _cast_weight_slabs((w1f_ref, w3f_ref, w2f_ref), (w1b_ref, w3b_ref, w2b_ref))
    else:
        o_ref, h2_ref, hi_ref, lo_ref = rest
    t = pl.program_id(0)
    shift, scale, gate = mod_ref[0:1, :], mod_ref[1:2, :], mod_ref[2:3, :]
    g = g1_ref[...]

    n_lat_tiles = ROWS_LAT // tm
    is_lat = t < n_lat_tiles
    tiles_in_seq = jnp.where(is_lat, SEQ // tm, CTX_LEN // tm)
    pos_tile = jnp.where(is_lat, t % (SEQ // tm), (t - n_lat_tiles) % (CTX_LEN // tm))
    first = pos_tile == 0
    last = pos_tile == tiles_in_seq - 1

    x = jnp.where(is_lat, xl_ref[...], xc_ref[...])
    h = _norm_mod(x, g, shift, scale)
    halo = jnp.concatenate([jnp.where(first, 0.0, _norm_mod(xp_ref[...], g, shift, scale)),
                            jnp.where(last, 0.0, _norm_mod(xn_ref[...], g, shift, scale))], axis=0)
    hi_ref[0:tm, :], lo_ref[0:tm, :] = _split_bf16(h)
    hi_ref[tm:, :], lo_ref[tm:, :] = _split_bf16(halo)

    pos = pos_tile * tm + lax.broadcasted_iota(jnp.int32, (tm, 1), 0)
    n = tiles_in_seq * tm
    for k in range(POOL_GROUPS):
        w = POOL_WINDOWS[k]
        lanes = slice(k * POOL_GC, (k + 1) * POOL_GC)
        wsum = _dot(band_ref[k], hi_ref[:, lanes]) + _dot(band_ref[k], lo_ref[:, lanes])
        cnt = jnp.minimum(pos + (w - w // 2), n) - jnp.maximum(pos - w // 2, 0)
        pooled = wsum / cnt.astype(F32) - h[:, lanes]
        y = _dot(pooled.astype(BF16), w_ref[k]) * ls_ref[:, lanes]
        o_ref[:, lanes] = x[:, lanes] + gate[:, lanes] * y
    h2_ref[...] = _norm_mod(o_ref[...], g2_ref[...], mod_ref[3:4, :], mod_ref[4:5, :]).astype(BF16)


def _pool_layer(x_lat, x_ctx, mods_l, g1, g2, w_all, j, ls, n_rows, cast_ffn_layer=None, ffn_weights=()):
    tm = 256
    n_lat = ROWS_LAT // tm
    nb8 = x_lat.shape[0] // POOL_HALO
    ctx_blk0 = (x_ctx.shape[0] - ROWS_CTX) // tm
    halo_blk = lambda b: jnp.clip(b, 0, nb8 - 1)
    row = lambda v: v.reshape(1, D_MODEL)
    cast_ffn = cast_ffn_layer is not None
    w_in, w_out, w_shapes = (_weight_cast_specs(cast_ffn_layer, 1, n_lat, lambda t: jnp.minimum(t, n_lat - 1))
                             if cast_ffn else ([], [], []))
    return pl.pallas_call(
        functools.partial(_pool_kernel, tm=tm, cast_ffn=cast_ffn),
        grid=(n_rows // tm,),
        in_specs=[
            pl.BlockSpec((tm, D_MODEL), lambda t: (jnp.minimum(t, n_lat - 1), 0)),
            pl.BlockSpec((tm, D_MODEL), lambda t: (ctx_blk0 + jnp.maximum(t - n_lat, 0), 0)),
            pl.BlockSpec((POOL_HALO, D_MODEL), lambda t: (halo_blk(t * (tm // POOL_HALO) - 1), 0)),
            pl.BlockSpec((POOL_HALO, D_MODEL), lambda t: (halo_blk((t + 1) * (tm // POOL_HALO)), 0)),
            pl.BlockSpec((None, 6, D_MODEL), lambda t: (_mod_row(t, tm), 0, 0)),
            pl.BlockSpec((1, D_MODEL), lambda t: (0, 0)),
            pl.BlockSpec((1, D_MODEL), lambda t: (0, 0)),
            pl.BlockSpec((POOL_GROUPS, tm, tm + 2 * POOL_HALO), lambda t: (0, 0, 0)),
            pl.BlockSpec((None, POOL_GROUPS, POOL_GC, POOL_GC), lambda t: (j, 0, 0, 0)),
            pl.BlockSpec((1, D_MODEL), lambda t: (0, 0)),
        ] + w_in,
        out_specs=[pl.BlockSpec((tm, D_MODEL), lambda t: (t, 0)),
                   pl.BlockSpec((tm, D_MODEL), lambda t: (t, 0))] + w_out,
        out_shape=[jax.ShapeDtypeStruct((n_rows, D_MODEL), F32),
                   jax.ShapeDtypeStruct((n_rows, D_MODEL), BF16)] + w_shapes,
        scratch_shapes=[pltpu.VMEM((tm + 2 * POOL_HALO, D_MODEL), BF16),
                        pltpu.VMEM((tm + 2 * POOL_HALO, D_MODEL), BF16)],
        compiler_params=_params("parallel"),
        name="pool_layer",
    )(x_lat, x_ctx, x_lat, x_lat, mods_l, row(g1), row(g2), _pool_bands(tm), w_all, row(ls), *ffn_weights)


ATTN_CAST_ROWS = 16


def _ffn_kernel(*refs, nf, final, emit_next_h, cast_plan):
    it = iter(refs)
    x_ref, h_ref, mod_ref, w1_ref, w3_ref, w2_ref, fg_ref = (next(it) for _ in range(7))
    if emit_next_h:
        gn_ref, modn_ref = next(it), next(it)
    src_refs = [[next(it) for _ in cols] for _, cols in cast_plan]
    o_ref = next(it)
    if emit_next_h:
        hn_ref = next(it)
    dst_refs = [next(it) for _ in cast_plan]
    acc_ref = next(it)
    f = pl.program_id(1)

    for (kind, cols), srcs, dst in zip(cast_plan, src_refs, dst_refs):
        off = 0
        for j, (c, src) in enumerate(zip(cols, srcs)):
            if kind == "stack":
                dst[j] = src[...].astype(BF16)
            else:
                dst[:, off:off + c] = src[...].astype(BF16)
                off += c

    @pl.when(f == 0)
    def _():
        acc_ref[...] = jnp.zeros_like(acc_ref)

    h = h_ref[...]
    a = _dot(h, w1_ref[...])
    b = _dot(h, w3_ref[...])
    acc_ref[...] += _dot((_silu(a) * b).astype(BF16), w2_ref[...])

    @pl.when(f == nf - 1)
    def _():
        y = x_ref[...] + mod_ref[5:6, :] * acc_ref[...]
        if final:
            y = _rms(y, fg_ref[...])
        o_ref[...] = y
        if emit_next_h:
            hn_ref[...] = _norm_mod(y, gn_ref[...], modn_ref[0:1, :], modn_ref[1:2, :]).astype(BF16)


def _ffn_layer(x, h, mods_l, w1_all, w3_all, w2_all, layer, final_g, n_rows, final, next_norm=None, side_casts=()):
    tm, tf = 512, 512
    nf = FFN_HIDDEN // tf
    emit_next_h = next_norm is not None
    n_slabs = D_MODEL // ATTN_CAST_ROWS
    assert not side_casts or (n_rows // tm) * nf >= n_slabs
    slab = lambda t, f: jnp.minimum(t * nf + f, n_slabs - 1)
    tile = pl.BlockSpec((tm, D_MODEL), lambda t, f: (t, 0))
    mod_spec = pl.BlockSpec((None, 6, D_MODEL), lambda t, f: (_mod_row(t, tm), 0, 0))
    row_spec = pl.BlockSpec((1, D_MODEL), lambda t, f: (0, 0))
    in_specs = [
        tile, tile, mod_spec,
        pl.BlockSpec((None, D_MODEL, tf), lambda t, f: (layer, 0, f)),
        pl.BlockSpec((None, D_MODEL, tf), lambda t, f: (layer, 0, f)),
        pl.BlockSpec((None, tf, D_MODEL), lambda t, f: (layer, f, 0)),
        row_spec,
    ]
    args = [x, h, mods_l, w1_all, w3_all, w2_all, final_g.reshape(1, D_MODEL)]
    out_specs, out_shape = [tile], [jax.ShapeDtypeStruct((n_rows, D_MODEL), F32)]
    if emit_next_h:
        in_specs += [row_spec, mod_spec]
        args += [next_norm[0].reshape(1, D_MODEL), next_norm[1]]
        out_specs.append(tile)
        out_shape.append(jax.ShapeDtypeStruct((n_rows, D_MODEL), BF16))
    cast_plan = []
    for kind, idx, weights in side_casts:
        cols = tuple(w.shape[-1] for w in weights)
        cast_plan.append((kind, cols))
        for w, c in zip(weights, cols):
            in_specs.append(pl.BlockSpec((None, ATTN_CAST_ROWS, c), lambda t, f, idx=idx: (idx, slab(t, f), 0)))
            args.append(w)
        if kind == "stack":
            out_specs.append(pl.BlockSpec((len(cols), ATTN_CAST_ROWS, cols[0]), lambda t, f: (0, slab(t, f), 0)))
            out_shape.append(jax.ShapeDtypeStruct((len(cols), D_MODEL, cols[0]), BF16))
        else:
            out_specs.append(pl.BlockSpec((ATTN_CAST_ROWS, sum(cols)), lambda t, f: (slab(t, f), 0)))
            out_shape.append(jax.ShapeDtypeStruct((D_MODEL, sum(cols)), BF16))
    outs = pl.pallas_call(
        functools.partial(_ffn_kernel, nf=nf, final=final, emit_next_h=emit_next_h, cast_plan=tuple(cast_plan)),
        grid=(n_rows // tm, nf),
        in_specs=in_specs,
        out_specs=out_specs,
        out_shape=out_shape,
        scratch_shapes=[pltpu.VMEM((tm, D_MODEL), F32)],
        compiler_params=_params("parallel", "arbitrary"),
        name="ffn_layer",
    )(*args)
    return outs[0] if len(outs) == 1 else outs


def _attn_out_kernel(ol_ref, oc_ref, x_ref, mod_ref, wo_ref, g2_ref, o_ref, h2_ref, *, tm):
    t = pl.program_id(0)
    o = jnp.where(t < ROWS_LAT // tm, ol_ref[...], oc_ref[...])
    y = x_ref[...] + mod_ref[2:3, :] * _dot(o, wo_ref[...])
    o_ref[...] = y
    h2_ref[...] = _norm_mod(y, g2_ref[...], mod_ref[3:4, :], mod_ref[4:5, :]).astype(BF16)


def _attn_out_layer(o_lat, o_ctx, x, mods_l, wo, g2):
    tm = 512
    n_lat = ROWS_LAT // tm
    return pl.pallas_call(
        functools.partial(_attn_out_kernel, tm=tm),
        grid=(ROWS_ALL // tm,),
        in_specs=[
            pl.BlockSpec((tm, D_MODEL), lambda t: (jnp.minimum(t, n_lat - 1), 0)),
            pl.BlockSpec((tm, D_MODEL), lambda t: (jnp.maximum(t - n_lat, 0), 0)),
            pl.BlockSpec((tm, D_MODEL), lambda t: (t, 0)),
            pl.BlockSpec((None, 6, D_MODEL), lambda t: (_mod_row(t, tm), 0, 0)),
            pl.BlockSpec((D_MODEL, D_MODEL), lambda t: (0, 0)),
            pl.BlockSpec((1, D_MODEL), lambda t: (0, 0)),
        ],
        out_specs=[pl.BlockSpec((tm, D_MODEL), lambda t: (t, 0)), pl.BlockSpec((tm, D_MODEL), lambda t: (t, 0))],
        out_shape=[jax.ShapeDtypeStruct((ROWS_ALL, D_MODEL), F32), jax.ShapeDtypeStruct((ROWS_ALL, D_MODEL), BF16)],
        compiler_params=_params("parallel"),
        name="attn_out",
    )(o_lat, o_ctx, x, mods_l, wo, g2.reshape(1, D_MODEL))


def _rope(xh, cosf, sinf, even):
    swapped = jnp.where(even, pltpu.roll(xh, HEAD_DIM - 1, 1), pltpu.roll(xh, 1, 1))
    return xh * cosf + swapped * sinf


def _gqa_proj_kernel(x_ref, mod_ref, g_ref, w_ref, qn_ref, kn_ref, cos_ref, sin_ref, q_ref, k_ref, v_ref, *, tm):
    h = _norm_mod(x_ref[...], g_ref[...], mod_ref[0:1, :], mod_ref[1:2, :]).astype(BF16)
    qkv = _dot(h, w_ref[...])
    cosf, sinf = cos_ref[...], sin_ref[...]
    even = lax.broadcasted_iota(jnp.int32, (tm, HEAD_DIM), 1) % 2 == 0
    for hd in range(GQA_HEADS):
        lanes = slice(hd * HEAD_DIM, (hd + 1) * HEAD_DIM)
        qh = _rope(_rms(qkv[:, lanes], qn_ref[...]), cosf, sinf, even)
        q_ref[:, lanes] = (qh * QK_PRESCALE).astype(BF16)
    k0 = GQA_HEADS * HEAD_DIM
    for hd in range(GQA_KV_HEADS):
        lanes = slice(hd * HEAD_DIM, (hd + 1) * HEAD_DIM)
        kh = qkv[:, k0 + hd * HEAD_DIM:k0 + (hd + 1) * HEAD_DIM]
        k_ref[:, lanes] = _rope(_rms(kh, kn_ref[...]), cosf, sinf, even).astype(BF16)
    v_ref[...] = qkv[:, k0 + GQA_KV_HEADS * HEAD_DIM:].astype(BF16)


def _gqa_proj(x, mods_l, g, w_qkv, qn, kn, cos_ext, sin_ext):
    tm = GQA_PROJ_ROWS
    dq, dkv = GQA_HEADS * HEAD_DIM, GQA_KV_HEADS * HEAD_DIM
    n_lat = ROWS_LAT // tm
    rope_blk = lambda t: (jnp.where(t < n_lat, t % (SEQ // tm), SEQ // tm), 0)
    return pl.pallas_call(
        functools.partial(_gqa_proj_kernel, tm=tm),
        grid=(ROWS_ALL // tm,),
        in_specs=[
            pl.BlockSpec((tm, D_MODEL), lambda t: (t, 0)),
            pl.BlockSpec((None, 6, D_MODEL), lambda t: (_mod_row(t, tm), 0, 0)),
            pl.BlockSpec((1, D_MODEL), lambda t: (0, 0)),
            pl.BlockSpec((D_MODEL, dq + 2 * dkv), lambda t: (0, 0)),
            pl.BlockSpec((1, HEAD_DIM), lambda t: (0, 0)),
            pl.BlockSpec((1, HEAD_DIM), lambda t: (0, 0)),
            pl.BlockSpec((tm, HEAD_DIM), rope_blk),
            pl.BlockSpec((tm, HEAD_DIM), rope_blk),
        ],
        out_specs=[
            pl.BlockSpec((tm, dq), lambda t: (t, 0)),
            pl.BlockSpec((tm, dkv), lambda t: (t, 0)),
            pl.BlockSpec((tm, dkv), lambda t: (t, 0)),
        ],
        out_shape=[
            jax.ShapeDtypeStruct((ROWS_ALL, dq), BF16),
            jax.ShapeDtypeStruct((ROWS_ALL, dkv), BF16),
            jax.ShapeDtypeStruct((ROWS_ALL, dkv), BF16),
        ],
        compiler_params=_params("parallel"),
        name="gqa_proj",
    )(x, mods_l, g.reshape(1, D_MODEL), w_qkv, qn.reshape(1, HEAD_DIM), kn.reshape(1, HEAD_DIM), cos_ext, sin_ext)


def _softmax_rows(s):
    m = jnp.max(s, axis=-1, keepdims=True)
    p = jnp.exp2(s - m)
    return p, jnp.sum(p, axis=-1, keepdims=True)


def _stack_heads(ref, n):
    return jnp.concatenate([ref[:, g * HEAD_DIM:(g + 1) * HEAD_DIM] for g in range(n)], axis=0)


def _online_softmax_chunk(s_ref, p_ref, m_ref, a_ref, n_rows, width):
    nslab = width // LANE
    blocks = [slice(i * SOFTMAX_ROWS, (i + 1) * SOFTMAX_ROWS) for i in range(n_rows // SOFTMAX_ROWS)]
    for rows in blocks:
        mc = functools.reduce(jnp.maximum, [s_ref[rows, c * LANE:(c + 1) * LANE] for c in range(nslab)])
        m_prev = m_ref[rows, :]
        m_new = jnp.maximum(m_prev, jnp.max(mc, axis=-1, keepdims=True))
        a_ref[rows, :] = jnp.exp2(m_prev - m_new)
        m_ref[rows, :] = m_new
    for rows in blocks:
        m_new = m_ref[rows, :]
        for c in range(nslab):
            p_ref[rows, c * LANE:(c + 1) * LANE] = jnp.exp2(s_ref[rows, c * LANE:(c + 1) * LANE] - m_new).astype(BF16)


FFN_WEIGHT_SHAPES = ((D_MODEL, FFN_HIDDEN), (D_MODEL, FFN_HIDDEN), (FFN_HIDDEN, D_MODEL))


def _weight_cast_specs(first_layer, n_layers, n_steps, step_id):
    per_layer = n_steps // n_layers
    in_specs, out_specs, out_shapes = [], [], []
    for rows, cols in FFN_WEIGHT_SHAPES:
        blk = (None, rows // per_layer, cols)
        in_specs.append(pl.BlockSpec(
            blk, lambda *g: (first_layer + step_id(*g) // per_layer, step_id(*g) % per_layer, 0)))
        out_specs.append(pl.BlockSpec(blk, lambda *g: (step_id(*g) // per_layer, step_id(*g) % per_layer, 0)))
        out_shapes.append(jax.ShapeDtypeStruct((n_layers, rows, cols), BF16))
    return in_specs, out_specs, out_shapes


def _cast_weight_slabs(src_refs, dst_refs):
    for src, dst in zip(src_refs, dst_refs):
        dst[...] = src[...].astype(BF16)


def _gqa_attn_kernel(q_ref, qc_ref, k_ref, v_ref, kc_ref, vc_ref, w1f_ref, w3f_ref, w2f_ref,
                     ol_ref, oc_ref, w1b_ref, w3b_ref, w2b_ref,
                     s_ref, p_ref, m_ref, a_ref, acc_ref, vaug_ref, *, tq, tk, n_sub):
    _cast_weight_slabs((w1f_ref, w3f_ref, w2f_ref), (w1b_ref, w3b_ref, w2b_ref))
    kc, vc = kc_ref[...], vc_ref[...]
    sub_q = tq // n_sub
    n_rows = GQA_GROUP * sub_q

    @pl.when(pl.program_id(2) == 0)
    def _():
        p, l = _softmax_rows(_dot_nt(_stack_heads(qc_ref, GQA_GROUP), kc))
        oc = _dot(p.astype(BF16), vc) / l
        for g in range(GQA_GROUP):
            oc_ref[:, g * HEAD_DIM:(g + 1) * HEAD_DIM] = oc[g * CTX_LEN:(g + 1) * CTX_LEN].astype(BF16)
        vaug_ref[0:SEQ, 0:HEAD_DIM] = v_ref[...]
        vaug_ref[SEQ:, 0:HEAD_DIM] = vc
        vaug_ref[:, HEAD_DIM:] = jnp.ones((SEQ + CTX_LEN, HEAD_DIM), BF16)

    qs = [_stack_heads(q_ref.at[u * sub_q:(u + 1) * sub_q, :], GQA_GROUP) for u in range(n_sub)]
    m_ref[...] = jnp.full_like(m_ref, MASK_VALUE)
    acc_ref[...] = jnp.zeros_like(acc_ref)

    chunks = [(k_ref, j * tk, j * tk, tk) for j in range(SEQ // tk)] + [(kc_ref, 0, SEQ, CTX_LEN)]
    for n, (kr, off, voff, width) in enumerate(chunks):
        for u in range(n_sub):
            sb, pb = s_ref.at[u, n % 2], p_ref.at[u, n % 2]
            sb[:, :width] = _dot_nt(qs[u], kr[off:off + width, :])
            _online_softmax_chunk(sb, pb, m_ref.at[u], a_ref.at[u], n_rows, width)
            pv = _dot(pb[:, :width], vaug_ref[voff:voff + width, :])
            for half in (slice(0, HEAD_DIM), slice(HEAD_DIM, 2 * HEAD_DIM)):
                acc_ref[u, :, half] = a_ref[u] * acc_ref[u, :, half] + pv[:, half]

    for u in range(n_sub):
        o = acc_ref[u, :, 0:HEAD_DIM] / acc_ref[u, :, HEAD_DIM:]
        for g in range(GQA_GROUP):
            ol_ref[u * sub_q:(u + 1) * sub_q, g * HEAD_DIM:(g + 1) * HEAD_DIM] = (
                o[g * sub_q:(g + 1) * sub_q].astype(BF16))


def _gqa_attn(q, k, v, ffn_weights, first_layer, n_layers):
    tq, tk, n_sub = 512, 1024, 2
    sub_rows = GQA_GROUP * tq // n_sub
    gw = GQA_GROUP * HEAD_DIM
    nq = SEQ // tq
    ctx_blk = ROWS_LAT // CTX_LEN
    w_in, w_out, w_shapes = _weight_cast_specs(first_layer, n_layers, BATCH * GQA_KV_HEADS * nq,
                                               lambda b, h, i: (b * GQA_KV_HEADS + h) * nq + i)
    return pl.pallas_call(
        functools.partial(_gqa_attn_kernel, tq=tq, tk=tk, n_sub=n_sub),
        grid=(BATCH, GQA_KV_HEADS, nq),
        in_specs=[
            pl.BlockSpec((tq, gw), lambda b, h, i: (b * nq + i, h)),
            pl.BlockSpec((CTX_LEN, gw), lambda b, h, i: (ctx_blk + b, h)),
            pl.BlockSpec((SEQ, HEAD_DIM), lambda b, h, i: (b, h)),
            pl.BlockSpec((SEQ, HEAD_DIM), lambda b, h, i: (b, h)),
            pl.BlockSpec((CTX_LEN, HEAD_DIM), lambda b, h, i: (ctx_blk + b, h)),
            pl.BlockSpec((CTX_LEN, HEAD_DIM), lambda b, h, i: (ctx_blk + b, h)),
        ] + w_in,
        out_specs=[
            pl.BlockSpec((tq, gw), lambda b, h, i: (b * nq + i, h)),
            pl.BlockSpec((CTX_LEN, gw), lambda b, h, i: (b, h)),
        ] + w_out,
        out_shape=[
            jax.ShapeDtypeStruct((ROWS_LAT, D_MODEL), BF16),
            jax.ShapeDtypeStruct((ROWS_CTX, D_MODEL), BF16),
        ] + w_shapes,
        scratch_shapes=[
            pltpu.VMEM((n_sub, 2, sub_rows, tk), F32),
            pltpu.VMEM((n_sub, 2, sub_rows, tk), BF16),
            pltpu.VMEM((n_sub, sub_rows, LANE), F32),
            pltpu.VMEM((n_sub, sub_rows, LANE), F32),
            pltpu.VMEM((n_sub, sub_rows, 2 * HEAD_DIM), F32),
            pltpu.VMEM((SEQ + CTX_LEN, 2 * HEAD_DIM), BF16),
        ],
        compiler_params=_params("parallel", "parallel", "arbitrary"),
        name="gqa_attn",
    )(q, q, k, v, k, v, *ffn_weights)


def _nat_proj_kernel(x_ref, mod_ref, g_ref, w_ref, o_ref):
    h = _norm_mod(x_ref[...], g_ref[...], mod_ref[0:1, :], mod_ref[1:2, :]).astype(BF16)
    o_ref[0] = (_dot(h, w_ref[0]) * QK_PRESCALE).astype(BF16)
    o_ref[1] = _dot(h, w_ref[1]).astype(BF16)
    o_ref[2] = _dot(h, w_ref[2]).astype(BF16)


def _nat_proj(x, mods_l, g, w_qkv):
    tm = 512
    return pl.pallas_call(
        _nat_proj_kernel,
        grid=(ROWS_ALL // tm,),
        in_specs=[
            pl.BlockSpec((tm, D_MODEL), lambda t: (t, 0)),
            pl.BlockSpec((None, 6, D_MODEL), lambda t: (_mod_row(t, tm), 0, 0)),
            pl.BlockSpec((1, D_MODEL), lambda t: (0, 0)),
            pl.BlockSpec((3, D_MODEL, D_MODEL), lambda t: (0, 0, 0), pipeline_mode=pl.Buffered(1)),
        ],
        out_specs=pl.BlockSpec((3, tm, D_MODEL), lambda t: (0, t, 0)),
        out_shape=jax.ShapeDtypeStruct((3, ROWS_ALL, D_MODEL), BF16),
        compiler_params=_params("parallel"),
        name="nat_proj",
    )(x, mods_l, g.reshape(1, D_MODEL), w_qkv)


NAT_BLOCK_ROWS = 8
NAT_WIN_ROWS = NAT_BLOCK_ROWS + NAT_KH
NAT_Q = NAT_BLOCK_ROWS * GRID_W
NAT_K = NAT_WIN_ROWS * GRID_W
NAT_SLABS = NAT_K // LANE
NAT_PAIR_TILES = 2 * NAT_KH - 2
NAT_TILE_LEFT_MASKED = NAT_PAIR_TILES
NAT_TILE_RIGHT_MASKED = NAT_PAIR_TILES + 1
NAT_TILES = NAT_PAIR_TILES + 2
NAT_BUFFERS = 3


def _nat_row_plan(cls, i):
    half = NAT_KH // 2
    if cls == "first":
        r, ws = i, 0
    elif cls == "mid":
        r, ws = NAT_BLOCK_ROWS + i, NAT_BLOCK_ROWS - half
    else:
        r, ws = GRID_H - NAT_BLOCK_ROWS + i, GRID_H - NAT_WIN_ROWS
    rstart = min(max(r - half, 0), GRID_H - NAT_KH)
    p, delta = r - rstart, rstart - ws
    assert 0 <= delta and delta + NAT_KH <= NAT_WIN_ROWS
    if delta % 2 == 0:
        slabs = [delta // 2 + t for t in range(half)]
        tiles = [2 * t - p + NAT_KH - 1 for t in range(half)]
    else:
        assert p == half
        slabs = [(delta - 1) // 2 + t for t in range(half + 1)]
        tiles = ([NAT_TILE_LEFT_MASKED] + [2 * t - 1 - p + NAT_KH - 1 for t in range(1, half)]
                 + [NAT_TILE_RIGHT_MASKED])
    return slabs, tiles


def _nat_build_tiles(rpbw_ref, t2_ref):
    lane = lax.broadcasted_iota(jnp.int32, (GRID_W, LANE), 1)
    cq = lax.broadcasted_iota(jnp.int32, (GRID_W, LANE), 0)
    kc = lane % GRID_W
    cstart = jnp.clip(cq - NAT_KW // 2, 0, GRID_W - NAT_KW)
    inside = (kc >= cstart) & (kc < cstart + NAT_KW)
    for d in range(NAT_PAIR_TILES):
        w = jnp.broadcast_to(rpbw_ref[d:d + 1, :] * LOG2E, (GRID_W, LANE))
        t = pltpu.roll(w, LANE - (NAT_KW - 1), 1, stride=1, stride_axis=0)
        t2_ref[d] = jnp.where(inside, t, MASK_VALUE)
    half = NAT_KH // 2
    t2_ref[NAT_TILE_LEFT_MASKED] = jnp.where(lane < GRID_W, MASK_VALUE, t2_ref[half - 2])
    t2_ref[NAT_TILE_RIGHT_MASKED] = jnp.where(lane < GRID_W, t2_ref[NAT_KH + half - 2], MASK_VALUE)


def _nat_softmax_block(cls, s_ref, sc_ref, t2_ref, p_ref, pc_ref, m_ref):
    n_ctx_slabs = CTX_LEN // LANE
    subs = []
    for i in range(NAT_BLOCK_ROWS):
        slabs, tiles = _nat_row_plan(cls, i)
        for a in range(NAT_SLABS):
            if a not in slabs:
                p_ref[i * GRID_W:(i + 1) * GRID_W, a * LANE:(a + 1) * LANE] = jnp.zeros((GRID_W, LANE), BF16)
        for sub in range(GRID_W // SOFTMAX_ROWS):
            trow = slice(sub * SOFTMAX_ROWS, (sub + 1) * SOFTMAX_ROWS)
            rows = slice(i * GRID_W + sub * SOFTMAX_ROWS, i * GRID_W + (sub + 1) * SOFTMAX_ROWS)
            subs.append((rows, trow, slabs, tiles))

    def scores(rows, trow, slabs, tiles):
        sv = [s_ref[rows, a * LANE:(a + 1) * LANE] + t2_ref[t, trow, :] for a, t in zip(slabs, tiles)]
        return sv + [sc_ref[rows, c * LANE:(c + 1) * LANE] for c in range(n_ctx_slabs)]

    for rows, trow, slabs, tiles in subs:
        m = jnp.max(functools.reduce(jnp.maximum, scores(rows, trow, slabs, tiles)), axis=-1, keepdims=True)
        m_ref[rows, :] = jnp.broadcast_to(m, (SOFTMAX_ROWS, LANE))
    for rows, trow, slabs, tiles in subs:
        m = m_ref[rows, :]
        pv = [jnp.exp2(s - m) for s in scores(rows, trow, slabs, tiles)]
        for a, pa in zip(slabs, pv):
            p_ref[rows, a * LANE:(a + 1) * LANE] = pa.astype(BF16)
        for c in range(n_ctx_slabs):
            pc_ref[rows, c * LANE:(c + 1) * LANE] = pv[len(slabs) + c].astype(BF16)


def _nat_attn_kernel(q_ref, k_ref, v_ref, qc_ref, kc_ref, vc_ref, rpbw_ref, w1f_ref, w3f_ref, w2f_ref,
                     ol_ref, oc_ref, w1b_ref, w3b_ref, w2b_ref,
                     t2_ref, s_ref, sc_ref, p_ref, pc_ref, m_ref, vaug_ref):
    _cast_weight_slabs((w1f_ref, w3f_ref, w2f_ref), (w1b_ref, w3b_ref, w2b_ref))
    kc, vc = kc_ref[...], vc_ref[...]
    p, l = _softmax_rows(_dot_nt(qc_ref[...], kc))
    oc_ref[...] = (_dot(p.astype(BF16), vc) / l).astype(BF16)

    _nat_build_tiles(rpbw_ref, t2_ref)
    vaug_ref[0:SEQ, 0:HEAD_DIM] = v_ref[...]
    vaug_ref[SEQ:, 0:HEAD_DIM] = vc
    vaug_ref[:, HEAD_DIM:] = jnp.ones((SEQ + CTX_LEN, HEAD_DIM), BF16)

    n_blocks = GRID_H // NAT_BLOCK_ROWS
    half_win = (NAT_KH // 2) * GRID_W
    specs = [("first", 0, 0)]
    specs += [("mid", j * NAT_Q, j * NAT_Q - half_win) for j in range(1, n_blocks - 1)]
    specs += [("last", SEQ - NAT_Q, SEQ - NAT_K)]

    def scores(n):
        _, q_tok, k_tok = specs[n]
        qb = q_ref[q_tok:q_tok + NAT_Q, :]
        s_ref[n % NAT_BUFFERS] = _dot_nt(qb, k_ref[k_tok:k_tok + NAT_K, :])
        sc_ref[n % NAT_BUFFERS] = _dot_nt(qb, kc)

    def finish(n):
        cls, q_tok, k_tok = specs[n]
        b = n % NAT_BUFFERS
        _nat_softmax_block(cls, s_ref.at[b], sc_ref.at[b], t2_ref, p_ref.at[b], pc_ref.at[b], m_ref.at[b])
        o = _dot(p_ref[b], vaug_ref[k_tok:k_tok + NAT_K, :]) + _dot(pc_ref[b], vaug_ref[SEQ:, :])
        ol_ref[q_tok:q_tok + NAT_Q, :] = (o[:, 0:HEAD_DIM] / o[:, HEAD_DIM:]).astype(BF16)

    scores(0)
    for n in range(n_blocks):
        if n + 1 < n_blocks:
            scores(n + 1)
        finish(n)


def _nat_attn(qkv, rpbw, ffn_weights, first_layer, n_layers):
    ctx_blk = ROWS_LAT // CTX_LEN
    lat = lambda n: pl.BlockSpec((None, SEQ, HEAD_DIM), lambda b, h: (n, b, h))
    ctx = lambda n: pl.BlockSpec((None, CTX_LEN, HEAD_DIM), lambda b, h: (n, ctx_blk + b, h))
    w_in, w_out, w_shapes = _weight_cast_specs(first_layer, n_layers, BATCH * NAT_HEADS,
                                               lambda b, h: b * NAT_HEADS + h)
    return pl.pallas_call(
        _nat_attn_kernel,
        grid=(BATCH, NAT_HEADS),
        in_specs=[lat(0), lat(1), lat(2), ctx(0), ctx(1), ctx(2),
                  pl.BlockSpec((None, 2 * NAT_KH, LANE), lambda b, h: (h, 0, 0))] + w_in,
        out_specs=[
            pl.BlockSpec((SEQ, HEAD_DIM), lambda b, h: (b, h)),
            pl.BlockSpec((CTX_LEN, HEAD_DIM), lambda b, h: (b, h)),
        ] + w_out,
        out_shape=[
            jax.ShapeDtypeStruct((ROWS_LAT, D_MODEL), BF16),
            jax.ShapeDtypeStruct((ROWS_CTX, D_MODEL), BF16),
        ] + w_shapes,
        scratch_shapes=[
            pltpu.VMEM((NAT_TILES, GRID_W, LANE), F32),
            pltpu.VMEM((NAT_BUFFERS, NAT_Q, NAT_K), F32),
            pltpu.VMEM((NAT_BUFFERS, NAT_Q, CTX_LEN), F32),
            pltpu.VMEM((NAT_BUFFERS, NAT_Q, NAT_K), BF16),
            pltpu.VMEM((NAT_BUFFERS, NAT_Q, CTX_LEN), BF16),
            pltpu.VMEM((NAT_BUFFERS, NAT_Q, LANE), F32),
            pltpu.VMEM((SEQ + CTX_LEN, 2 * HEAD_DIM), BF16),
        ],
        compiler_params=_params("parallel", "parallel"),
        name="nat_attn",
    )(qkv, qkv, qkv, qkv, qkv, qkv, rpbw, *ffn_weights)


def _nat_rpb_pairs(rpb):
    padded = jnp.pad(rpb, ((0, 0), (0, 2), (0, GRID_W - (2 * NAT_KW - 1))))
    return jnp.concatenate([padded[:, :2 * NAT_KH], padded[:, 1:2 * NAT_KH + 1]], axis=-1)


def _rope_tables(tm):
    t = jnp.arange(SEQ)
    row = (t // GRID_W).astype(F32)
    col = (t % GRID_W).astype(F32)
    inv = ROPE_THETA ** (-jnp.arange(0, ROPE_AXIS_DIM, 2, dtype=F32) / ROPE_AXIS_DIM)
    ang = jnp.concatenate([row[:, None] * inv, col[:, None] * inv], axis=-1)
    cosf = jnp.repeat(jnp.cos(ang), 2, axis=-1)
    sign = jnp.tile(jnp.array([-1.0, 1.0], F32), HEAD_DIM // 2)
    sinf = jnp.repeat(jnp.sin(ang), 2, axis=-1) * sign
    cos_ext = jnp.concatenate([cosf, jnp.ones((tm, HEAD_DIM), F32)], axis=0)
    sin_ext = jnp.concatenate([sinf, jnp.zeros((tm, HEAD_DIM), F32)], axis=0)
    return cos_ext, sin_ext


def kernel(x, c, ctx, c_ctx, ada_w, ada_b, norm_g, ffn_w1, ffn_w3, ffn_w2, pool_w, pool_ls, gqa_wq, gqa_wk, gqa_wv, gqa_wo, gqa_qn, gqa_kn, nat_wq, nat_wk, nat_wv, nat_wo, nat_rpb, final_g):
    c8 =jnp.concatenate([c, c_ctx[None, :], jnp.zeros((8 - BATCH - 1, D_MODEL), F32)], axis=0)
    mods = _ada_all(c8, ada_w, ada_b)[:, :BATCH + 1].reshape(DEPTH, BATCH + 1, 6, D_MODEL)
    pool_w_all = pool_w.astype(BF16)
    ffn_f32 = (ffn_w1, ffn_w3, ffn_w2)
    ffn_bf16 = {}
    gqa_cast_layers, nat_cast_layers = (1, 2), (3, 1)
    attn_layers = [i for i in range(1, DEPTH) if i % N_MIXERS != 0]
    attn_bf16 = {}

    xa = None
    for i in range(DEPTH):
        kind, j = i % N_MIXERS, i // N_MIXERS
        last = i == DEPTH - 1
        n_rows = ROWS_LAT if last else ROWS_ALL
        g1, g2 = norm_g[i, 0], norm_g[i, 1]
        if kind == 0:
            x_lat, x_ctx = (x.reshape(ROWS_LAT, D_MODEL), ctx.reshape(ROWS_CTX, D_MODEL)) if i == 0 else (xa, xa)
            if i in ffn_bf16:
                xa, h2 = _pool_layer(x_lat, x_ctx, mods[i], g1, g2, pool_w_all, j, pool_ls[j], n_rows)
            else:
                xa, h2, *cast = _pool_layer(x_lat, x_ctx, mods[i], g1, g2, pool_w_all, j, pool_ls[j], n_rows,
                                            cast_ffn_layer=i, ffn_weights=ffn_f32)
                ffn_bf16[i] = (tuple(cast), 0)
        else:
            w_qkv, wo = attn_bf16[i]
            if kind == 1:
                cos_ext, sin_ext = _rope_tables(GQA_PROJ_ROWS)
                q, k, v = _gqa_proj(xa, mods[i], g1, w_qkv, gqa_qn[j], gqa_kn[j], cos_ext, sin_ext)
                first, count = gqa_cast_layers
                o_lat, o_ctx, *cast = _gqa_attn(q, k, v, ffn_f32, first, count)
            else:
                qkv = _nat_proj(xa, mods[i], g1, w_qkv)
                first, count = nat_cast_layers
                o_lat, o_ctx, *cast = _nat_attn(qkv, _nat_rpb_pairs(nat_rpb[j]), ffn_f32, first, count)
            for n in range(count):
                ffn_bf16[first + n] = (tuple(cast), n)
            xa, h2 = _attn_out_layer(o_lat, o_ctx, xa, mods[i], wo, g2)
        (w1_all, w3_all, w2_all), idx = ffn_bf16[i]
        next_norm = None
        side_casts = []
        if i == 0:
            for a in attn_layers:
                ja = a // N_MIXERS
                if a % N_MIXERS == 1:
                    side_casts += [("concat", ja, [gqa_wq, gqa_wk, gqa_wv]), ("concat", ja, [gqa_wo])]
                else:
                    side_casts += [("stack", ja, [nat_wq, nat_wk, nat_wv]), ("concat", ja, [nat_wo])]
        outs = _ffn_layer(xa, h2, mods[i], w1_all, w3_all, w2_all, idx, final_g, n_rows, last,
                          next_norm=next_norm, side_casts=side_casts)
        if next_norm is None and not side_casts:
            xa = outs
        else:
            xa, rest = outs[0], list(outs[1:])
            if next_norm is not None:
                h1 = rest.pop(0)
            for n, a in enumerate(attn_layers if side_casts else ()):
                attn_bf16[a] = (rest[2 * n], rest[2 * n + 1])
    return xa.reshape(BATCH, SEQ, D_MODEL)
```

```python
import functools

import jax
import jax.numpy as jnp
import numpy as np
from jax import lax
from jax.experimental import pallas as pl
from jax.experimental.pallas import tpu as pltpu

D_MODEL = 2048
BATCH = 2
SEQ = 4096
DEPTH = 4
GRID_W = 64
CTX_LEN = 256
N_MIXERS = 3
POOL_GROUPS = 4
POOL_WINDOWS = (2, 4, 8, 16)
POOL_GC = D_MODEL // POOL_GROUPS
HEAD_DIM = 128
GQA_HEADS = D_MODEL // HEAD_DIM
GQA_KV_HEADS = 4
GQA_GROUP = GQA_HEADS // GQA_KV_HEADS
ROPE_THETA = 10000.0
ROPE_AXIS_DIM = HEAD_DIM // 2
NAT_HEADS = D_MODEL // HEAD_DIM
NAT_KH = 8
NAT_KW = 16
FFN_HIDDEN = -(-8 * D_MODEL // (3 * 256)) * 256
NORM_EPS = 1e-6

ROWS_LAT = BATCH * SEQ
ROWS_CTX = BATCH * CTX_LEN
ROWS_ALL = ROWS_LAT + ROWS_CTX
GRID_H = SEQ // GRID_W
ATTN_SCALE = HEAD_DIM ** -0.5
LOG2E = 1.4426950408889634
QK_PRESCALE = ATTN_SCALE * LOG2E
MASK_VALUE = -1e30
LANE = 128
POOL_HALO = 8
SOFTMAX_ROWS = 32
GQA_PROJ_ROWS = 256

VMEM_LIMIT_BYTES = 56 * 1024 * 1024

F32 = jnp.float32
BF16 = jnp.bfloat16


def _params(*sem):
    return pltpu.CompilerParams(dimension_semantics=sem, vmem_limit_bytes=VMEM_LIMIT_BYTES)


def _dot(a, b):
    return jnp.dot(a, b, preferred_element_type=F32)


def _dot_nt(a, b):
    return lax.dot_general(a, b, (((1,), (1,)), ((), ())), preferred_element_type=F32)


def _silu(a):
    return a * (1.0 / (1.0 + jnp.exp(-a)))


def _rms(x, g):
    return x * lax.rsqrt(jnp.mean(x * x, axis=-1, keepdims=True) + NORM_EPS) * g


def _norm_mod(x, g, shift, scale):
    return x * lax.rsqrt(jnp.mean(x * x, axis=-1, keepdims=True) + NORM_EPS) * (g * (1.0 + scale)) + shift


def _mod_row(t, tm):
    return jnp.where(t < ROWS_LAT // tm, t // (SEQ // tm), BATCH)


def _ada_kernel(c_ref, w_ref, b_ref, o_ref):
    s = _silu(c_ref[...])
    o_ref[...] = _dot(s.astype(BF16), w_ref[...].astype(BF16)) + b_ref[...]


def _ada_all(c8, ada_w, ada_b):
    tn = 1024
    return pl.pallas_call(
        _ada_kernel,
        grid=(DEPTH, 6 * D_MODEL // tn),
        in_specs=[
            pl.BlockSpec((8, D_MODEL), lambda l, n: (0, 0)),
            pl.BlockSpec((None, D_MODEL, tn), lambda l, n: (l, 0, n)),
            pl.BlockSpec((None, 1, tn), lambda l, n: (l, 0, n)),
        ],
        out_specs=pl.BlockSpec((None, 8, tn), lambda l, n: (l, 0, n)),
        out_shape=jax.ShapeDtypeStruct((DEPTH, 8, 6 * D_MODEL), F32),
        compiler_params=_params("parallel", "parallel"),
        name="ada",
    )(c8, ada_w, ada_b.reshape(DEPTH, 1, 6 * D_MODEL))


def _pool_bands(tm):
    t = np.arange(tm)[:, None]
    j = np.arange(tm + 2 * POOL_HALO)[None, :]
    pos = np.where(j < tm, j, np.where(j < tm + POOL_HALO, j - tm - POOL_HALO, j - POOL_HALO))
    bands = [(pos >= t - w // 2) & (pos < t + w - w // 2) for w in POOL_WINDOWS]
    return jnp.asarray(np.stack(bands).astype(np.float32), BF16)


def _split_bf16(v):
    hi = v.astype(BF16)
    return hi, (v - hi.astype(F32)).astype(BF16)


def _pool_kernel(xl_ref, xc_ref, xp_ref, xn_ref, mod_ref, g1_ref, g2_ref, band_ref, w_ref, ls_ref, *rest,
                 tm, cast_ffn):
    if cast_ffn:
        w1f_ref, w3f_ref, w2f_ref, o_ref, h2_ref, w1b_ref, w3b_ref, w2b_ref, hi_ref, lo_ref = rest
        _cast_weight_slabs((w1f_ref, w3f_ref, w2f_ref), (w1b_ref, w3b_ref, w2b_ref))
    else:
        o_ref, h2_ref, hi_ref, lo_ref = rest
    t = pl.program_id(0)
    shift, scale, gate = mod_ref[0:1, :], mod_ref[1:2, :], mod_ref[2:3, :]
    g = g1_ref[...]

    n_lat_tiles = ROWS_LAT // tm
    is_lat = t < n_lat_tiles
    tiles_in_seq = jnp.where(is_lat, SEQ // tm, CTX_LEN // tm)
    pos_tile = jnp.where(is_lat, t % (SEQ // tm), (t - n_lat_tiles) % (CTX_LEN // tm))
    first = pos_tile == 0
    last = pos_tile == tiles_in_seq - 1

    x = jnp.where(is_lat, xl_ref[...], xc_ref[...])
    h = _norm_mod(x, g, shift, scale)
    halo = jnp.concatenate([jnp.where(first, 0.0, _norm_mod(xp_ref[...], g, shift, scale)),
                            jnp.where(last, 0.0, _norm_mod(xn_ref[...], g, shift, scale))], axis=0)
    hi_ref[0:tm, :], lo_ref[0:tm, :] = _split_bf16(h)
    hi_ref[tm:, :], lo_ref[tm:, :] = _split_bf16(halo)

    pos = pos_tile * tm + lax.broadcasted_iota(jnp.int32, (tm, 1), 0)
    n = tiles_in_seq * tm
    for k in range(POOL_GROUPS):
        w = POOL_WINDOWS[k]
        lanes = slice(k * POOL_GC, (k + 1) * POOL_GC)
        wsum = _dot(band_ref[k], hi_ref[:, lanes]) + _dot(band_ref[k], lo_ref[:, lanes])
        cnt = jnp.minimum(pos + (w - w // 2), n) - jnp.maximum(pos - w // 2, 0)
        pooled = wsum / cnt.astype(F32) - h[:, lanes]
        y = _dot(pooled.astype(BF16), w_ref[k]) * ls_ref[:, lanes]
        o_ref[:, lanes] = x[:, lanes] + gate[:, lanes] * y
    h2_ref[...] = _norm_mod(o_ref[...], g2_ref[...], mod_ref[3:4, :], mod_ref[4:5, :]).astype(BF16)


def _pool_layer(x_lat, x_ctx, mods_l, g1, g2, w_all, j, ls, n_rows, cast_ffn_layer=None, ffn_weights=()):
    tm = 256
    n_lat = ROWS_LAT // tm
    nb8 = x_lat.shape[0] // POOL_HALO
    ctx_blk0 = (x_ctx.shape[0] - ROWS_CTX) // tm
    halo_blk = lambda b: jnp.clip(b, 0, nb8 - 1)
    row = lambda v: v.reshape(1, D_MODEL)
    cast_ffn = cast_ffn_layer is not None
    w_in, w_out, w_shapes = (_weight_cast_specs(cast_ffn_layer, 1, n_lat, lambda t: jnp.minimum(t, n_lat - 1))
                             if cast_ffn else ([], [], []))
    return pl.pallas_call(
        functools.partial(_pool_kernel, tm=tm, cast_ffn=cast_ffn),
        grid=(n_rows // tm,),
        in_specs=[
            pl.BlockSpec((tm, D_MODEL), lambda t: (jnp.minimum(t, n_lat - 1), 0)),
            pl.BlockSpec((tm, D_MODEL), lambda t: (ctx_blk0 + jnp.maximum(t - n_lat, 0), 0)),
            pl.BlockSpec((POOL_HALO, D_MODEL), lambda t: (halo_blk(t * (tm // POOL_HALO) - 1), 0)),
            pl.BlockSpec((POOL_HALO, D_MODEL), lambda t: (halo_blk((t + 1) * (tm // POOL_HALO)), 0)),
            pl.BlockSpec((None, 6, D_MODEL), lambda t: (_mod_row(t, tm), 0, 0)),
            pl.BlockSpec((1, D_MODEL), lambda t: (0, 0)),
            pl.BlockSpec((1, D_MODEL), lambda t: (0, 0)),
            pl.BlockSpec((POOL_GROUPS, tm, tm + 2 * POOL_HALO), lambda t: (0, 0, 0)),
            pl.BlockSpec((None, POOL_GROUPS, POOL_GC, POOL_GC), lambda t: (j, 0, 0, 0)),
            pl.BlockSpec((1, D_MODEL), lambda t: (0, 0)),
        ] + w_in,
        out_specs=[pl.BlockSpec((tm, D_MODEL), lambda t: (t, 0)),
                   pl.BlockSpec((tm, D_MODEL), lambda t: (t, 0))] + w_out,
        out_shape=[jax.ShapeDtypeStruct((n_rows, D_MODEL), F32),
                   jax.ShapeDtypeStruct((n_rows, D_MODEL), BF16)] + w_shapes,
        scratch_shapes=[pltpu.VMEM((tm + 2 * POOL_HALO, D_MODEL), BF16),
                        pltpu.VMEM((tm + 2 * POOL_HALO, D_MODEL), BF16)],
        compiler_params=_params("parallel"),
        name="pool_layer",
    )(x_lat, x_ctx, x_lat, x_lat, mods_l, row(g1), row(g2), _pool_bands(tm), w_all, row(ls), *ffn_weights)


ATTN_CAST_ROWS = 16


def _ffn_kernel(*refs, nf, final, emit_next_h, cast_plan):
    it = iter(refs)
    x_ref, h_ref, mod_ref, w1_ref, w3_ref, w2_ref, fg_ref = (next(it) for _ in range(7))
    if emit_next_h:
        gn_ref, modn_ref = next(it), next(it)
    src_refs = [[next(it) for _ in cols] for _, cols in cast_plan]
    o_ref = next(it)
    if emit_next_h:
        hn_ref = next(it)
    dst_refs = [next(it) for _ in cast_plan]
    f = pl.program_id(1)

    for (kind, cols), srcs, dst in zip(cast_plan, src_refs, dst_refs):
        off = 0
        for j, (c, src) in enumerate(zip(cols, srcs)):
            if kind == "stack":
                dst[j] = src[...].astype(BF16)
            else:
                dst[:, off:off + c] = src[...].astype(BF16)
                off += c

    @pl.when(f == 0)
    def _():
        o_ref[...] = x_ref[...]

    h = h_ref[...]
    a = _dot(h, w1_ref[...])
    b = _dot(h, w3_ref[...])
    o_ref[...] += mod_ref[5:6, :] * _dot((_silu(a) * b).astype(BF16), w2_ref[...])

    if final or emit_next_h:
        @pl.when(f == nf - 1)
        def _():
            y = o_ref[...]
            if final:
                y = _rms(y, fg_ref[...])
                o_ref[...] = y
            if emit_next_h:
                hn_ref[...] = _norm_mod(y, gn_ref[...], modn_ref[0:1, :], modn_ref[1:2, :]).astype(BF16)


def _ffn_layer(x, h, mods_l, w1_all, w3_all, w2_all, layer, final_g, n_rows, final, next_norm=None, side_casts=()):
    tm, tf = 512, 512
    nf = FFN_HIDDEN // tf
    emit_next_h = next_norm is not None
    n_slabs = D_MODEL // ATTN_CAST_ROWS
    assert not side_casts or (n_rows // tm) * nf >= n_slabs
    slab = lambda t, f: jnp.minimum(t * nf + f, n_slabs - 1)
    tile = pl.BlockSpec((tm, D_MODEL), lambda t, f: (t, 0))
    mod_spec = pl.BlockSpec((None, 6, D_MODEL), lambda t, f: (_mod_row(t, tm), 0, 0))
    row_spec = pl.BlockSpec((1, D_MODEL), lambda t, f: (0, 0))
    in_specs = [
        tile, tile, mod_spec,
        pl.BlockSpec((None, D_MODEL, tf), lambda t, f: (layer, 0, f)),
        pl.BlockSpec((None, D_MODEL, tf), lambda t, f: (layer, 0, f)),
        pl.BlockSpec((None, tf, D_MODEL), lambda t, f: (layer, f, 0)),
        row_spec,
    ]
    args = [x, h, mods_l, w1_all, w3_all, w2_all, final_g.reshape(1, D_MODEL)]
    out_specs, out_shape = [tile], [jax.ShapeDtypeStruct((n_rows, D_MODEL), F32)]
    if emit_next_h:
        in_specs += [row_spec, mod_spec]
        args += [next_norm[0].reshape(1, D_MODEL), next_norm[1]]
        out_specs.append(tile)
        out_shape.append(jax.ShapeDtypeStruct((n_rows, D_MODEL), BF16))
    cast_plan = []
    for kind, idx, weights in side_casts:
        cols = tuple(w.shape[-1] for w in weights)
        cast_plan.append((kind, cols))
        for w, c in zip(weights, cols):
            in_specs.append(pl.BlockSpec((None, ATTN_CAST_ROWS, c), lambda t, f, idx=idx: (idx, slab(t, f), 0)))
            args.append(w)
        if kind == "stack":
            out_specs.append(pl.BlockSpec((len(cols), ATTN_CAST_ROWS, cols[0]), lambda t, f: (0, slab(t, f), 0)))
            out_shape.append(jax.ShapeDtypeStruct((len(cols), D_MODEL, cols[0]), BF16))
        else:
            out_specs.append(pl.BlockSpec((ATTN_CAST_ROWS, sum(cols)), lambda t, f: (slab(t, f), 0)))
            out_shape.append(jax.ShapeDtypeStruct((D_MODEL, sum(cols)), BF16))
    outs = pl.pallas_call(
        functools.partial(_ffn_kernel, nf=nf, final=final, emit_next_h=emit_next_h, cast_plan=tuple(cast_plan)),
        grid=(n_rows // tm, nf),
        in_specs=in_specs,
        out_specs=out_specs,
        out_shape=out_shape,
        compiler_params=_params("parallel", "arbitrary"),
        name="ffn_layer",
    )(*args)
    return outs[0] if len(outs) == 1 else outs


def _attn_out_kernel(ol_ref, oc_ref, x_ref, mod_ref, wo_ref, g2_ref, o_ref, h2_ref, *, tm):
    t = pl.program_id(0)
    o = jnp.where(t < ROWS_LAT // tm, ol_ref[...], oc_ref[...])
    y = x_ref[...] + mod_ref[2:3, :] * _dot(o, wo_ref[...])
    o_ref[...] = y
    h2_ref[...] = _norm_mod(y, g2_ref[...], mod_ref[3:4, :], mod_ref[4:5, :]).astype(BF16)


def _attn_out_layer(o_lat, o_ctx, x, mods_l, wo, g2):
    tm = 512
    n_lat = ROWS_LAT // tm
    return pl.pallas_call(
        functools.partial(_attn_out_kernel, tm=tm),
        grid=(ROWS_ALL // tm,),
        in_specs=[
            pl.BlockSpec((tm, D_MODEL), lambda t: (jnp.minimum(t, n_lat - 1), 0)),
            pl.BlockSpec((tm, D_MODEL), lambda t: (jnp.maximum(t - n_lat, 0), 0)),
            pl.BlockSpec((tm, D_MODEL), lambda t: (t, 0)),
            pl.BlockSpec((None, 6, D_MODEL), lambda t: (_mod_row(t, tm), 0, 0)),
            pl.BlockSpec((D_MODEL, D_MODEL), lambda t: (0, 0)),
            pl.BlockSpec((1, D_MODEL), lambda t: (0, 0)),
        ],
        out_specs=[pl.BlockSpec((tm, D_MODEL), lambda t: (t, 0)), pl.BlockSpec((tm, D_MODEL), lambda t: (t, 0))],
        out_shape=[jax.ShapeDtypeStruct((ROWS_ALL, D_MODEL), F32), jax.ShapeDtypeStruct((ROWS_ALL, D_MODEL), BF16)],
        compiler_params=_params("parallel"),
        name="attn_out",
    )(o_lat, o_ctx, x, mods_l, wo, g2.reshape(1, D_MODEL))


def _rope(xh, cosf, sinf, even):
    swapped = jnp.where(even, pltpu.roll(xh, HEAD_DIM - 1, 1), pltpu.roll(xh, 1, 1))
    return xh * cosf + swapped * sinf


def _gqa_proj_kernel(x_ref, mod_ref, g_ref, w_ref, qn_ref, kn_ref, cos_ref, sin_ref, q_ref, k_ref, v_ref, *, tm):
    h = _norm_mod(x_ref[...], g_ref[...], mod_ref[0:1, :], mod_ref[1:2, :]).astype(BF16)
    qkv = _dot(h, w_ref[...])
    cosf, sinf = cos_ref[...], sin_ref[...]
    even = lax.broadcasted_iota(jnp.int32, (tm, HEAD_DIM), 1) % 2 == 0
    for hd in range(GQA_HEADS):
        lanes = slice(hd * HEAD_DIM, (hd + 1) * HEAD_DIM)
        qh = _rope(_rms(qkv[:, lanes], qn_ref[...]), cosf, sinf, even)
        q_ref[:, lanes] = (qh * QK_PRESCALE).astype(BF16)
    k0 = GQA_HEADS * HEAD_DIM
    for hd in range(GQA_KV_HEADS):
        lanes = slice(hd * HEAD_DIM, (hd + 1) * HEAD_DIM)
        kh = qkv[:, k0 + hd * HEAD_DIM:k0 + (hd + 1) * HEAD_DIM]
        k_ref[:, lanes] = _rope(_rms(kh, kn_ref[...]), cosf, sinf, even).astype(BF16)
    v_ref[...] = qkv[:, k0 + GQA_KV_HEADS * HEAD_DIM:].astype(BF16)


def _gqa_proj(x, mods_l, g, w_qkv, qn, kn, cos_ext, sin_ext):
    tm = GQA_PROJ_ROWS
    dq, dkv = GQA_HEADS * HEAD_DIM, GQA_KV_HEADS * HEAD_DIM
    n_lat = ROWS_LAT // tm
    rope_blk = lambda t: (jnp.where(t < n_lat, t % (SEQ // tm), SEQ // tm), 0)
    return pl.pallas_call(
        functools.partial(_gqa_proj_kernel, tm=tm),
        grid=(ROWS_ALL // tm,),
        in_specs=[
            pl.BlockSpec((tm, D_MODEL), lambda t: (t, 0)),
            pl.BlockSpec((None, 6, D_MODEL), lambda t: (_mod_row(t, tm), 0, 0)),
            pl.BlockSpec((1, D_MODEL), lambda t: (0, 0)),
            pl.BlockSpec((D_MODEL, dq + 2 * dkv), lambda t: (0, 0)),
            pl.BlockSpec((1, HEAD_DIM), lambda t: (0, 0)),
            pl.BlockSpec((1, HEAD_DIM), lambda t: (0, 0)),
            pl.BlockSpec((tm, HEAD_DIM), rope_blk),
            pl.BlockSpec((tm, HEAD_DIM), rope_blk),
        ],
        out_specs=[
            pl.BlockSpec((tm, dq), lambda t: (t, 0)),
            pl.BlockSpec((tm, dkv), lambda t: (t, 0)),
            pl.BlockSpec((tm, dkv), lambda t: (t, 0)),
        ],
        out_shape=[
            jax.ShapeDtypeStruct((ROWS_ALL, dq), BF16),
            jax.ShapeDtypeStruct((ROWS_ALL, dkv), BF16),
            jax.ShapeDtypeStruct((ROWS_ALL, dkv), BF16),
        ],
        compiler_params=_params("parallel"),
        name="gqa_proj",
    )(x, mods_l, g.reshape(1, D_MODEL), w_qkv, qn.reshape(1, HEAD_DIM), kn.reshape(1, HEAD_DIM), cos_ext, sin_ext)


def _softmax_rows(s):
    m = jnp.max(s, axis=-1, keepdims=True)
    p = jnp.exp2(s - m)
    return p, jnp.sum(p, axis=-1, keepdims=True)


def _stack_heads(ref, n):
    return jnp.concatenate([ref[:, g * HEAD_DIM:(g + 1) * HEAD_DIM] for g in range(n)], axis=0)


def _online_softmax_chunk(s_ref, p_ref, m_ref, a_ref, n_rows, width):
    nslab = width // LANE
    blocks = [slice(i * SOFTMAX_ROWS, (i + 1) * SOFTMAX_ROWS) for i in range(n_rows // SOFTMAX_ROWS)]
    for rows in blocks:
        mc = functools.reduce(jnp.maximum, [s_ref[rows, c * LANE:(c + 1) * LANE] for c in range(nslab)])
        m_prev = m_ref[rows, :]
        m_new = jnp.maximum(m_prev, jnp.max(mc, axis=-1, keepdims=True))
        a_ref[rows, :] = jnp.exp2(m_prev - m_new)
        m_ref[rows, :] = m_new
    for rows in blocks:
        m_new = m_ref[rows, :]
        for c in range(nslab):
            p_ref[rows, c * LANE:(c + 1) * LANE] = jnp.exp2(s_ref[rows, c * LANE:(c + 1) * LANE] - m_new).astype(BF16)


FFN_WEIGHT_SHAPES = ((D_MODEL, FFN_HIDDEN), (D_MODEL, FFN_HIDDEN), (FFN_HIDDEN, D_MODEL))


def _weight_cast_specs(first_layer, n_layers, n_steps, step_id):
    per_layer = n_steps // n_layers
    in_specs, out_specs, out_shapes = [], [], []
    for rows, cols in FFN_WEIGHT_SHAPES:
        blk = (None, rows // per_layer, cols)
        in_specs.append(pl.BlockSpec(
            blk, lambda *g: (first_layer + step_id(*g) // per_layer, step_id(*g) % per_layer, 0)))
        out_specs.append(pl.BlockSpec(blk, lambda *g: (step_id(*g) // per_layer, step_id(*g) % per_layer, 0)))
        out_shapes.append(jax.ShapeDtypeStruct((n_layers, rows, cols), BF16))
    return in_specs, out_specs, out_shapes


def _cast_weight_slabs(src_refs, dst_refs):
    for src, dst in zip(src_refs, dst_refs):
        dst[...] = src[...].astype(BF16)


def _gqa_attn_kernel(q_ref, qc_ref, k_ref, v_ref, kc_ref, vc_ref, w1f_ref, w3f_ref, w2f_ref,
                     ol_ref, oc_ref, w1b_ref, w3b_ref, w2b_ref,
                     s_ref, p_ref, m_ref, a_ref, acc_ref, vaug_ref, *, tq, tk, n_sub):
    _cast_weight_slabs((w1f_ref, w3f_ref, w2f_ref), (w1b_ref, w3b_ref, w2b_ref))
    kc, vc = kc_ref[...], vc_ref[...]
    sub_q = tq // n_sub
    n_rows = GQA_GROUP * sub_q

    @pl.when(pl.program_id(2) == 0)
    def _():
        p, l = _softmax_rows(_dot_nt(_stack_heads(qc_ref, GQA_GROUP), kc))
        oc = _dot(p.astype(BF16), vc) / l
        for g in range(GQA_GROUP):
            oc_ref[:, g * HEAD_DIM:(g + 1) * HEAD_DIM] = oc[g * CTX_LEN:(g + 1) * CTX_LEN].astype(BF16)
        vaug_ref[0:SEQ, 0:HEAD_DIM] = v_ref[...]
        vaug_ref[SEQ:, 0:HEAD_DIM] = vc
        vaug_ref[:, HEAD_DIM:] = jnp.ones((SEQ + CTX_LEN, HEAD_DIM), BF16)

    qs = [_stack_heads(q_ref.at[u * sub_q:(u + 1) * sub_q, :], GQA_GROUP) for u in range(n_sub)]
    m_ref[...] = jnp.full_like(m_ref, MASK_VALUE)
    acc_ref[...] = jnp.zeros_like(acc_ref)

    chunks = [(k_ref, j * tk, j * tk, tk) for j in range(SEQ // tk)] + [(kc_ref, 0, SEQ, CTX_LEN)]
    for n, (kr, off, voff, width) in enumerate(chunks):
        for u in range(n_sub):
            sb, pb = s_ref.at[u, n % 2], p_ref.at[u, n % 2]
            sb[:, :width] = _dot_nt(qs[u], kr[off:off + width, :])
            _online_softmax_chunk(sb, pb, m_ref.at[u], a_ref.at[u], n_rows, width)
            pv = _dot(pb[:, :width], vaug_ref[voff:voff + width, :])
            for half in (slice(0, HEAD_DIM), slice(HEAD_DIM, 2 * HEAD_DIM)):
                acc_ref[u, :, half] = a_ref[u] * acc_ref[u, :, half] + pv[:, half]

    for u in range(n_sub):
        o = acc_ref[u, :, 0:HEAD_DIM] / acc_ref[u, :, HEAD_DIM:]
        for g in range(GQA_GROUP):
            ol_ref[u * sub_q:(u + 1) * sub_q, g * HEAD_DIM:(g + 1) * HEAD_DIM] = (
                o[g * sub_q:(g + 1) * sub_q].astype(BF16))


def _gqa_attn(q, k, v, ffn_weights, first_layer, n_layers):
    tq, tk, n_sub = 512, 1024, 2
    sub_rows = GQA_GROUP * tq // n_sub
    gw = GQA_GROUP * HEAD_DIM
    nq = SEQ // tq
    ctx_blk = ROWS_LAT // CTX_LEN
    w_in, w_out, w_shapes = _weight_cast_specs(first_layer, n_layers, BATCH * GQA_KV_HEADS * nq,
                                               lambda b, h, i: (b * GQA_KV_HEADS + h) * nq + i)
    return pl.pallas_call(
        functools.partial(_gqa_attn_kernel, tq=tq, tk=tk, n_sub=n_sub),
        grid=(BATCH, GQA_KV_HEADS, nq),
        in_specs=[
            pl.BlockSpec((tq, gw), lambda b, h, i: (b * nq + i, h)),
            pl.BlockSpec((CTX_LEN, gw), lambda b, h, i: (ctx_blk + b, h)),
            pl.BlockSpec((SEQ, HEAD_DIM), lambda b, h, i: (b, h)),
            pl.BlockSpec((SEQ, HEAD_DIM), lambda b, h, i: (b, h)),
            pl.BlockSpec((CTX_LEN, HEAD_DIM), lambda b, h, i: (ctx_blk + b, h)),
            pl.BlockSpec((CTX_LEN, HEAD_DIM), lambda b, h, i: (ctx_blk + b, h)),
        ] + w_in,
        out_specs=[
            pl.BlockSpec((tq, gw), lambda b, h, i: (b * nq + i, h)),
            pl.BlockSpec((CTX_LEN, gw), lambda b, h, i: (b, h)),
        ] + w_out,
        out_shape=[
            jax.ShapeDtypeStruct((ROWS_LAT, D_MODEL), BF16),
            jax.ShapeDtypeStruct((ROWS_CTX, D_MODEL), BF16),
        ] + w_shapes,
        scratch_shapes=[
            pltpu.VMEM((n_sub, 2, sub_rows, tk), F32),
            pltpu.VMEM((n_sub, 2, sub_rows, tk), BF16),
            pltpu.VMEM((n_sub, sub_rows, LANE), F32),
            pltpu.VMEM((n_sub, sub_rows, LANE), F32),
            pltpu.VMEM((n_sub, sub_rows, 2 * HEAD_DIM), F32),
            pltpu.VMEM((SEQ + CTX_LEN, 2 * HEAD_DIM), BF16),
        ],
        compiler_params=_params("parallel", "parallel", "arbitrary"),
        name="gqa_attn",
    )(q, q, k, v, k, v, *ffn_weights)


def _nat_proj_kernel(x_ref, mod_ref, g_ref, w_ref, o_ref):
    h = _norm_mod(x_ref[...], g_ref[...], mod_ref[0:1, :], mod_ref[1:2, :]).astype(BF16)
    o_ref[0] = (_dot(h, w_ref[0]) * QK_PRESCALE).astype(BF16)
    o_ref[1] = _dot(h, w_ref[1]).astype(BF16)
    o_ref[2] = _dot(h, w_ref[2]).astype(BF16)


def _nat_proj(x, mods_l, g, w_qkv):
    tm = 512
    return pl.pallas_call(
        _nat_proj_kernel,
        grid=(ROWS_ALL // tm,),
        in_specs=[
            pl.BlockSpec((tm, D_MODEL), lambda t: (t, 0)),
            pl.BlockSpec((None, 6, D_MODEL), lambda t: (_mod_row(t, tm), 0, 0)),
            pl.BlockSpec((1, D_MODEL), lambda t: (0, 0)),
            pl.BlockSpec((3, D_MODEL, D_MODEL), lambda t: (0, 0, 0), pipeline_mode=pl.Buffered(1)),
        ],
        out_specs=pl.BlockSpec((3, tm, D_MODEL), lambda t: (0, t, 0)),
        out_shape=jax.ShapeDtypeStruct((3, ROWS_ALL, D_MODEL), BF16),
        compiler_params=_params("parallel"),
        name="nat_proj",
    )(x, mods_l, g.reshape(1, D_MODEL), w_qkv)


NAT_BLOCK_ROWS = 8
NAT_WIN_ROWS = NAT_BLOCK_ROWS + NAT_KH
NAT_Q = NAT_BLOCK_ROWS * GRID_W
NAT_K = NAT_WIN_ROWS * GRID_W
NAT_SLABS = NAT_K // LANE
NAT_PAIR_TILES = 2 * NAT_KH - 2
NAT_TILE_LEFT_MASKED = NAT_PAIR_TILES
NAT_TILE_RIGHT_MASKED = NAT_PAIR_TILES + 1
NAT_TILES = NAT_PAIR_TILES + 2
NAT_BUFFERS = 3


def _nat_row_plan(cls, i):
    half = NAT_KH // 2
    if cls == "first":
        r, ws = i, 0
    elif cls == "mid":
        r, ws = NAT_BLOCK_ROWS + i, NAT_BLOCK_ROWS - half
    else:
        r, ws = GRID_H - NAT_BLOCK_ROWS + i, GRID_H - NAT_WIN_ROWS
    rstart = min(max(r - half, 0), GRID_H - NAT_KH)
    p, delta = r - rstart, rstart - ws
    assert 0 <= delta and delta + NAT_KH <= NAT_WIN_ROWS
    if delta % 2 == 0:
        slabs = [delta // 2 + t for t in range(half)]
        tiles = [2 * t - p + NAT_KH - 1 for t in range(half)]
    else:
        assert p == half
        slabs = [(delta - 1) // 2 + t for t in range(half + 1)]
        tiles = ([NAT_TILE_LEFT_MASKED] + [2 * t - 1 - p + NAT_KH - 1 for t in range(1, half)]
                 + [NAT_TILE_RIGHT_MASKED])
    return slabs, tiles


def _nat_build_tiles(rpbw_ref, t2_ref):
    lane = lax.broadcasted_iota(jnp.int32, (GRID_W, LANE), 1)
    cq = lax.broadcasted_iota(jnp.int32, (GRID_W, LANE), 0)
    kc = lane % GRID_W
    cstart = jnp.clip(cq - NAT_KW // 2, 0, GRID_W - NAT_KW)
    inside = (kc >= cstart) & (kc < cstart + NAT_KW)
    for d in range(NAT_PAIR_TILES):
        w = jnp.broadcast_to(rpbw_ref[d:d + 1, :] * LOG2E, (GRID_W, LANE))
        t = pltpu.roll(w, LANE - (NAT_KW - 1), 1, stride=1, stride_axis=0)
        t2_ref[d] = jnp.where(inside, t, MASK_VALUE)
    half = NAT_KH // 2
    t2_ref[NAT_TILE_LEFT_MASKED] = jnp.where(lane < GRID_W, MASK_VALUE, t2_ref[half - 2])
    t2_ref[NAT_TILE_RIGHT_MASKED] = jnp.where(lane < GRID_W, t2_ref[NAT_KH + half - 2], MASK_VALUE)


def _nat_softmax_block(cls, s_ref, sc_ref, t2_ref, p_ref, pc_ref, m_ref):
    n_ctx_slabs = CTX_LEN // LANE
    subs = []
    for i in range(NAT_BLOCK_ROWS):
        slabs, tiles = _nat_row_plan(cls, i)
        for a in range(NAT_SLABS):
            if a not in slabs:
                p_ref[i * GRID_W:(i + 1) * GRID_W, a * LANE:(a + 1) * LANE] = jnp.zeros((GRID_W, LANE), BF16)
        for sub in range(GRID_W // SOFTMAX_ROWS):
            trow = slice(sub * SOFTMAX_ROWS, (sub + 1) * SOFTMAX_ROWS)
            rows = slice(i * GRID_W + sub * SOFTMAX_ROWS, i * GRID_W + (sub + 1) * SOFTMAX_ROWS)
            subs.append((rows, trow, slabs, tiles))

    def scores(rows, trow, slabs, tiles):
        sv = [s_ref[rows, a * LANE:(a + 1) * LANE] + t2_ref[t, trow, :] for a, t in zip(slabs, tiles)]
        return sv + [sc_ref[rows, c * LANE:(c + 1) * LANE] for c in range(n_ctx_slabs)]

    for rows, trow, slabs, tiles in subs:
        m = jnp.max(functools.reduce(jnp.maximum, scores(rows, trow, slabs, tiles)), axis=-1, keepdims=True)
        m_ref[rows, :] = jnp.broadcast_to(m, (SOFTMAX_ROWS, LANE))
    for rows, trow, slabs, tiles in subs:
        m = m_ref[rows, :]
        pv = [jnp.exp2(s - m) for s in scores(rows, trow, slabs, tiles)]
        for a, pa in zip(slabs, pv):
            p_ref[rows, a * LANE:(a + 1) * LANE] = pa.astype(BF16)
        for c in range(n_ctx_slabs):
            pc_ref[rows, c * LANE:(c + 1) * LANE] = pv[len(slabs) + c].astype(BF16)


def _nat_attn_kernel(q_ref, k_ref, v_ref, qc_ref, kc_ref, vc_ref, rpbw_ref, w1f_ref, w3f_ref, w2f_ref,
                     ol_ref, oc_ref, w1b_ref, w3b_ref, w2b_ref,
                     t2_ref, s_ref, sc_ref, p_ref, pc_ref, m_ref, vaug_ref):
    _cast_weight_slabs((w1f_ref, w3f_ref, w2f_ref), (w1b_ref, w3b_ref, w2b_ref))
    kc, vc = kc_ref[...], vc_ref[...]
    p, l = _softmax_rows(_dot_nt(qc_ref[...], kc))
    oc_ref[...] = (_dot(p.astype(BF16), vc) / l).astype(BF16)

    _nat_build_tiles(rpbw_ref, t2_ref)
    vaug_ref[0:SEQ, 0:HEAD_DIM] = v_ref[...]
    vaug_ref[SEQ:, 0:HEAD_DIM] = vc
    vaug_ref[:, HEAD_DIM:] = jnp.ones((SEQ + CTX_LEN, HEAD_DIM), BF16)

    n_blocks = GRID_H // NAT_BLOCK_ROWS
    half_win = (NAT_KH // 2) * GRID_W
    specs = [("first", 0, 0)]
    specs += [("mid", j * NAT_Q, j * NAT_Q - half_win) for j in range(1, n_blocks - 1)]
    specs += [("last", SEQ - NAT_Q, SEQ - NAT_K)]

    def scores(n):
        _, q_tok, k_tok = specs[n]
        qb = q_ref[q_tok:q_tok + NAT_Q, :]
        s_ref[n % NAT_BUFFERS] = _dot_nt(qb, k_ref[k_tok:k_tok + NAT_K, :])
        sc_ref[n % NAT_BUFFERS] = _dot_nt(qb, kc)

    def finish(n):
        cls, q_tok, k_tok = specs[n]
        b = n % NAT_BUFFERS
        _nat_softmax_block(cls, s_ref.at[b], sc_ref.at[b], t2_ref, p_ref.at[b], pc_ref.at[b], m_ref.at[b])
        o = _dot(p_ref[b], vaug_ref[k_tok:k_tok + NAT_K, :]) + _dot(pc_ref[b], vaug_ref[SEQ:, :])
        ol_ref[q_tok:q_tok + NAT_Q, :] = (o[:, 0:HEAD_DIM] / o[:, HEAD_DIM:]).astype(BF16)

    scores(0)
    for n in range(n_blocks):
        if n + 1 < n_blocks:
            scores(n + 1)
        finish(n)


def _nat_attn(qkv, rpbw, ffn_weights, first_layer, n_layers):
    ctx_blk = ROWS_LAT // CTX_LEN
    lat = lambda n: pl.BlockSpec((None, SEQ, HEAD_DIM), lambda b, h: (n, b, h))
    ctx = lambda n: pl.BlockSpec((None, CTX_LEN, HEAD_DIM), lambda b, h: (n, ctx_blk + b, h))
    w_in, w_out, w_shapes = _weight_cast_specs(first_layer, n_layers, BATCH * NAT_HEADS,
                                               lambda b, h: b * NAT_HEADS + h)
    return pl.pallas_call(
        _nat_attn_kernel,
        grid=(BATCH, NAT_HEADS),
        in_specs=[lat(0), lat(1), lat(2), ctx(0), ctx(1), ctx(2),
                  pl.BlockSpec((None, 2 * NAT_KH, LANE), lambda b, h: (h, 0, 0))] + w_in,
        out_specs=[
            pl.BlockSpec((SEQ, HEAD_DIM), lambda b, h: (b, h)),
            pl.BlockSpec((CTX_LEN, HEAD_DIM), lambda b, h: (b, h)),
        ] + w_out,
        out_shape=[
            jax.ShapeDtypeStruct((ROWS_LAT, D_MODEL), BF16),
            jax.ShapeDtypeStruct((ROWS_CTX, D_MODEL), BF16),
        ] + w_shapes,
        scratch_shapes=[
            pltpu.VMEM((NAT_TILES, GRID_W, LANE), F32),
            pltpu.VMEM((NAT_BUFFERS, NAT_Q, NAT_K), F32),
            pltpu.VMEM((NAT_BUFFERS, NAT_Q, CTX_LEN), F32),
            pltpu.VMEM((NAT_BUFFERS, NAT_Q, NAT_K), BF16),
            pltpu.VMEM((NAT_BUFFERS, NAT_Q, CTX_LEN), BF16),
            pltpu.VMEM((NAT_BUFFERS, NAT_Q, LANE), F32),
            pltpu.VMEM((SEQ + CTX_LEN, 2 * HEAD_DIM), BF16),
        ],
        compiler_params=_params("parallel", "parallel"),
        name="nat_attn",
    )(qkv, qkv, qkv, qkv, qkv, qkv, rpbw, *ffn_weights)


def _nat_rpb_pairs(rpb):
    padded = jnp.pad(rpb, ((0, 0), (0, 2), (0, GRID_W - (2 * NAT_KW - 1))))
    return jnp.concatenate([padded[:, :2 * NAT_KH], padded[:, 1:2 * NAT_KH + 1]], axis=-1)


def _rope_tables(tm):
    t = jnp.arange(SEQ)
    row = (t // GRID_W).astype(F32)
    col = (t % GRID_W).astype(F32)
    inv = ROPE_THETA ** (-jnp.arange(0, ROPE_AXIS_DIM, 2, dtype=F32) / ROPE_AXIS_DIM)
    ang = jnp.concatenate([row[:, None] * inv, col[:, None] * inv], axis=-1)
    cosf = jnp.repeat(jnp.cos(ang), 2, axis=-1)
    sign = jnp.tile(jnp.array([-1.0, 1.0], F32), HEAD_DIM // 2)
    sinf = jnp.repeat(jnp.sin(ang), 2, axis=-1) * sign
    cos_ext = jnp.concatenate([cosf, jnp.ones((tm, HEAD_DIM), F32)], axis=0)
    sin_ext = jnp.concatenate([sinf, jnp.zeros((tm, HEAD_DIM), F32)], axis=0)
    return cos_ext, sin_ext


def kernel(x, c, ctx, c_ctx, ada_w, ada_b, norm_g, ffn_w1, ffn_w3, ffn_w2, pool_w, pool_ls, gqa_wq, gqa_wk, gqa_wv, gqa_wo, gqa_qn, gqa_kn, nat_wq, nat_wk, nat_wv, nat_wo, nat_rpb, final_g):
    c8 =jnp.concatenate([c, c_ctx[None, :], jnp.zeros((8 - BATCH - 1, D_MODEL), F32)], axis=0)
    mods = _ada_all(c8, ada_w, ada_b)[:, :BATCH + 1].reshape(DEPTH, BATCH + 1, 6, D_MODEL)
    pool_w_all = pool_w.astype(BF16)
    ffn_f32 = (ffn_w1, ffn_w3, ffn_w2)
    ffn_bf16 = {}
    gqa_cast_layers, nat_cast_layers = (1, 2), (3, 1)
    attn_layers = [i for i in range(1, DEPTH) if i % N_MIXERS != 0]
    attn_bf16 = {}

    xa = None
    for i in range(DEPTH):
        kind, j = i % N_MIXERS, i // N_MIXERS
        last = i == DEPTH - 1
        n_rows = ROWS_LAT if last else ROWS_ALL
        g1, g2 = norm_g[i, 0], norm_g[i, 1]
        if kind == 0:
            x_lat, x_ctx = (x.reshape(ROWS_LAT, D_MODEL), ctx.reshape(ROWS_CTX, D_MODEL)) if i == 0 else (xa, xa)
            if i in ffn_bf16:
                xa, h2 = _pool_layer(x_lat, x_ctx, mods[i], g1, g2, pool_w_all, j, pool_ls[j], n_rows)
            else:
                xa, h2, *cast = _pool_layer(x_lat, x_ctx, mods[i], g1, g2, pool_w_all, j, pool_ls[j], n_rows,
                                            cast_ffn_layer=i, ffn_weights=ffn_f32)
                ffn_bf16[i] = (tuple(cast), 0)
        else:
            w_qkv, wo = attn_bf16[i]
            if kind == 1:
                cos_ext, sin_ext = _rope_tables(GQA_PROJ_ROWS)
                q, k, v = _gqa_proj(xa, mods[i], g1, w_qkv, gqa_qn[j], gqa_kn[j], cos_ext, sin_ext)
                first, count = gqa_cast_layers
                o_lat, o_ctx, *cast = _gqa_attn(q, k, v, ffn_f32, first, count)
            else:
                qkv = _nat_proj(xa, mods[i], g1, w_qkv)
                first, count = nat_cast_layers
                o_lat, o_ctx, *cast = _nat_attn(qkv, _nat_rpb_pairs(nat_rpb[j]), ffn_f32, first, count)
            for n in range(count):
                ffn_bf16[first + n] = (tuple(cast), n)
            xa, h2 = _attn_out_layer(o_lat, o_ctx, xa, mods[i], wo, g2)
        (w1_all, w3_all, w2_all), idx = ffn_bf16[i]
        next_norm = None
        side_casts = []
        if i == 0:
            for a in attn_layers:
                ja = a // N_MIXERS
                if a % N_MIXERS == 1:
                    side_casts += [("concat", ja, [gqa_wq, gqa_wk, gqa_wv]), ("concat", ja, [gqa_wo])]
                else:
                    side_casts += [("stack", ja, [nat_wq, nat_wk, nat_wv]), ("concat", ja, [nat_wo])]
        outs = _ffn_layer(xa, h2, mods[i], w1_all, w3_all, w2_all, idx, final_g, n_rows, last,
                          next_norm=next_norm, side_casts=side_casts)
        if next_norm is None and not side_casts:
            xa = outs
        else:
            xa, rest = outs[0], list(outs[1:])
            if next_norm is not None:
                h1 = rest.pop(0)
            for n, a in enumerate(attn_layers if side_casts else ()):
                attn_bf16[a] = (rest[2 * n], rest[2 * n + 1])
    return xa.reshape(BATCH, SEQ, D_MODEL)
```
